```python
import math
import jax, jax.numpy as jnp
from jax import lax
import numpy as np

D_MODEL = 2048
BATCH = 8
SEQ = 2048
DEPTH = 1

N_QK_HEADS = 16
N_V_HEADS = 32
HEAD_K = 128
HEAD_V = 128
CONV_WIDTH = 4
CHUNK = 64
QK_WIDTH = N_QK_HEADS * HEAD_K
V_WIDTH = N_V_HEADS * HEAD_V
CONV_CH = 2 * QK_WIDTH + V_WIDTH

POOL_WINDOWS = (2, 4, 8, 16)
N_POOL_GROUPS = len(POOL_WINDOWS)
POOL_WIDTH = D_MODEL
POOL_GROUP = POOL_WIDTH // N_POOL_GROUPS

N_EXPERTS = 32
TOP_K = 4
D_EXPERT = D_MODEL
SWIGLU_ALPHA = 1.702
SWIGLU_LIMIT = 7.0
EXPERT_BLOCK = 128

RMS_EPS = 1e-6
L2_EPS = 1e-6

IN_SIZES = (CONV_CH, V_WIDTH, N_V_HEADS, N_V_HEADS, POOL_WIDTH, D_MODEL, D_MODEL)
IN_WIDTH = sum(IN_SIZES)
SPLIT_POINTS = tuple(int(s) for s in np.cumsum(IN_SIZES)[:-1])

kernel_name = 'gated_delta_pool_moe_hybrid'


def rmsnorm(x, w):
    xf = x.astype(jnp.float32)
    r = xf * lax.rsqrt(jnp.mean(xf * xf, axis=-1, keepdims=True) + RMS_EPS)
    return (r * w.astype(jnp.float32)).astype(x.dtype)


def l2norm(x):
    return x * lax.rsqrt(jnp.sum(x * x, axis=-1, keepdims=True) + L2_EPS)


def causal_depthwise_conv(u, w):
    S = u.shape[1]
    up = jnp.pad(u, ((0, 0), (CONV_WIDTH - 1, 0), (0, 0)))
    return sum(up[:, i:i + S] * w[i] for i in range(CONV_WIDTH))


def chunk_gated_delta_rule(q, k, v, g, beta):
    B, S, H, DK = q.shape
    DV = v.shape[-1]
    N = S // CHUNK
    f32 = jnp.float32
    q, k, v, g, beta = (t.astype(f32) for t in (q, k, v, g, beta))
    q = l2norm(q) * (DK ** -0.5)
    k = l2norm(k)

    def chunks(t):
        t = t.reshape((B, N, CHUNK, H) + t.shape[3:])
        return jnp.moveaxis(t, 3, 1)

    q, k, v, g, beta = (chunks(t) for t in (q, k, v, g, beta))
    g_cum = jnp.cumsum(g, axis=-1)
    idx = jnp.arange(CHUNK)
    incl = idx[:, None] >= idx[None, :]
    strict = idx[:, None] > idx[None, :]
    decay = jnp.exp(jnp.where(incl, g_cum[..., :, None] - g_cum[..., None, :], -jnp.inf))
    kk = jnp.einsum('bhnid,bhnjd->bhnij', k, k)
    lower = jnp.where(strict, kk * decay * beta[..., :, None], 0.0)
    a_mat = lower + jnp.eye(CHUNK, dtype=f32)
    rhs = jnp.concatenate([v * beta[..., None], k * (beta * jnp.exp(g_cum))[..., None]], axis=-1)
    sol = lax.linalg.triangular_solve(a_mat, rhs, left_side=True, lower=True, unit_diagonal=True)
    u, w = sol[..., :DV], sol[..., DV:]
    qk = jnp.einsum('bhnid,bhnjd->bhnij', q, k) * decay
    q_dec = q * jnp.exp(g_cum)[..., None]
    k_dec = k * jnp.exp(g_cum[..., -1:] - g_cum)[..., None]
    chunk_decay = jnp.exp(g_cum[..., -1])
    xs = tuple(jnp.moveaxis(t, 2, 0) for t in (qk, q_dec, k_dec, u, w, chunk_decay))

    def step(state, inp):
        qk_c, q_c, k_c, u_c, w_c, d_c = inp
        v_new = u_c - jnp.einsum('bhck,bhkv->bhcv', w_c, state)
        o_c = jnp.einsum('bhck,bhkv->bhcv', q_c, state) + jnp.einsum('bhij,bhjv->bhiv', qk_c, v_new)
        state = state * d_c[..., None, None] + jnp.einsum('bhck,bhcv->bhkv', k_c, v_new)
        return state, o_c

    state0 = jnp.zeros((B, H, DK, DV), f32)
    _, o = lax.scan(step, state0, xs)
    return jnp.transpose(o, (1, 0, 3, 2, 4)).reshape(B, S, H, DV)


def multiscale_pool(p, w_pool, pool_scale):
    B, S, _ = p.shape
    pg = p.reshape(B, S, N_POOL_GROUPS, POOL_GROUP).astype(jnp.float32)
    c = jnp.pad(jnp.cumsum(pg, axis=1), ((0, 0), (1, 0), (0, 0), (0, 0)))
    pos = jnp.arange(1, S + 1, dtype=jnp.float32)
    groups = []
    for gi, win in enumerate(POOL_WINDOWS):
        cg = c[:, :, gi]
        lower = jnp.pad(cg, ((0, 0), (win - 1, 0), (0, 0)))[:, :S]
        count = jnp.minimum(pos, float(win))[None, :, None]
        groups.append((cg[:, 1:] - lower) / count - pg[:, :, gi])
    mixed = jnp.stack(groups, axis=2)
    y = jnp.einsum('bsgc,gce->bsge', mixed, w_pool.astype(jnp.float32))
    y = y * pool_scale.astype(jnp.float32).reshape(N_POOL_GROUPS, POOL_GROUP)
    return y.reshape(B, S, POOL_WIDTH).astype(p.dtype)


def hybrid_mixer(xn, w_in, b_gate, conv_w, A_log, dt_bias, onorm_w,
                 w_branch_a, w_pool, pool_scale, w_branch_p, w_out):
    B, S, _ = xn.shape
    proj = jnp.einsum('bsd,de->bse', xn, w_in)
    qkv, z, a, b, p, ga, gp = jnp.split(proj, SPLIT_POINTS, axis=-1)
    qkv = jax.nn.silu(causal_depthwise_conv(qkv, conv_w))
    q, k, v = jnp.split(qkv, [QK_WIDTH, 2 * QK_WIDTH], axis=-1)
    rep = N_V_HEADS // N_QK_HEADS
    q = jnp.repeat(q.reshape(B, S, N_QK_HEADS, HEAD_K), rep, axis=2)
    k = jnp.repeat(k.reshape(B, S, N_QK_HEADS, HEAD_K), rep, axis=2)
    v = v.reshape(B, S, N_V_HEADS, HEAD_V)
    beta = jax.nn.sigmoid(b.astype(jnp.float32))
    g = -jnp.exp(A_log.astype(jnp.float32)) * jax.nn.softplus(a.astype(jnp.float32) + dt_bias.astype(jnp.float32))
    o = chunk_gated_delta_rule(q, k, v, g, beta)
    o = rmsnorm(o, onorm_w) * jax.nn.silu(z.reshape(B, S, N_V_HEADS, HEAD_V).astype(jnp.float32))
    o_a = o.reshape(B, S, V_WIDTH).astype(xn.dtype)
    o_p = multiscale_pool(p, w_pool, pool_scale)
    gate_a = jax.nn.sigmoid(ga + b_gate[:D_MODEL])
    gate_p = jax.nn.sigmoid(gp + b_gate[D_MODEL:])
    h = gate_a * jnp.einsum('bsv,vd->bsd', o_a, w_branch_a) + gate_p * jnp.einsum('bsp,pd->bsd', o_p, w_branch_p)
    return jnp.einsum('bsd,de->bse', h, w_out)


def clamped_swiglu(h):
    x_glu = jnp.minimum(h[..., ::2], SWIGLU_LIMIT)
    x_lin = jnp.clip(h[..., 1::2], -SWIGLU_LIMIT, SWIGLU_LIMIT)
    return x_glu * jax.nn.sigmoid(SWIGLU_ALPHA * x_glu) * (x_lin + 1.0)


def moe_ffn(xn, router_w, router_b, w1, b1, w2, b2):
    B, S, D = xn.shape
    T = B * S
    xf = xn.reshape(T, D)
    logits = jnp.einsum('td,de->te', xf, router_w).astype(jnp.float32) + router_b.astype(jnp.float32)
    top_vals, top_idx = lax.top_k(logits, TOP_K)
    gates = jax.nn.softmax(top_vals, axis=-1)
    A = T * TOP_K
    flat_e = top_idx.reshape(A).astype(jnp.int32)
    flat_tok = jnp.arange(A, dtype=jnp.int32) // TOP_K
    flat_g = gates.reshape(A)
    order = jnp.argsort(flat_e)
    se, stok, sg = flat_e[order], flat_tok[order], flat_g[order]
    counts = jnp.zeros((N_EXPERTS,), jnp.int32).at[flat_e].add(1)
    padded = (counts + EXPERT_BLOCK - 1) // EXPERT_BLOCK * EXPERT_BLOCK
    start = jnp.cumsum(counts) - counts
    pend = jnp.cumsum(padded)
    pstart = pend - padded
    dest = pstart[se] + (jnp.arange(A, dtype=jnp.int32) - start[se])
    R = (A + EXPERT_BLOCK - 1) // EXPERT_BLOCK * EXPERT_BLOCK + N_EXPERTS * EXPERT_BLOCK
    n_blocks = R // EXPERT_BLOCK
    x_buf = jnp.zeros((R, D), xf.dtype).at[dest].set(xf[stok])
    block_start = jnp.arange(n_blocks, dtype=jnp.int32) * EXPERT_BLOCK
    block_expert = jnp.minimum(jnp.searchsorted(pend, block_start, side='right'), N_EXPERTS - 1)

    def expert_block(args):
        xb, e = args
        h = jnp.einsum('cd,df->cf', xb, w1[e]) + b1[e]
        return jnp.einsum('cf,fd->cd', clamped_swiglu(h), w2[e]) + b2[e]

    y_buf = lax.map(expert_block, (x_buf.reshape(n_blocks, EXPERT_BLOCK, D), block_expert)).reshape(R, D)
    y = y_buf[dest] * sg[:, None].astype(y_buf.dtype)
    out = jax.ops.segment_sum(y, stok, num_segments=T)
    return out.reshape(B, S, D).astype(xn.dtype)


def setup_inputs(seed: int = 0) -> dict:
    key = jax.random.key(seed)
    ks = jax.random.split(key, 24)
    f32 = jnp.float32
    L = DEPTH

    def nrm(k, shape, scale):
        return jax.random.normal(k, shape, f32) * scale

    x = nrm(ks[0], (BATCH, SEQ, D_MODEL), 1.0)
    norm1_w = 1.0 + nrm(ks[1], (L, D_MODEL), 0.02)
    w_in = nrm(ks[2], (L, D_MODEL, IN_WIDTH), D_MODEL ** -0.5)
    b_gate = nrm(ks[3], (L, 2 * D_MODEL), 0.02)
    conv_w = nrm(ks[4], (L, CONV_WIDTH, CONV_CH), CONV_WIDTH ** -0.5)
    A_log = jnp.log(jax.random.uniform(ks[5], (L, N_V_HEADS), f32, 1.0, 16.0))
    dt = jnp.exp(jax.random.uniform(ks[6], (L, N_V_HEADS), f32, math.log(1e-3), math.log(1e-1)))
    dt_bias = dt + jnp.log(-jnp.expm1(-dt))
    onorm_w = 1.0 + nrm(ks[7], (L, HEAD_V), 0.02)
    w_branch_a = nrm(ks[8], (L, V_WIDTH, D_MODEL), V_WIDTH ** -0.5)
    w_pool = nrm(ks[9], (L, N_POOL_GROUPS, POOL_GROUP, POOL_GROUP), POOL_GROUP ** -0.5)
    pool_scale = 1.0 + nrm(ks[10], (L, POOL_WIDTH), 0.1)
    w_branch_p = nrm(ks[11], (L, POOL_WIDTH, D_MODEL), POOL_WIDTH ** -0.5)
    w_out = nrm(ks[12], (L, D_MODEL, D_MODEL), D_MODEL ** -0.5)
    norm2_w = 1.0 + nrm(ks[13], (L, D_MODEL), 0.02)
    router_w = nrm(ks[14], (L, D_MODEL, N_EXPERTS), D_MODEL ** -0.5)
    router_b = nrm(ks[15], (L, N_EXPERTS), 0.01)
    w1 = nrm(ks[16], (L, N_EXPERTS, D_MODEL, 2 * D_EXPERT), D_MODEL ** -0.5)
    b1 = nrm(ks[17], (L, N_EXPERTS, 2 * D_EXPERT), 0.02)
    w2 = nrm(ks[18], (L, N_EXPERTS, D_EXPERT, D_MODEL), D_EXPERT ** -0.5)
    b2 = nrm(ks[19], (L, N_EXPERTS, D_MODEL), 0.02)
    norm_f_w = 1.0 + nrm(ks[20], (D_MODEL,), 0.02)
    return {'x': x, 'norm1_w': norm1_w, 'w_in': w_in, 'b_gate': b_gate, 'conv_w': conv_w,
            'A_log': A_log, 'dt_bias': dt_bias, 'onorm_w': onorm_w, 'w_branch_a': w_branch_a,
            'w_pool': w_pool, 'pool_scale': pool_scale, 'w_branch_p': w_branch_p, 'w_out': w_out,
            'norm2_w': norm2_w, 'router_w': router_w, 'router_b': router_b,
            'w1': w1, 'b1': b1, 'w2': w2, 'b2': b2, 'norm_f_w': norm_f_w}


def reference(x, norm1_w, w_in, b_gate, conv_w, A_log, dt_bias, onorm_w, w_branch_a,
              w_pool, pool_scale, w_branch_p, w_out, norm2_w, router_w, router_b,
              w1, b1, w2, b2, norm_f_w):
    for l in range(DEPTH):
        x = x + hybrid_mixer(rmsnorm(x, norm1_w[l]), w_in[l], b_gate[l], conv_w[l], A_log[l],
                             dt_bias[l], onorm_w[l], w_branch_a[l], w_pool[l], pool_scale[l],
                             w_branch_p[l], w_out[l])
        x = x + moe_ffn(rmsnorm(x, norm2_w[l]), router_w[l], router_b[l], w1[l], b1[l], w2[l], b2[l])
    return rmsnorm(x, norm_f_w)
```

```python
import functools
import math

import jax
import jax.numpy as jnp
from jax import lax
from jax.experimental import pallas as pl
from jax.experimental.pallas import tpu as pltpu

F32 = jnp.float32
BF16 = jnp.bfloat16

HEAD = 128
CHUNK = 64
CONV_WIDTH = 4
POOL_WINDOWS = (2, 4, 8, 16)
TOP_K = 4
SWIGLU_ALPHA = 1.702
SWIGLU_LIMIT = 7.0
RMS_EPS = 1e-6
L2_EPS = 1e-6
LANES = 128
HIGHEST = lax.Precision.HIGHEST
VMEM_LIMIT = 56 * 1024 * 1024


def _cparams(sem):
    return pltpu.CompilerParams(dimension_semantics=sem, vmem_limit_bytes=VMEM_LIMIT)


def _sigmoid(x):
    return 1.0 / (1.0 + jnp.exp(-x))


def _bdot(a, b):
    return jnp.dot(a.astype(BF16), b.astype(BF16), preferred_element_type=F32)


def _inproj_kernel(x_ref, nw_ref, w_ref, wab_ref, o_ref, ab_ref, xn_ref, *, rows):
    @pl.when(pl.program_id(1) == 0)
    def _():
        def body(r, c):
            r0 = pl.multiple_of(r * rows, rows)
            x = x_ref[pl.ds(r0, rows), :]
            inv = lax.rsqrt(jnp.mean(x * x, axis=-1, keepdims=True) + RMS_EPS)
            xn_ref[pl.ds(r0, rows), :] = (x * inv * nw_ref[...]).astype(BF16)
            return c
        lax.fori_loop(0, x_ref.shape[0] // rows, body, 0)
        ab_ref[...] = jnp.dot(xn_ref[...], wab_ref[...], preferred_element_type=F32)

    o_ref[...] = jnp.dot(xn_ref[...], w_ref[...], preferred_element_type=F32).astype(o_ref.dtype)


def _in_proj(x2, norm_w, w_main, w_ab, tm, tn):
    T, D = x2.shape
    N = w_main.shape[1]
    rows = min(128, tm)
    return pl.pallas_call(
        functools.partial(_inproj_kernel, rows=rows),
        grid=(T // tm, N // tn),
        in_specs=[
            pl.BlockSpec((tm, D), lambda i, j: (i, 0)),
            pl.BlockSpec((1, D), lambda i, j: (0, 0)),
            pl.BlockSpec((D, tn), lambda i, j: (0, j)),
            pl.BlockSpec((D, LANES), lambda i, j: (0, 0)),
        ],
        out_specs=[
            pl.BlockSpec((tm, tn), lambda i, j: (i, j)),
            pl.BlockSpec((tm, LANES), lambda i, j: (i, 0)),
        ],
        out_shape=[jax.ShapeDtypeStruct((T, N), BF16), jax.ShapeDtypeStruct((T, LANES), F32)],
        scratch_shapes=[pltpu.VMEM((tm, D), BF16)],
        compiler_params=_cparams(("arbitrary", "arbitrary")),
        name="in_proj",
    )(x2, norm_w.reshape(1, D), w_main, w_ab)


def _inv_unit_lower(L, eye):
    P = eye - L
    M = L
    for _ in range(int(math.log2(CHUNK)) - 1):
        M = _bdot(M, M)
        P = P + _bdot(P, M)
    return P


def _delta_kernel(q_ref, k_ref, v_ref, z_ref, ab_ref, cwq_ref, cwk_ref, cwv_ref,
                  alog_ref, dtb_ref, onw_ref, o_ref,
                  s_ref, tq_ref, tk_ref, tv_ref, *, G, H):
    S = q_ref.shape[0]
    C = CHUNK
    j = pl.program_id(1)

    s_ref[...] = jnp.zeros_like(s_ref)
    tq_ref[...] = jnp.zeros_like(tq_ref)
    tk_ref[...] = jnp.zeros_like(tk_ref)
    tv_ref[...] = jnp.zeros_like(tv_ref)

    row = lax.broadcasted_iota(jnp.int32, (C, C), 0)
    col = lax.broadcasted_iota(jnp.int32, (C, C), 1)
    incl = row >= col
    strict = row > col
    eye = jnp.where(row == col, 1.0, 0.0).astype(F32)
    tril = jnp.where(incl, 1.0, 0.0).astype(F32)
    lane = lax.broadcasted_iota(jnp.int32, (C, LANES), 1)
    shift = lax.rem(LANES - 2 * G * j, LANES)

    def conv_silu(raw_ref, tail_ref, cw_ref, r0, c0, width):
        cur = raw_ref[pl.ds(r0, C), c0:c0 + width].astype(F32)
        ext = jnp.concatenate([tail_ref[:, c0:c0 + width], cur], axis=0)
        y = cur * cw_ref[CONV_WIDTH - 1:CONV_WIDTH, c0:c0 + width]
        for i in range(CONV_WIDTH - 1):
            sh = CONV_WIDTH - 1 - i
            y = y + pltpu.roll(ext, sh, 0)[8:8 + C] * cw_ref[i:i + 1, c0:c0 + width]
        tail_ref[:, c0:c0 + width] = cur[C - 8:C]
        return y * _sigmoid(y)

    def chunk(n, carry):
        r0 = pl.multiple_of(n * C, C)
        ab = ab_ref[pl.ds(r0, C), :]
        sp_in = ab + dtb_ref[...]
        softplus = jnp.maximum(sp_in, 0.0) + jnp.log(1.0 + jnp.exp(-jnp.abs(sp_in)))
        gval = -jnp.exp(alog_ref[...]) * softplus
        vals = jnp.where(lane < H, gval, _sigmoid(ab))
        vals = pltpu.roll(vals, shift, 1)
        gc = jnp.dot(tril, vals, preferred_element_type=F32, precision=HIGHEST)
        gct = gc.T

        for h in range(G):
            q = conv_silu(q_ref, tq_ref, cwq_ref, r0, h * HEAD, HEAD)
            k = conv_silu(k_ref, tk_ref, cwk_ref, r0, h * HEAD, HEAD)
            v = conv_silu(v_ref, tv_ref, cwv_ref, r0, 2 * h * HEAD, 2 * HEAD)
            q = q * lax.rsqrt(jnp.sum(q * q, axis=-1, keepdims=True) + L2_EPS) * (HEAD ** -0.5)
            k = k * lax.rsqrt(jnp.sum(k * k, axis=-1, keepdims=True) + L2_EPS)
            qk_cat = jnp.concatenate([q, k], axis=0).astype(BF16)
            kb = k.astype(BF16)
            qkk = lax.dot_general(qk_cat, kb, (((1,), (1,)), ((), ())),
                                  preferred_element_type=F32)
            qk = qkk[:C]
            kk = qkk[C:]
            s_cat = jnp.concatenate([s_ref[2 * h], s_ref[2 * h + 1]], axis=1)
            ks = jnp.dot(qk_cat, s_cat.astype(BF16), preferred_element_type=F32)
            vnd = []
            dlast = []
            for e in range(2):
                hv = 2 * h + e
                lo, hi = e * HEAD, (e + 1) * HEAD
                gcol = jnp.broadcast_to(gc[:, hv:hv + 1], (C, LANES))
                bcol = jnp.broadcast_to(vals[:, H + hv:H + hv + 1], (C, LANES))
                grow = jnp.broadcast_to(gct[hv:hv + 1, :], (C, C))
                dec = jnp.exp(jnp.where(incl, gcol[:, :C] - grow, -jnp.inf))
                lmat = jnp.where(strict, kk * dec * bcol[:, :C], 0.0)
                tmat = _inv_unit_lower(lmat, eye)
                aqk = qk * dec
                eg = jnp.exp(gcol)
                rhs = bcol * (v[:, lo:hi] - eg * ks[C:, lo:hi])
                vnew = _bdot(tmat, rhs)
                o = eg * ks[:C, lo:hi] + _bdot(aqk, vnew)
                glast = gcol[C - 1:C, :]
                vnd.append(jnp.exp(glast - gcol) * vnew)
                dlast.append(jnp.exp(glast))
                zz = z_ref[pl.ds(r0, C), hv * HEAD:(hv + 1) * HEAD].astype(F32)
                on = o * lax.rsqrt(jnp.mean(o * o, axis=-1, keepdims=True) + RMS_EPS)
                on = on * onw_ref[...] * (zz * _sigmoid(zz))
                o_ref[pl.ds(r0, C), hv * HEAD:(hv + 1) * HEAD] = on.astype(o_ref.dtype)
            vnd_cat = jnp.concatenate(vnd, axis=1).astype(BF16)
            upd = lax.dot_general(kb, vnd_cat, (((0,), (0,)), ((), ())),
                                  preferred_element_type=F32)
            for e in range(2):
                hv = 2 * h + e
                s_ref[hv] = s_ref[hv] * dlast[e] + upd[:, e * HEAD:(e + 1) * HEAD]
        return carry

    lax.fori_loop(0, S // C, chunk, 0)


def _delta(proj3, ab3, conv_w, a_log, dt_bias, onorm_w, qk_width, v_width, G):
    B, S, _ = proj3.shape
    H = a_log.shape[0]
    n_groups = qk_width // (HEAD * G)
    qw, vw = HEAD * G, 2 * HEAD * G
    k_blk0 = qk_width // qw
    v_blk0 = (2 * qk_width) // vw
    z_blk0 = (2 * qk_width + v_width) // vw
    assert (2 * qk_width) % vw == 0 and (2 * qk_width + v_width) % vw == 0
    assert 2 * H <= LANES and S % CHUNK == 0
    pad = lambda p: jnp.zeros((1, LANES), F32).at[0, :H].set(p.astype(F32))
    conv_w = conv_w.astype(F32)
    return pl.pallas_call(
        functools.partial(_delta_kernel, G=G, H=H),
        grid=(B, n_groups),
        in_specs=[
            pl.BlockSpec((None, S, qw), lambda b, j: (b, 0, j)),
            pl.BlockSpec((None, S, qw), lambda b, j: (b, 0, k_blk0 + j)),
            pl.BlockSpec((None, S, vw), lambda b, j: (b, 0, v_blk0 + j)),
            pl.BlockSpec((None, S, vw), lambda b, j: (b, 0, z_blk0 + j)),
            pl.BlockSpec((None, S, LANES), lambda b, j: (b, 0, 0)),
            pl.BlockSpec((CONV_WIDTH, qw), lambda b, j: (0, j)),
            pl.BlockSpec((CONV_WIDTH, qw), lambda b, j: (0, k_blk0 + j)),
            pl.BlockSpec((CONV_WIDTH, vw), lambda b, j: (0, v_blk0 + j)),
            pl.BlockSpec((1, LANES), lambda b, j: (0, 0)),
            pl.BlockSpec((1, LANES), lambda b, j: (0, 0)),
            pl.BlockSpec((1, HEAD), lambda b, j: (0, 0)),
        ],
        out_specs=pl.BlockSpec((None, S, vw), lambda b, j: (b, 0, j)),
        out_shape=jax.ShapeDtypeStruct((B, S, v_width), BF16),
        scratch_shapes=[
            pltpu.VMEM((2 * G, HEAD, HEAD), F32),
            pltpu.VMEM((8, qw), F32),
            pltpu.VMEM((8, qw), F32),
            pltpu.VMEM((8, vw), F32),
        ],
        compiler_params=_cparams(("arbitrary", "arbitrary")),
        name="delta",
    )(proj3, proj3, proj3, proj3, ab3, conv_w, conv_w, conv_w,
      pad(a_log), pad(dt_bias), onorm_w.reshape(1, HEAD).astype(F32))


POOL_HALO = 16


def _pool_kernel(p_ref, w_ref, sc_ref, o_ref, pad_ref, *, rt):
    S = p_ref.shape[0]
    g = pl.program_id(1)
    pad_ref[0:POOL_HALO, :] = jnp.zeros((POOL_HALO, pad_ref.shape[1]), F32)
    pad_ref[POOL_HALO:, :] = p_ref[...].astype(F32)
    win = jnp.left_shift(2, g).astype(F32)

    def body(r, c):
        r0 = pl.multiple_of(r * rt, rt)
        x = pad_ref[pl.ds(r0, rt + POOL_HALO), :]
        y1 = x + pltpu.roll(x, 1, 0)
        y2 = y1 + pltpu.roll(y1, 2, 0)
        y3 = y2 + pltpu.roll(y2, 4, 0)
        y4 = y3 + pltpu.roll(y3, 8, 0)
        ysel = jnp.where(g == 0, y1, jnp.where(g == 1, y2, jnp.where(g == 2, y3, y4)))
        pos = (r0 + 1 + lax.broadcasted_iota(jnp.int32, (rt, 1), 0)).astype(F32)
        cnt = jnp.minimum(pos, win)
        mixed = ysel[POOL_HALO:] / cnt - x[POOL_HALO:]
        out = _bdot(mixed, w_ref[...]) * sc_ref[...]
        o_ref[pl.ds(r0, rt), :] = out.astype(o_ref.dtype)
        return c

    lax.fori_loop(0, S // rt, body, 0)


def _pool(proj3, w_pool, pool_scale, p_off, d_model):
    B, S, _ = proj3.shape
    ng, cg, _ = w_pool.shape
    assert POOL_WINDOWS == tuple(2 << i for i in range(ng)) and max(POOL_WINDOWS) <= POOL_HALO
    assert p_off % cg == 0
    blk0 = p_off // cg
    rt = min(256, S)
    return pl.pallas_call(
        functools.partial(_pool_kernel, rt=rt),
        grid=(B, ng),
        in_specs=[
            pl.BlockSpec((None, S, cg), lambda b, g: (b, 0, blk0 + g)),
            pl.BlockSpec((None, cg, cg), lambda b, g: (g, 0, 0)),
            pl.BlockSpec((1, cg), lambda b, g: (0, g)),
        ],
        out_specs=pl.BlockSpec((None, S, cg), lambda b, g: (b, 0, g)),
        out_shape=jax.ShapeDtypeStruct((B, S, d_model), BF16),
        scratch_shapes=[pltpu.VMEM((S + POOL_HALO, cg), F32)],
        compiler_params=_cparams(("arbitrary", "arbitrary")),
        name="pool",
    )(proj3, w_pool.astype(BF16), pool_scale.reshape(1, -1).astype(F32))


def _branch_kernel(oa_ref, op_ref, wa_ref, wp_ref, ga_ref, gp_ref, bga_ref, bgp_ref, h_ref):
    a = jnp.dot(oa_ref[...], wa_ref[...], preferred_element_type=F32)
    p = jnp.dot(op_ref[...], wp_ref[...], preferred_element_type=F32)
    ga = _sigmoid(ga_ref[...].astype(F32) + bga_ref[...])
    gp = _sigmoid(gp_ref[...].astype(F32) + bgp_ref[...])
    h_ref[...] = (ga * a + gp * p).astype(h_ref.dtype)


def _branch(o_a, o_p, proj, w_a, w_p, b_gate, ga_off, gp_off, tm, tn):
    T, V = o_a.shape
    D = w_a.shape[1]
    assert ga_off % tn == 0 and gp_off % tn == 0
    ga0, gp0 = ga_off // tn, gp_off // tn
    nd = D // tn
    bg = b_gate.reshape(1, 2 * D).astype(F32)
    return pl.pallas_call(
        _branch_kernel,
        grid=(T // tm, nd),
        in_specs=[
            pl.BlockSpec((tm, V), lambda i, j: (i, 0)),
            pl.BlockSpec((tm, D), lambda i, j: (i, 0)),
            pl.BlockSpec((V, tn), lambda i, j: (0, j)),
            pl.BlockSpec((D, tn), lambda i, j: (0, j)),
            pl.BlockSpec((tm, tn), lambda i, j: (i, ga0 + j)),
            pl.BlockSpec((tm, tn), lambda i, j: (i, gp0 + j)),
            pl.BlockSpec((1, tn), lambda i, j: (0, j)),
            pl.BlockSpec((1, tn), lambda i, j: (0, nd + j)),
        ],
        out_specs=pl.BlockSpec((tm, tn), lambda i, j: (i, j)),
        out_shape=jax.ShapeDtypeStruct((T, D), BF16),
        compiler_params=_cparams(("arbitrary", "arbitrary")),
        name="branch",
    )(o_a, o_p, w_a, w_p, proj, proj, bg, bg)


def _pack_halves(xb):
    half = xb.shape[1] // 2
    lo = lax.bitcast_convert_type(xb[:, :half], jnp.uint32)
    hi = lax.bitcast_convert_type(xb[:, half:], jnp.uint32)
    return jnp.right_shift(lo, jnp.uint32(16)) | (hi & jnp.uint32(0xFFFF0000))


def _unpack_halves(u):
    lo = lax.bitcast_convert_type(jnp.left_shift(u, jnp.uint32(16)), F32)
    hi = lax.bitcast_convert_type(u & jnp.uint32(0xFFFF0000), F32)
    return lo, hi


def _post_kernel(x_ref, h_ref, wo_ref, nw_ref, rw_ref, rb_ref,
                 x1_ref, xp_ref, idx_ref, gate_ref, *, n_experts):
    x1 = x_ref[...] + jnp.dot(h_ref[...], wo_ref[...], preferred_element_type=F32)
    x1_ref[...] = x1
    xn = x1 * lax.rsqrt(jnp.mean(x1 * x1, axis=-1, keepdims=True) + RMS_EPS) * nw_ref[...]
    xp_ref[...] = _pack_halves(xn.astype(BF16).astype(F32))
    logits = jnp.dot(xn, rw_ref[...], preferred_element_type=F32, precision=HIGHEST) + rb_ref[...]
    lane = lax.broadcasted_iota(jnp.int32, logits.shape, 1)
    work = jnp.where(lane < n_experts, logits, -jnp.inf)
    vals, idxs = [], []
    for _ in range(TOP_K):
        m = jnp.max(work, axis=-1, keepdims=True)
        idx = jnp.min(jnp.where(work == m, lane, LANES), axis=-1, keepdims=True)
        vals.append(m)
        idxs.append(idx)
        work = jnp.where(lane == idx, -jnp.inf, work)
    ex = [jnp.exp(v - vals[0]) for v in vals]
    den = ex[0]
    for e in ex[1:]:
        den = den + e
    idx_out = jnp.zeros(logits.shape, jnp.int32)
    gate_out = jnp.zeros(logits.shape, F32)
    for k in range(TOP_K):
        idx_out = jnp.where(lane == k, idxs[k], idx_out)
        gate_out = jnp.where(lane == k, ex[k] / den, gate_out)
    idx_ref[...] = idx_out
    gate_ref[...] = gate_out


def _post(x2, h, w_out, norm2_w, router_w, router_b, tm):
    T, D = x2.shape
    E = router_w.shape[1]
    rw = jnp.zeros((D, LANES), F32).at[:, :E].set(router_w.astype(F32))
    rb = jnp.zeros((1, LANES), F32).at[0, :E].set(router_b.astype(F32))
    return pl.pallas_call(
        functools.partial(_post_kernel, n_experts=E),
        grid=(T // tm,),
        in_specs=[
            pl.BlockSpec((tm, D), lambda i: (i, 0)),
            pl.BlockSpec((tm, D), lambda i: (i, 0)),
            pl.BlockSpec((D, D), lambda i: (0, 0)),
            pl.BlockSpec((1, D), lambda i: (0, 0)),
            pl.BlockSpec((D, LANES), lambda i: (0, 0)),
            pl.BlockSpec((1, LANES), lambda i: (0, 0)),
        ],
        out_specs=[
            pl.BlockSpec((tm, D), lambda i: (i, 0)),
            pl.BlockSpec((tm, D // 2), lambda i: (i, 0)),
            pl.BlockSpec((tm, LANES), lambda i: (i, 0)),
            pl.BlockSpec((tm, LANES), lambda i: (i, 0)),
        ],
        out_shape=[
            jax.ShapeDtypeStruct((T, D), F32),
            jax.ShapeDtypeStruct((T, D // 2), jnp.uint32),
            jax.ShapeDtypeStruct((T, LANES), jnp.int32),
            jax.ShapeDtypeStruct((T, LANES), F32),
        ],
        compiler_params=_cparams(("arbitrary",)),
        name="post",
    )(x2, h, w_out, norm2_w.reshape(1, D).astype(F32), rw, rb)


def _gather_kernel(src_ref, x_hbm, o_hbm, sem, *, tile):
    base = pl.program_id(0) * tile

    def start(r, c):
        tok = src_ref[0, 0, r]
        pltpu.make_async_copy(x_hbm.at[pl.ds(tok, 1)], o_hbm.at[pl.ds(base + r, 1)], sem).start()
        return c

    def wait(r, c):
        pltpu.make_async_copy(x_hbm.at[pl.ds(0, 1)], o_hbm.at[pl.ds(base + r, 1)], sem).wait()
        return c

    lax.fori_loop(0, tile, start, 0)
    lax.fori_loop(0, tile, wait, 0)


def _gather_rows(xp, row_src, tile):
    R = row_src.shape[0]
    W = xp.shape[1]
    return pl.pallas_call(
        functools.partial(_gather_kernel, tile=tile),
        grid=(R // tile,),
        in_specs=[
            pl.BlockSpec((1, 1, tile), lambda i: (i, 0, 0), memory_space=pltpu.SMEM),
            pl.BlockSpec(memory_space=pl.ANY),
        ],
        out_specs=pl.BlockSpec(memory_space=pl.ANY),
        out_shape=jax.ShapeDtypeStruct((R, W), xp.dtype),
        scratch_shapes=[pltpu.SemaphoreType.DMA(())],
        compiler_params=_cparams(("arbitrary",)),
        name="gather",
    )(row_src.reshape(R // tile, 1, tile), xp)


def _gmm1_kernel(be_ref, bv_ref, x_ref, wg_ref, wl_ref, bg_ref, bl_ref, o_ref):
    m = pl.program_id(1)

    @pl.when(bv_ref[m] == 1)
    def _():
        lo, hi = _unpack_halves(x_ref[...])
        half = lo.shape[1]
        lo = lo.astype(BF16)
        hi = hi.astype(BF16)

        def proj(w_ref, b_ref):
            return (jnp.dot(lo, w_ref[:half, :], preferred_element_type=F32)
                    + jnp.dot(hi, w_ref[half:, :], preferred_element_type=F32) + b_ref[...])

        x_glu = jnp.minimum(proj(wg_ref, bg_ref), SWIGLU_LIMIT)
        x_lin = jnp.clip(proj(wl_ref, bl_ref), -SWIGLU_LIMIT, SWIGLU_LIMIT)
        act = x_glu * _sigmoid(SWIGLU_ALPHA * x_glu) * (x_lin + 1.0)
        o_ref[...] = act.astype(o_ref.dtype)

    @pl.when(bv_ref[m] == 0)
    def _():
        o_ref[...] = jnp.zeros_like(o_ref)


def _gmm1(block_expert, block_valid, x_buf, w1g, w1l, b1g, b1l, bm, tn):
    R, half = x_buf.shape
    E, D, F = w1g.shape
    nb = R // bm
    grid_spec = pltpu.PrefetchScalarGridSpec(
        num_scalar_prefetch=2,
        grid=(F // tn, nb),
        in_specs=[
            pl.BlockSpec((bm, half), lambda n, m, be, bv: (m, 0)),
            pl.BlockSpec((None, D, tn), lambda n, m, be, bv: (be[m], 0, n)),
            pl.BlockSpec((None, D, tn), lambda n, m, be, bv: (be[m], 0, n)),
            pl.BlockSpec((None, 1, tn), lambda n, m, be, bv: (be[m], 0, n)),
            pl.BlockSpec((None, 1, tn), lambda n, m, be, bv: (be[m], 0, n)),
        ],
        out_specs=pl.BlockSpec((bm, tn), lambda n, m, be, bv: (m, n)),
    )
    return pl.pallas_call(
        _gmm1_kernel,
        grid_spec=grid_spec,
        out_shape=jax.ShapeDtypeStruct((R, F), BF16),
        compiler_params=_cparams(("arbitrary", "arbitrary")),
        name="gmm1",
    )(block_expert, block_valid, x_buf, w1g, w1l, b1g, b1l)


def _gmm2_kernel(be_ref, bv_ref, a_ref, w_ref, b_ref, o_ref):
    m = pl.program_id(1)

    @pl.when(bv_ref[m] == 1)
    def _():
        o_ref[...] = jnp.dot(a_ref[...], w_ref[...], preferred_element_type=F32) + b_ref[...]

    @pl.when(bv_ref[m] == 0)
    def _():
        o_ref[...] = jnp.zeros_like(o_ref)


def _gmm2(block_expert, block_valid, act, w2, b2, bm, tn):
    R, F = act.shape
    E, _, D = w2.shape
    nb = R // bm
    grid_spec = pltpu.PrefetchScalarGridSpec(
        num_scalar_prefetch=2,
        grid=(D // tn, nb),
        in_specs=[
            pl.BlockSpec((bm, F), lambda n, m, be, bv: (m, 0)),
            pl.BlockSpec((None, F, tn), lambda n, m, be, bv: (be[m], 0, n)),
            pl.BlockSpec((None, 1, tn), lambda n, m, be, bv: (be[m], 0, n)),
        ],
        out_specs=pl.BlockSpec((bm, tn), lambda n, m, be, bv: (m, n)),
    )
    return pl.pallas_call(
        _gmm2_kernel,
        grid_spec=grid_spec,
        out_shape=jax.ShapeDtypeStruct((R, D), F32),
        compiler_params=_cparams(("arbitrary", "arbitrary")),
        name="gmm2",
    )(block_expert, block_valid, act, w2, b2)


def _combine_kernel(dcur_ref, dnext_ref, gate_ref, x1_ref, nw_ref, y_hbm, o_ref, ybuf, sem, *, tm):
    i = pl.program_id(0)
    n = pl.num_programs(0)
    slot = lax.rem(i, 2)

    def copy(d_ref, s, r, k):
        return pltpu.make_async_copy(y_hbm.at[pl.ds(d_ref[0, 0, r * TOP_K + k], 1)],
                                     ybuf.at[s, k, pl.ds(r, 1)], sem.at[s])

    def issue(d_ref, s):
        def body(r, c):
            for k in range(TOP_K):
                copy(d_ref, s, r, k).start()
            return c
        lax.fori_loop(0, tm, body, 0)

    @pl.when(i == 0)
    def _():
        issue(dcur_ref, 0)

    @pl.when(i + 1 < n)
    def _():
        issue(dnext_ref, 1 - slot)

    def wait_body(r, c):
        for k in range(TOP_K):
            copy(dcur_ref, slot, r, k).wait()
        return c
    lax.fori_loop(0, tm, wait_body, 0)

    lane = lax.broadcasted_iota(jnp.int32, (tm, LANES), 1)
    gates = gate_ref[...]
    acc = x1_ref[...]
    for k in range(TOP_K):
        gk = jnp.sum(jnp.where(lane == k, gates, 0.0), axis=-1, keepdims=True)
        acc = acc + gk * ybuf[slot, k]
    out = acc * lax.rsqrt(jnp.mean(acc * acc, axis=-1, keepdims=True) + RMS_EPS) * nw_ref[...]
    o_ref[...] = out


def _combine(dest, gates, x1, norm_f_w, y_buf, tm):
    T, D = x1.shape
    nt = T // tm
    d3 = dest.reshape(nt, 1, tm * TOP_K)
    return pl.pallas_call(
        functools.partial(_combine_kernel, tm=tm),
        grid=(nt,),
        in_specs=[
            pl.BlockSpec((1, 1, tm * TOP_K), lambda i: (i, 0, 0), memory_space=pltpu.SMEM),
            pl.BlockSpec((1, 1, tm * TOP_K), lambda i: (jnp.minimum(i + 1, nt - 1), 0, 0),
                         memory_space=pltpu.SMEM),
            pl.BlockSpec((tm, LANES), lambda i: (i, 0)),
            pl.BlockSpec((tm, D), lambda i: (i, 0)),
            pl.BlockSpec((1, D), lambda i: (0, 0)),
            pl.BlockSpec(memory_space=pl.ANY),
        ],
        out_specs=pl.BlockSpec((tm, D), lambda i: (i, 0)),
        out_shape=jax.ShapeDtypeStruct((T, D), F32),
        scratch_shapes=[pltpu.VMEM((2, TOP_K, tm, D), F32), pltpu.SemaphoreType.DMA((2,))],
        compiler_params=_cparams(("arbitrary",)),
        name="combine",
    )(d3, d3, gates, x1, norm_f_w.reshape(1, D).astype(F32), y_buf)


def _routing(top_idx, n_experts, bm):
    T = top_idx.shape[0]
    A = T * TOP_K
    flat_e = top_idx.reshape(A)
    onehot = (flat_e[:, None] == jnp.arange(n_experts, dtype=jnp.int32)[None, :]).astype(jnp.int32)
    csum = jnp.cumsum(onehot, axis=0)
    rank = jnp.sum((csum - onehot) * onehot, axis=1)
    counts = csum[-1]
    padded = (counts + bm - 1) // bm * bm
    pend = jnp.cumsum(padded)
    pstart = pend - padded
    dest = jnp.sum(onehot * pstart[None, :], axis=1) + rank
    R = (A + bm - 1) // bm * bm + n_experts * bm
    nb = R // bm
    row_src = jnp.zeros((R,), jnp.int32).at[dest].set(jnp.arange(A, dtype=jnp.int32) // TOP_K)
    block_start = jnp.arange(nb, dtype=jnp.int32) * bm
    block_expert = jnp.minimum(jnp.searchsorted(pend, block_start, side='right'),
                               n_experts - 1).astype(jnp.int32)
    block_valid = (block_start < pend[-1]).astype(jnp.int32)
    return dest.astype(jnp.int32), row_src, block_expert, block_valid


def _pick(pref, n):
    t = min(pref, n)
    while n % t:
        t //= 2
    return t


def kernel(x, norm1_w, w_in, b_gate, conv_w, A_log, dt_bias, onorm_w, w_branch_a, w_pool,
           pool_scale, w_branch_p, w_out, norm2_w, router_w, router_b, w1, b1, w2, b2, norm_f_w):
    B, S, D = x.shape
    T = B * S
    depth = norm1_w.shape[0]
    x2 = x.reshape(T, D)
    for l in range(depth):
        conv_ch = conv_w.shape[-1]
        v_width = w_branch_a.shape[1]
        qk_width = (conv_ch - v_width) // 2
        n_v = A_log.shape[-1]
        qkvz = conv_ch + v_width
        ab_lo, ab_hi = qkvz, qkvz + 2 * n_v
        E = router_w.shape[-1]

        wl = w_in[l]
        w_main = jnp.concatenate([wl[:, :qkvz], wl[:, ab_hi:]], axis=1).astype(BF16)
        w_ab = jnp.zeros((D, LANES), BF16).at[:, :2 * n_v].set(wl[:, ab_lo:ab_hi].astype(BF16))
        p_off, ga_off, gp_off = qkvz, qkvz + D, qkvz + 2 * D

        proj, ab = _in_proj(x2, norm1_w[l], w_main, w_ab, _pick(1024, T), _pick(1024, w_main.shape[1]))
        proj3 = proj.reshape(B, S, -1)
        G = 2 if (qk_width // HEAD) % 2 == 0 else 1
        o_a = _delta(proj3, ab.reshape(B, S, LANES), conv_w[l], A_log[l], dt_bias[l], onorm_w[l],
                     qk_width, v_width, G)
        o_p = _pool(proj3, w_pool[l], pool_scale[l], p_off, D)
        h = _branch(o_a.reshape(T, v_width), o_p.reshape(T, D), proj,
                    w_branch_a[l].astype(BF16), w_branch_p[l].astype(BF16), b_gate[l],
                    ga_off, gp_off, _pick(1024, T), _pick(512, D))
        x1, xp, idx_pad, gate_pad = _post(x2, h, w_out[l].astype(BF16), norm2_w[l],
                                          router_w[l], router_b[l], _pick(512, T))

        bm = 256
        dest, row_src, block_expert, block_valid = _routing(idx_pad[:, :TOP_K], E, bm)
        x_buf = _gather_rows(xp, row_src, _pick(512, row_src.shape[0]))
        w1l_ = w1[l]
        w1g = w1l_[:, :, 0::2].astype(BF16)
        w1lin = w1l_[:, :, 1::2].astype(BF16)
        b1g = b1[l][:, None, 0::2].astype(F32)
        b1lin = b1[l][:, None, 1::2].astype(F32)
        F = w1g.shape[-1]
        act = _gmm1(block_expert, block_valid, x_buf, w1g, w1lin, b1g, b1lin, bm, _pick(1024, F))
        y_buf = _gmm2(block_expert, block_valid, act, w2[l].astype(BF16),
                      b2[l][:, None, :].astype(F32), bm, _pick(1024, D))
        is_last = l == depth - 1
        assert is_last, "final norm is fused into the last layer's combine"
        x2 = _combine(dest, gate_pad, x1, norm_f_w, y_buf, _pick(128, T))
    return x2.reshape(B, S, D)
```

```python
import functools
import math

import jax
import jax.numpy as jnp
from jax import lax
from jax.experimental import pallas as pl
from jax.experimental.pallas import tpu as pltpu

F32 = jnp.float32
BF16 = jnp.bfloat16

HEAD = 128
CHUNK = 64
CONV_WIDTH = 4
POOL_WINDOWS = (2, 4, 8, 16)
TOP_K = 4
SWIGLU_ALPHA = 1.702
SWIGLU_LIMIT = 7.0
RMS_EPS = 1e-6
L2_EPS = 1e-6
LANES = 128
HIGHEST = lax.Precision.HIGHEST
VMEM_LIMIT = 56 * 1024 * 1024


def _cparams(sem):
    return pltpu.CompilerParams(dimension_semantics=sem, vmem_limit_bytes=VMEM_LIMIT)


def _sigmoid(x):
    return 1.0 / (1.0 + jnp.exp(-x))


def _bdot(a, b):
    return jnp.dot(a.astype(BF16), b.astype(BF16), preferred_element_type=F32)


def _inproj_kernel(x_ref, nw_ref, w_ref, wab_ref, o_ref, ab_ref, xn_ref, *, rows):
    @pl.when(pl.program_id(1) == 0)
    def _():
        def body(r, c):
            r0 = pl.multiple_of(r * rows, rows)
            x = x_ref[pl.ds(r0, rows), :]
            inv = lax.rsqrt(jnp.mean(x * x, axis=-1, keepdims=True) + RMS_EPS)
            xn_ref[pl.ds(r0, rows), :] = (x * inv * nw_ref[...]).astype(BF16)
            return c
        lax.fori_loop(0, x_ref.shape[0] // rows, body, 0)
        ab_ref[...] = jnp.dot(xn_ref[...], wab_ref[...], preferred_element_type=F32)

    o_ref[...] = jnp.dot(xn_ref[...], w_ref[...], preferred_element_type=F32).astype(o_ref.dtype)


def _in_proj(x2, norm_w, w_main, w_ab, tm, tn):
    T, D = x2.shape
    N = w_main.shape[1]
    rows = min(128, tm)
    return pl.pallas_call(
        functools.partial(_inproj_kernel, rows=rows),
        grid=(T // tm, N // tn),
        in_specs=[
            pl.BlockSpec((tm, D), lambda i, j: (i, 0)),
            pl.BlockSpec((1, D), lambda i, j: (0, 0)),
            pl.BlockSpec((D, tn), lambda i, j: (0, j)),
            pl.BlockSpec((D, LANES), lambda i, j: (0, 0)),
        ],
        out_specs=[
            pl.BlockSpec((tm, tn), lambda i, j: (i, j)),
            pl.BlockSpec((tm, LANES), lambda i, j: (i, 0)),
        ],
        out_shape=[jax.ShapeDtypeStruct((T, N), BF16), jax.ShapeDtypeStruct((T, LANES), F32)],
        scratch_shapes=[pltpu.VMEM((tm, D), BF16)],
        compiler_params=_cparams(("arbitrary", "arbitrary")),
        name="in_proj",
    )(x2, norm_w.reshape(1, D), w_main, w_ab)


def _inv_unit_lower(L, eye):
    P = eye - L
    M = L
    for _ in range(int(math.log2(CHUNK)) - 1):
        M = _bdot(M, M)
        P = P + _bdot(P, M)
    return P


def _delta_kernel(q_ref, k_ref, v_ref, z_ref, ab_ref, cwq_ref, cwk_ref, cwv_ref,
                  alog_ref, dtb_ref, onw_ref, o_ref,
                  s_ref, tq_ref, tk_ref, tv_ref, *, G, H):
    S = q_ref.shape[0]
    C = CHUNK
    j = pl.program_id(1)

    s_ref[...] = jnp.zeros_like(s_ref)
    tq_ref[...] = jnp.zeros_like(tq_ref)
    tk_ref[...] = jnp.zeros_like(tk_ref)
    tv_ref[...] = jnp.zeros_like(tv_ref)

    row = lax.broadcasted_iota(jnp.int32, (C, C), 0)
    col = lax.broadcasted_iota(jnp.int32, (C, C), 1)
    incl = row >= col
    strict = row > col
    eye = jnp.where(row == col, 1.0, 0.0).astype(F32)
    tril = jnp.where(incl, 1.0, 0.0).astype(F32)
    lane = lax.broadcasted_iota(jnp.int32, (C, LANES), 1)
    shift = lax.rem(LANES - 2 * G * j, LANES)

    def conv_silu(raw_ref, tail_ref, cw_ref, r0, c0, width):
        cur = raw_ref[pl.ds(r0, C), c0:c0 + width].astype(F32)
        ext = jnp.concatenate([tail_ref[:, c0:c0 + width], cur], axis=0)
        y = cur * cw_ref[CONV_WIDTH - 1:CONV_WIDTH, c0:c0 + width]
        for i in range(CONV_WIDTH - 1):
            sh = CONV_WIDTH - 1 - i
            y = y + pltpu.roll(ext, sh, 0)[8:8 + C] * cw_ref[i:i + 1, c0:c0 + width]
        tail_ref[:, c0:c0 + width] = cur[C - 8:C]
        return y * _sigmoid(y)

    def chunk(n, carry):
        r0 = pl.multiple_of(n * C, C)
        ab = ab_ref[pl.ds(r0, C), :]
        sp_in = ab + dtb_ref[...]
        softplus = jnp.maximum(sp_in, 0.0) + jnp.log(1.0 + jnp.exp(-jnp.abs(sp_in)))
        gval = -jnp.exp(alog_ref[...]) * softplus
        vals = jnp.where(lane < H, gval, _sigmoid(ab))
        vals = pltpu.roll(vals, shift, 1)
        gc = jnp.dot(tril, vals, preferred_element_type=F32, precision=HIGHEST)
        gct = gc.T

        for h in range(G):
            q = conv_silu(q_ref, tq_ref, cwq_ref, r0, h * HEAD, HEAD)
            k = conv_silu(k_ref, tk_ref, cwk_ref, r0, h * HEAD, HEAD)
            v = conv_silu(v_ref, tv_ref, cwv_ref, r0, 2 * h * HEAD, 2 * HEAD)
            q = q * lax.rsqrt(jnp.sum(q * q, axis=-1, keepdims=True) + L2_EPS) * (HEAD ** -0.5)
            k = k * lax.rsqrt(jnp.sum(k * k, axis=-1, keepdims=True) + L2_EPS)
            qk_cat = jnp.concatenate([q, k], axis=0).astype(BF16)
            kb = k.astype(BF16)
            qkk = lax.dot_general(qk_cat, kb, (((1,), (1,)), ((), ())),
                                  preferred_element_type=F32)
            qk = qkk[:C]
            kk = qkk[C:]
            s_cat = jnp.concatenate([s_ref[2 * h], s_ref[2 * h + 1]], axis=1)
            ks = jnp.dot(qk_cat, s_cat.astype(BF16), preferred_element_type=F32)
            vnd = []
            dlast = []
            for e in range(2):
                hv = 2 * h + e
                lo, hi = e * HEAD, (e + 1) * HEAD
                gcol = jnp.broadcast_to(gc[:, hv:hv + 1], (C, LANES))
                bcol = jnp.broadcast_to(vals[:, H + hv:H + hv + 1], (C, LANES))
                grow = jnp.broadcast_to(gct[hv:hv + 1, :], (C, C))
                dec = jnp.exp(jnp.where(incl, gcol[:, :C] - grow, -jnp.inf))
                lmat = jnp.where(strict, kk * dec * bcol[:, :C], 0.0)
                tmat = _inv_unit_lower(lmat, eye)
                aqk = qk * dec
                eg = jnp.exp(gcol)
                rhs = bcol * (v[:, lo:hi] - eg * ks[C:, lo:hi])
                vnew = _bdot(tmat, rhs)
                o = eg * ks[:C, lo:hi] + _bdot(aqk, vnew)
                glast = gcol[C - 1:C, :]
                vnd.append(jnp.exp(glast - gcol) * vnew)
                dlast.append(jnp.exp(glast))
                zz = z_ref[pl.ds(r0, C), hv * HEAD:(hv + 1) * HEAD].astype(F32)
                on = o * lax.rsqrt(jnp.mean(o * o, axis=-1, keepdims=True) + RMS_EPS)
                on = on * onw_ref[...] * (zz * _sigmoid(zz))
                o_ref[pl.ds(r0, C), hv * HEAD:(hv + 1) * HEAD] = on.astype(o_ref.dtype)
            vnd_cat = jnp.concatenate(vnd, axis=1).astype(BF16)
            upd = lax.dot_general(kb, vnd_cat, (((0,), (0,)), ((), ())),
                                  preferred_element_type=F32)
            for e in range(2):
                hv = 2 * h + e
                s_ref[hv] = s_ref[hv] * dlast[e] + upd[:, e * HEAD:(e + 1) * HEAD]
        return carry

    lax.fori_loop(0, S // C, chunk, 0)


def _delta(proj3, ab3, conv_w, a_log, dt_bias, onorm_w, qk_width, v_width, G):
    B, S, _ = proj3.shape
    H = a_log.shape[0]
    n_groups = qk_width // (HEAD * G)
    qw, vw = HEAD * G, 2 * HEAD * G
    k_blk0 = qk_width // qw
    v_blk0 = (2 * qk_width) // vw
    z_blk0 = (2 * qk_width + v_width) // vw
    assert (2 * qk_width) % vw == 0 and (2 * qk_width + v_width) % vw == 0
    assert 2 * H <= LANES and S % CHUNK == 0
    pad = lambda p: jnp.zeros((1, LANES), F32).at[0, :H].set(p.astype(F32))
    conv_w = conv_w.astype(F32)
    return pl.pallas_call(
        functools.partial(_delta_kernel, G=G, H=H),
        grid=(B, n_groups),
        in_specs=[
            pl.BlockSpec((None, S, qw), lambda b, j: (b, 0, j)),
            pl.BlockSpec((None, S, qw), lambda b, j: (b, 0, k_blk0 + j)),
            pl.BlockSpec((None, S, vw), lambda b, j: (b, 0, v_blk0 + j)),
            pl.BlockSpec((None, S, vw), lambda b, j: (b, 0, z_blk0 + j)),
            pl.BlockSpec((None, S, LANES), lambda b, j: (b, 0, 0)),
            pl.BlockSpec((CONV_WIDTH, qw), lambda b, j: (0, j)),
            pl.BlockSpec((CONV_WIDTH, qw), lambda b, j: (0, k_blk0 + j)),
            pl.BlockSpec((CONV_WIDTH, vw), lambda b, j: (0, v_blk0 + j)),
            pl.BlockSpec((1, LANES), lambda b, j: (0, 0)),
            pl.BlockSpec((1, LANES), lambda b, j: (0, 0)),
            pl.BlockSpec((1, HEAD), lambda b, j: (0, 0)),
        ],
        out_specs=pl.BlockSpec((None, S, vw), lambda b, j: (b, 0, j)),
        out_shape=jax.ShapeDtypeStruct((B, S, v_width), BF16),
        scratch_shapes=[
            pltpu.VMEM((2 * G, HEAD, HEAD), F32),
            pltpu.VMEM((8, qw), F32),
            pltpu.VMEM((8, qw), F32),
            pltpu.VMEM((8, vw), F32),
        ],
        compiler_params=_cparams(("arbitrary", "arbitrary")),
        name="delta",
    )(proj3, proj3, proj3, proj3, ab3, conv_w, conv_w, conv_w,
      pad(a_log), pad(dt_bias), onorm_w.reshape(1, HEAD).astype(F32))


POOL_HALO = 16


def _pool_kernel(p_ref, w_ref, sc_ref, o_ref, pad_ref, *, rt):
    S = p_ref.shape[0]
    g = pl.program_id(1)
    pad_ref[0:POOL_HALO, :] = jnp.zeros((POOL_HALO, pad_ref.shape[1]), F32)
    pad_ref[POOL_HALO:, :] = p_ref[...].astype(F32)
    win = jnp.left_shift(2, g).astype(F32)

    def body(r, c):
        r0 = pl.multiple_of(r * rt, rt)
        x = pad_ref[pl.ds(r0, rt + POOL_HALO), :]
        y1 = x + pltpu.roll(x, 1, 0)
        y2 = y1 + pltpu.roll(y1, 2, 0)
        y3 = y2 + pltpu.roll(y2, 4, 0)
        y4 = y3 + pltpu.roll(y3, 8, 0)
        ysel = jnp.where(g == 0, y1, jnp.where(g == 1, y2, jnp.where(g == 2, y3, y4)))
        pos = (r0 + 1 + lax.broadcasted_iota(jnp.int32, (rt, 1), 0)).astype(F32)
        cnt = jnp.minimum(pos, win)
        mixed = ysel[POOL_HALO:] / cnt - x[POOL_HALO:]
        out = _bdot(mixed, w_ref[...]) * sc_ref[...]
        o_ref[pl.ds(r0, rt), :] = out.astype(o_ref.dtype)
        return c

    lax.fori_loop(0, S // rt, body, 0)


def _pool(proj3, w_pool, pool_scale, p_off, d_model):
    B, S, _ = proj3.shape
    ng, cg, _ = w_pool.shape
    assert POOL_WINDOWS == tuple(2 << i for i in range(ng)) and max(POOL_WINDOWS) <= POOL_HALO
    assert p_off % cg == 0
    blk0 = p_off // cg
    rt = min(256, S)
    return pl.pallas_call(
        functools.partial(_pool_kernel, rt=rt),
        grid=(B, ng),
        in_specs=[
            pl.BlockSpec((None, S, cg), lambda b, g: (b, 0, blk0 + g)),
            pl.BlockSpec((None, cg, cg), lambda b, g: (g, 0, 0)),
            pl.BlockSpec((1, cg), lambda b, g: (0, g)),
        ],
        out_specs=pl.BlockSpec((None, S, cg), lambda b, g: (b, 0, g)),
        out_shape=jax.ShapeDtypeStruct((B, S, d_model), BF16),
        scratch_shapes=[pltpu.VMEM((S + POOL_HALO, cg), F32)],
        compiler_params=_cparams(("arbitrary", "arbitrary")),
        name="pool",
    )(proj3, w_pool.astype(BF16), pool_scale.reshape(1, -1).astype(F32))


def _branch_kernel(oa_ref, op_ref, wa_ref, wp_ref, ga_ref, gp_ref, bga_ref, bgp_ref, h_ref):
    a = jnp.dot(oa_ref[...], wa_ref[...], preferred_element_type=F32)
    p = jnp.dot(op_ref[...], wp_ref[...], preferred_element_type=F32)
    ga = _sigmoid(ga_ref[...].astype(F32) + bga_ref[...])
    gp = _sigmoid(gp_ref[...].astype(F32) + bgp_ref[...])
    h_ref[...] = (ga * a + gp * p).astype(h_ref.dtype)


def _branch(o_a, o_p, proj, w_a, w_p, b_gate, ga_off, gp_off, tm, tn):
    T, V = o_a.shape
    D = w_a.shape[1]
    assert ga_off % tn == 0 and gp_off % tn == 0
    ga0, gp0 = ga_off // tn, gp_off // tn
    nd = D // tn
    bg = b_gate.reshape(1, 2 * D).astype(F32)
    return pl.pallas_call(
        _branch_kernel,
        grid=(T // tm, nd),
        in_specs=[
            pl.BlockSpec((tm, V), lambda i, j: (i, 0)),
            pl.BlockSpec((tm, D), lambda i, j: (i, 0)),
            pl.BlockSpec((V, tn), lambda i, j: (0, j)),
            pl.BlockSpec((D, tn), lambda i, j: (0, j)),
            pl.BlockSpec((tm, tn), lambda i, j: (i, ga0 + j)),
            pl.BlockSpec((tm, tn), lambda i, j: (i, gp0 + j)),
            pl.BlockSpec((1, tn), lambda i, j: (0, j)),
            pl.BlockSpec((1, tn), lambda i, j: (0, nd + j)),
        ],
        out_specs=pl.BlockSpec((tm, tn), lambda i, j: (i, j)),
        out_shape=jax.ShapeDtypeStruct((T, D), BF16),
        compiler_params=_cparams(("arbitrary", "arbitrary")),
        name="branch",
    )(o_a, o_p, w_a, w_p, proj, proj, bg, bg)


def _pack_halves(xb):
    half = xb.shape[1] // 2
    lo = lax.bitcast_convert_type(xb[:, :half], jnp.uint32)
    hi = lax.bitcast_convert_type(xb[:, half:], jnp.uint32)
    return jnp.right_shift(lo, jnp.uint32(16)) | (hi & jnp.uint32(0xFFFF0000))


def _unpack_halves(u):
    lo = lax.bitcast_convert_type(jnp.left_shift(u, jnp.uint32(16)), F32)
    hi = lax.bitcast_convert_type(u & jnp.uint32(0xFFFF0000), F32)
    return lo, hi


def _post_kernel(x_ref, h_ref, wo_ref, nw_ref, rw_ref, rb_ref,
                 x1_ref, xp_ref, idx_ref, gate_ref, *, n_experts):
    x1 = x_ref[...] + jnp.dot(h_ref[...], wo_ref[...], preferred_element_type=F32)
    x1_ref[...] = x1
    xn = x1 * lax.rsqrt(jnp.mean(x1 * x1, axis=-1, keepdims=True) + RMS_EPS) * nw_ref[...]
    xp_ref[...] = _pack_halves(xn.astype(BF16).astype(F32))
    logits = jnp.dot(xn, rw_ref[...], preferred_element_type=F32, precision=HIGHEST) + rb_ref[...]
    lane = lax.broadcasted_iota(jnp.int32, logits.shape, 1)
    work = jnp.where(lane < n_experts, logits, -jnp.inf)
    vals, idxs = [], []
    for _ in range(TOP_K):
        m = jnp.max(work, axis=-1, keepdims=True)
        idx = jnp.min(jnp.where(work == m, lane, LANES), axis=-1, keepdims=True)
        vals.append(m)
        idxs.append(idx)
        work = jnp.where(lane == idx, -jnp.inf, work)
    ex = [jnp.exp(v - vals[0]) for v in vals]
    den = ex[0]
    for e in ex[1:]:
        den = den + e
    idx_out = jnp.zeros(logits.shape, jnp.int32)
    gate_out = jnp.zeros(logits.shape, F32)
    for k in range(TOP_K):
        idx_out = jnp.where(lane == k, idxs[k], idx_out)
        gate_out = jnp.where(lane == k, ex[k] / den, gate_out)
    idx_ref[...] = idx_out
    gate_ref[...] = gate_out


def _post(x2, h, w_out, norm2_w, router_w, router_b, tm):
    T, D = x2.shape
    E = router_w.shape[1]
    rw = jnp.zeros((D, LANES), F32).at[:, :E].set(router_w.astype(F32))
    rb = jnp.zeros((1, LANES), F32).at[0, :E].set(router_b.astype(F32))
    return pl.pallas_call(
        functools.partial(_post_kernel, n_experts=E),
        grid=(T // tm,),
        in_specs=[
            pl.BlockSpec((tm, D), lambda i: (i, 0)),
            pl.BlockSpec((tm, D), lambda i: (i, 0)),
            pl.BlockSpec((D, D), lambda i: (0, 0)),
            pl.BlockSpec((1, D), lambda i: (0, 0)),
            pl.BlockSpec((D, LANES), lambda i: (0, 0)),
            pl.BlockSpec((1, LANES), lambda i: (0, 0)),
        ],
        out_specs=[
            pl.BlockSpec((tm, D), lambda i: (i, 0)),
            pl.BlockSpec((tm, D // 2), lambda i: (i, 0)),
            pl.BlockSpec((tm, LANES), lambda i: (i, 0)),
            pl.BlockSpec((tm, LANES), lambda i: (i, 0)),
        ],
        out_shape=[
            jax.ShapeDtypeStruct((T, D), F32),
            jax.ShapeDtypeStruct((T, D // 2), jnp.uint32),
            jax.ShapeDtypeStruct((T, LANES), jnp.int32),
            jax.ShapeDtypeStruct((T, LANES), F32),
        ],
        compiler_params=_cparams(("arbitrary",)),
        name="post",
    )(x2, h, w_out, norm2_w.reshape(1, D).astype(F32), rw, rb)


def _gather_kernel(src_ref, x_hbm, o_ref, sem, *, tile):
    def copy(r):
        return pltpu.make_async_copy(x_hbm.at[pl.ds(src_ref[0, 0, r], 1)], o_ref.at[pl.ds(r, 1)], sem)

    def start(r, c):
        copy(r).start()
        return c

    def wait(r, c):
        copy(r).wait()
        return c

    lax.fori_loop(0, tile, start, 0)
    lax.fori_loop(0, tile, wait, 0)


def _gather_rows(xp, row_src, tile):
    R = row_src.shape[0]
    W = xp.shape[1]
    return pl.pallas_call(
        functools.partial(_gather_kernel, tile=tile),
        grid=(R // tile,),
        in_specs=[
            pl.BlockSpec((1, 1, tile), lambda i: (i, 0, 0), memory_space=pltpu.SMEM),
            pl.BlockSpec(memory_space=pl.ANY),
        ],
        out_specs=pl.BlockSpec((tile, W), lambda i: (i, 0)),
        out_shape=jax.ShapeDtypeStruct((R, W), xp.dtype),
        scratch_shapes=[pltpu.SemaphoreType.DMA(())],
        compiler_params=_cparams(("arbitrary",)),
        name="gather",
    )(row_src.reshape(R // tile, 1, tile), xp)


def _gmm1_kernel(be_ref, bv_ref, x_ref, w_ref, b_ref, o_ref):
    m = pl.program_id(1)

    @pl.when(bv_ref[m] == 1)
    def _():
        lo, hi = _unpack_halves(x_ref[...])
        half = lo.shape[1]
        lo = lo.astype(BF16)
        hi = hi.astype(BF16)
        even = lax.broadcasted_iota(jnp.int32, (lo.shape[0], LANES), 1) % 2 == 0
        hfull = (jnp.dot(lo, w_ref[:half, :], preferred_element_type=F32)
                 + jnp.dot(hi, w_ref[half:, :], preferred_element_type=F32) + b_ref[...])

        def act_block(c):
            hblk = hfull[:, c * LANES:(c + 1) * LANES]
            x_glu = jnp.minimum(hblk, SWIGLU_LIMIT)
            x_lin = jnp.clip(hblk, -SWIGLU_LIMIT, SWIGLU_LIMIT) + 1.0
            return x_glu * _sigmoid(SWIGLU_ALPHA * x_glu) * pltpu.roll(x_lin, LANES - 1, 1)

        for i in range(o_ref.shape[1] // LANES):
            packed = jnp.where(even, act_block(2 * i), pltpu.roll(act_block(2 * i + 1), 1, 1))
            o_ref[:, i * LANES:(i + 1) * LANES] = packed.astype(o_ref.dtype)

    @pl.when(bv_ref[m] == 0)
    def _():
        o_ref[...] = jnp.zeros_like(o_ref)


def _permute_w2_rows(w2):
    E, F, D = w2.shape
    return w2.reshape(E, F // LANES, 2, LANES // 2, D).transpose(0, 1, 3, 2, 4).reshape(E, F, D)


def _gmm1(block_expert, block_valid, x_buf, w1, b1, bm, tn):
    R, half = x_buf.shape
    E, D, F2 = w1.shape
    nb = R // bm
    grid_spec = pltpu.PrefetchScalarGridSpec(
        num_scalar_prefetch=2,
        grid=(F2 // tn, nb),
        in_specs=[
            pl.BlockSpec((bm, half), lambda n, m, be, bv: (m, 0)),
            pl.BlockSpec((None, D, tn), lambda n, m, be, bv: (be[m], 0, n)),
            pl.BlockSpec((None, 1, tn), lambda n, m, be, bv: (be[m], 0, n)),
        ],
        out_specs=pl.BlockSpec((bm, tn // 2), lambda n, m, be, bv: (m, n)),
    )
    return pl.pallas_call(
        _gmm1_kernel,
        grid_spec=grid_spec,
        out_shape=jax.ShapeDtypeStruct((R, F2 // 2), BF16),
        compiler_params=_cparams(("arbitrary", "arbitrary")),
        name="gmm1",
    )(block_expert, block_valid, x_buf, w1, b1)


def _gmm2_kernel(be_ref, bv_ref, a_ref, w_ref, b_ref, o_ref):
    m = pl.program_id(1)

    @pl.when(bv_ref[m] == 1)
    def _():
        o_ref[...] = jnp.dot(a_ref[...], w_ref[...], preferred_element_type=F32) + b_ref[...]

    @pl.when(bv_ref[m] == 0)
    def _():
        o_ref[...] = jnp.zeros_like(o_ref)


def _gmm2(block_expert, block_valid, act, w2, b2, bm, tn):
    R, F = act.shape
    E, _, D = w2.shape
    nb = R // bm
    grid_spec = pltpu.PrefetchScalarGridSpec(
        num_scalar_prefetch=2,
        grid=(D // tn, nb),
        in_specs=[
            pl.BlockSpec((bm, F), lambda n, m, be, bv: (m, 0)),
            pl.BlockSpec((None, F, tn), lambda n, m, be, bv: (be[m], 0, n)),
            pl.BlockSpec((None, 1, tn), lambda n, m, be, bv: (be[m], 0, n)),
        ],
        out_specs=pl.BlockSpec((bm, tn), lambda n, m, be, bv: (m, n)),
    )
    return pl.pallas_call(
        _gmm2_kernel,
        grid_spec=grid_spec,
        out_shape=jax.ShapeDtypeStruct((R, D), F32),
        compiler_params=_cparams(("arbitrary", "arbitrary")),
        name="gmm2",
    )(block_expert, block_valid, act, w2, b2)


def _combine_kernel(dcur_ref, dnext_ref, gate_ref, x1_ref, nw_ref, y_hbm, o_ref, ybuf, sem, *, tm):
    i = pl.program_id(0)
    n = pl.num_programs(0)
    slot = lax.rem(i, 2)

    def copy(d_ref, s, r, k):
        return pltpu.make_async_copy(y_hbm.at[pl.ds(d_ref[0, 0, r * TOP_K + k], 1)],
                                     ybuf.at[s, k, pl.ds(r, 1)], sem.at[s])

    def issue(d_ref, s):
        def body(r, c):
            for k in range(TOP_K):
                copy(d_ref, s, r, k).start()
            return c
        lax.fori_loop(0, tm, body, 0)

    @pl.when(i == 0)
    def _():
        issue(dcur_ref, 0)

    @pl.when(i + 1 < n)
    def _():
        issue(dnext_ref, 1 - slot)

    def wait_body(r, c):
        for k in range(TOP_K):
            copy(dcur_ref, slot, r, k).wait()
        return c
    lax.fori_loop(0, tm, wait_body, 0)

    lane = lax.broadcasted_iota(jnp.int32, (tm, LANES), 1)
    gates = gate_ref[...]
    acc = x1_ref[...]
    for k in range(TOP_K):
        gk = jnp.sum(jnp.where(lane == k, gates, 0.0), axis=-1, keepdims=True)
        acc = acc + gk * ybuf[slot, k]
    out = acc * lax.rsqrt(jnp.mean(acc * acc, axis=-1, keepdims=True) + RMS_EPS) * nw_ref[...]
    o_ref[...] = out


def _combine(dest, gates, x1, norm_f_w, y_buf, tm):
    T, D = x1.shape
    nt = T // tm
    d3 = dest.reshape(nt, 1, tm * TOP_K)
    return pl.pallas_call(
        functools.partial(_combine_kernel, tm=tm),
        grid=(nt,),
        in_specs=[
            pl.BlockSpec((1, 1, tm * TOP_K), lambda i: (i, 0, 0), memory_space=pltpu.SMEM),
            pl.BlockSpec((1, 1, tm * TOP_K), lambda i: (jnp.minimum(i + 1, nt - 1), 0, 0),
                         memory_space=pltpu.SMEM),
            pl.BlockSpec((tm, LANES), lambda i: (i, 0)),
            pl.BlockSpec((tm, D), lambda i: (i, 0)),
            pl.BlockSpec((1, D), lambda i: (0, 0)),
            pl.BlockSpec(memory_space=pl.ANY),
        ],
        out_specs=pl.BlockSpec((tm, D), lambda i: (i, 0)),
        out_shape=jax.ShapeDtypeStruct((T, D), F32),
        scratch_shapes=[pltpu.VMEM((2, TOP_K, tm, D), F32), pltpu.SemaphoreType.DMA((2,))],
        compiler_params=_cparams(("arbitrary",)),
        name="combine",
    )(d3, d3, gates, x1, norm_f_w.reshape(1, D).astype(F32), y_buf)


def _routing(top_idx, n_experts, bm):
    T = top_idx.shape[0]
    A = T * TOP_K
    flat_e = top_idx.reshape(A)
    onehot = (flat_e[:, None] == jnp.arange(n_experts, dtype=jnp.int32)[None, :]).astype(jnp.int32)
    csum = jnp.cumsum(onehot, axis=0)
    rank = jnp.sum((csum - onehot) * onehot, axis=1)
    counts = csum[-1]
    padded = (counts + bm - 1) // bm * bm
    pend = jnp.cumsum(padded)
    pstart = pend - padded
    dest = jnp.sum(onehot * pstart[None, :], axis=1) + rank
    R = (A + bm - 1) // bm * bm + n_experts * bm
    nb = R // bm
    row_src = jnp.zeros((R,), jnp.int32).at[dest].set(jnp.arange(A, dtype=jnp.int32) // TOP_K)
    block_start = jnp.arange(nb, dtype=jnp.int32) * bm
    block_expert = jnp.minimum(jnp.searchsorted(pend, block_start, side='right'),
                               n_experts - 1).astype(jnp.int32)
    block_valid = (block_start < pend[-1]).astype(jnp.int32)
    return dest.astype(jnp.int32), row_src, block_expert, block_valid


def _pick(pref, n):
    t = min(pref, n)
    while n % t:
        t //= 2
    return t


def kernel(x, norm1_w, w_in, b_gate, conv_w, A_log, dt_bias, onorm_w, w_branch_a, w_pool,
           pool_scale, w_branch_p, w_out, norm2_w, router_w, router_b, w1, b1, w2, b2, norm_f_w):
    B, S, D = x.shape
    T = B * S
    depth = norm1_w.shape[0]
    x2 = x.reshape(T, D)
    for l in range(depth):
        conv_ch = conv_w.shape[-1]
        v_width = w_branch_a.shape[1]
        qk_width = (conv_ch - v_width) // 2
        n_v = A_log.shape[-1]
        qkvz = conv_ch + v_width
        ab_lo, ab_hi = qkvz, qkvz + 2 * n_v
        E = router_w.shape[-1]

        wl = w_in[l]
        w_main = jnp.concatenate([wl[:, :qkvz], wl[:, ab_hi:]], axis=1).astype(BF16)
        w_ab = jnp.zeros((D, LANES), BF16).at[:, :2 * n_v].set(wl[:, ab_lo:ab_hi].astype(BF16))
        p_off, ga_off, gp_off = qkvz, qkvz + D, qkvz + 2 * D

        proj, ab = _in_proj(x2, norm1_w[l], w_main, w_ab, _pick(1024, T), _pick(1024, w_main.shape[1]))
        proj3 = proj.reshape(B, S, -1)
        G = 2 if (qk_width // HEAD) % 2 == 0 else 1
        o_a = _delta(proj3, ab.reshape(B, S, LANES), conv_w[l], A_log[l], dt_bias[l], onorm_w[l],
                     qk_width, v_width, G)
        o_p = _pool(proj3, w_pool[l], pool_scale[l], p_off, D)
        h = _branch(o_a.reshape(T, v_width), o_p.reshape(T, D), proj,
                    w_branch_a[l].astype(BF16), w_branch_p[l].astype(BF16), b_gate[l],
                    ga_off, gp_off, _pick(1024, T), _pick(512, D))
        x1, xp, idx_pad, gate_pad = _post(x2, h, w_out[l].astype(BF16), norm2_w[l],
                                          router_w[l], router_b[l], _pick(512, T))

        bm = 256
        dest, row_src, block_expert, block_valid = _routing(idx_pad[:, :TOP_K], E, bm)
        x_buf = _gather_rows(xp, row_src, _pick(512, row_src.shape[0]))
        act = _gmm1(block_expert, block_valid, x_buf, w1[l].astype(BF16),
                    b1[l][:, None, :].astype(F32), bm, _pick(1024, w1.shape[-1]))
        y_buf = _gmm2(block_expert, block_valid, act, _permute_w2_rows(w2[l]).astype(BF16),
                      b2[l][:, None, :].astype(F32), bm, _pick(1024, D))
        is_last = l == depth - 1
        assert is_last, "final norm is fused into the last layer's combine"
        x2 = _combine(dest, gate_pad, x1, norm_f_w, y_buf, _pick(128, T))
    return x2.reshape(B, S, D)
```

```python
import functools
import math

import jax
import jax.numpy as jnp
from jax import lax
from jax.experimental import pallas as pl
from jax.experimental.pallas import tpu as pltpu

F32 = jnp.float32
BF16 = jnp.bfloat16

HEAD = 128
CHUNK = 64
CONV_WIDTH = 4
POOL_WINDOWS = (2, 4, 8, 16)
TOP_K = 4
SWIGLU_ALPHA = 1.702
SWIGLU_LIMIT = 7.0
RMS_EPS = 1e-6
L2_EPS = 1e-6
LANES = 128
HIGHEST = lax.Precision.HIGHEST
VMEM_LIMIT = 56 * 1024 * 1024


def _cparams(sem):
    return pltpu.CompilerParams(dimension_semantics=sem, vmem_limit_bytes=VMEM_LIMIT)


def _sigmoid(x):
    return 1.0 / (1.0 + jnp.exp(-x))


def _bdot(a, b):
    return jnp.dot(a.astype(BF16), b.astype(BF16), preferred_element_type=F32)


def _inproj_kernel(x_ref, nw_ref, w_ref, wab_ref, o_ref, ab_ref, xn_ref, *, rows):
    @pl.when(pl.program_id(1) == 0)
    def _():
        def body(r, c):
            r0 = pl.multiple_of(r * rows, rows)
            x = x_ref[pl.ds(r0, rows), :]
            inv = lax.rsqrt(jnp.mean(x * x, axis=-1, keepdims=True) + RMS_EPS)
            xn_ref[pl.ds(r0, rows), :] = (x * inv * nw_ref[...]).astype(BF16)
            return c
        lax.fori_loop(0, x_ref.shape[0] // rows, body, 0)
        ab_ref[...] = jnp.dot(xn_ref[...], wab_ref[...], preferred_element_type=F32)

    o_ref[...] = jnp.dot(xn_ref[...], w_ref[...], preferred_element_type=F32).astype(o_ref.dtype)


def _in_proj(x2, norm_w, w_main, w_ab, tm, tn):
    T, D = x2.shape
    N = w_main.shape[1]
    rows = min(128, tm)
    return pl.pallas_call(
        functools.partial(_inproj_kernel, rows=rows),
        grid=(T // tm, N // tn),
        in_specs=[
            pl.BlockSpec((tm, D), lambda i, j: (i, 0)),
            pl.BlockSpec((1, D), lambda i, j: (0, 0)),
            pl.BlockSpec((D, tn), lambda i, j: (0, j)),
            pl.BlockSpec((D, LANES), lambda i, j: (0, 0)),
        ],
        out_specs=[
            pl.BlockSpec((tm, tn), lambda i, j: (i, j)),
            pl.BlockSpec((tm, LANES), lambda i, j: (i, 0)),
        ],
        out_shape=[jax.ShapeDtypeStruct((T, N), BF16), jax.ShapeDtypeStruct((T, LANES), F32)],
        scratch_shapes=[pltpu.VMEM((tm, D), BF16)],
        compiler_params=_cparams(("arbitrary", "arbitrary")),
        name="in_proj",
    )(x2, norm_w.reshape(1, D), w_main, w_ab)


def _delta_kernel(q_ref, k_ref, v_ref, z_ref, ab_ref, qp_ref, kp_ref, vp_ref, cwq_ref, cwk_ref, cwv_ref,
                  alog_ref, dtb_ref, onw_ref, o_ref,
                  s_ref, qs_ref, ks_ref, vs_ref, t_ref, a_ref, gc_ref, vl_ref, *, G, H):
    St = q_ref.shape[0]
    C = CHUNK
    WIDE = 4 * C
    NW = G // 2
    t_idx = pl.program_id(2)
    j = pl.program_id(1)
    PRE = qp_ref.shape[0]

    @pl.when(t_idx == 0)
    def _():
        s_ref[...] = jnp.zeros_like(s_ref)

    row = lax.broadcasted_iota(jnp.int32, (C, WIDE), 0)
    lanew = lax.broadcasted_iota(jnp.int32, (C, WIDE), 1)
    colw = lanew % C
    blk = lanew // C
    incl_w = row >= colw
    strict_w = row > colw
    eye_w = jnp.where(row == colw, 1.0, 0.0).astype(F32)
    first_half = (lax.broadcasted_iota(jnp.int32, (C, 2 * C), 1) < C)
    r2 = lax.broadcasted_iota(jnp.int32, (C, C), 0)
    c2 = lax.broadcasted_iota(jnp.int32, (C, C), 1)
    tril = jnp.where(r2 >= c2, 1.0, 0.0).astype(F32)
    lane = lax.broadcasted_iota(jnp.int32, (C, LANES), 1)
    shift = lax.rem(LANES - 2 * G * j, LANES)

    def conv_silu(raw_ref, halo_ref, cw_ref, c, r0, c0, width):
        cur = raw_ref[pl.ds(r0, C), c0:c0 + width].astype(F32)
        rp = pl.multiple_of(jnp.maximum(r0 - PRE, 0), PRE)
        inside = raw_ref[pl.ds(rp, PRE), c0:c0 + width].astype(F32)
        halo = jnp.where(t_idx > 0, halo_ref[:, c0:c0 + width].astype(F32), 0.0)
        prev = jnp.where(c > 0, inside, halo)
        ext = jnp.concatenate([prev, cur], axis=0)
        y = cur * cw_ref[CONV_WIDTH - 1:CONV_WIDTH, c0:c0 + width]
        for i in range(CONV_WIDTH - 1):
            sh = CONV_WIDTH - 1 - i
            y = y + pltpu.roll(ext, sh, 0)[PRE:PRE + C] * cw_ref[i:i + 1, c0:c0 + width]
        return y * _sigmoid(y)

    def pair_cols(x, c0):
        a = jnp.broadcast_to(x[:, c0:c0 + 1], (C, 2 * C))
        b = jnp.broadcast_to(x[:, c0 + 1:c0 + 2], (C, 2 * C))
        return jnp.where(first_half, a, b)

    def block_diag(mw):
        return jnp.concatenate([jnp.where(blk == e, mw, jnp.zeros_like(mw)) for e in range(4)], axis=0)

    def prep_chunk(c, carry):
        r0 = pl.multiple_of(c * C, C)
        ab = ab_ref[pl.ds(r0, C), :]
        sp_in = ab + dtb_ref[...]
        softplus = jnp.maximum(sp_in, 0.0) + jnp.log(1.0 + jnp.exp(-jnp.abs(sp_in)))
        gval = -jnp.exp(alog_ref[...]) * softplus
        vals = jnp.where(lane < H, gval, _sigmoid(ab))
        vals = pltpu.roll(vals, shift, 1)
        gc = jnp.dot(tril, vals, preferred_element_type=F32, precision=HIGHEST)
        gc_ref[pl.ds(r0, C), :] = gc
        vl_ref[pl.ds(r0, C), :] = vals
        qbs, kbs = [], []
        for h in range(G):
            q = conv_silu(q_ref, qp_ref, cwq_ref, c, r0, h * HEAD, HEAD)
            k = conv_silu(k_ref, kp_ref, cwk_ref, c, r0, h * HEAD, HEAD)
            v = conv_silu(v_ref, vp_ref, cwv_ref, c, r0, 2 * h * HEAD, 2 * HEAD)
            q = q * lax.rsqrt(jnp.sum(q * q, axis=-1, keepdims=True) + L2_EPS) * (HEAD ** -0.5)
            k = k * lax.rsqrt(jnp.sum(k * k, axis=-1, keepdims=True) + L2_EPS)
            qb = q.astype(BF16)
            kb = k.astype(BF16)
            qs_ref[pl.ds(r0, C), h * HEAD:(h + 1) * HEAD] = qb
            ks_ref[pl.ds(r0, C), h * HEAD:(h + 1) * HEAD] = kb
            vs_ref[pl.ds(r0, C), 2 * h * HEAD:2 * (h + 1) * HEAD] = v
            qbs.append(qb)
            kbs.append(kb)
        qkk = [lax.dot_general(jnp.concatenate([qbs[h], kbs[h]], axis=0),
                               jnp.concatenate([kbs[h], kbs[h]], axis=0),
                               (((1,), (1,)), ((), ())), preferred_element_type=F32)
               for h in range(G)]
        ps, ms = [], []
        for w in range(NW):
            qk_w = jnp.concatenate([qkk[2 * w][:C], qkk[2 * w + 1][:C]], axis=1)
            kk_w = jnp.concatenate([qkk[2 * w][C:], qkk[2 * w + 1][C:]], axis=1)
            gcol = jnp.concatenate([pair_cols(gc, 4 * w), pair_cols(gc, 4 * w + 2)], axis=1)
            bcol = jnp.concatenate([pair_cols(vals, H + 4 * w), pair_cols(vals, H + 4 * w + 2)], axis=1)
            grow = jnp.sum(gcol * eye_w, axis=0, keepdims=True)
            dec = jnp.exp(jnp.where(incl_w, gcol - grow, -jnp.inf))
            lmat = jnp.where(strict_w, kk_w * dec * bcol, 0.0)
            a_ref[pl.ds(r0, C), w * WIDE:(w + 1) * WIDE] = (qk_w * dec).astype(BF16)
            ps.append(eye_w - lmat)
            ms.append(lmat)
        ms = [_bdot(m, block_diag(m.astype(BF16))) for m in ms]
        for _ in range(int(math.log2(C)) - 2):
            outs = [jnp.dot(jnp.concatenate([p, m], axis=0).astype(BF16), block_diag(m.astype(BF16)),
                            preferred_element_type=F32) for p, m in zip(ps, ms)]
            ps = [p + o[:C] for p, o in zip(ps, outs)]
            ms = [o[C:] for o in outs]
        outs = [_bdot(p, block_diag(m.astype(BF16))) for p, m in zip(ps, ms)]
        for w in range(NW):
            t_ref[pl.ds(r0, C), w * WIDE:(w + 1) * WIDE] = (ps[w] + outs[w]).astype(BF16)
        return carry

    lax.fori_loop(0, St // C, prep_chunk, 0)

    def scan_chunk(c, carry):
        r0 = pl.multiple_of(c * C, C)
        gc = gc_ref[pl.ds(r0, C), :]
        vals = vl_ref[pl.ds(r0, C), :]
        kbs = [ks_ref[pl.ds(r0, C), h * HEAD:(h + 1) * HEAD] for h in range(G)]
        ks = [jnp.dot(jnp.concatenate([qs_ref[pl.ds(r0, C), h * HEAD:(h + 1) * HEAD], kbs[h]], axis=0),
                      jnp.concatenate([s_ref[2 * h], s_ref[2 * h + 1]], axis=1).astype(BF16),
                      preferred_element_type=F32) for h in range(G)]
        egs, gcols, rhss = [], [], []
        for hv in range(2 * G):
            h, e = divmod(hv, 2)
            gcol = jnp.broadcast_to(gc[:, hv:hv + 1], (C, LANES))
            bcol = jnp.broadcast_to(vals[:, H + hv:H + hv + 1], (C, LANES))
            eg = jnp.exp(gcol)
            v = vs_ref[pl.ds(r0, C), hv * HEAD:(hv + 1) * HEAD]
            rhss.append((bcol * (v - eg * ks[h][C:, e * HEAD:(e + 1) * HEAD])).astype(BF16))
            egs.append(eg)
            gcols.append(gcol)
        vnews = [jnp.dot(t_ref[pl.ds(r0, C), hv * C:(hv + 1) * C], rhss[hv],
                         preferred_element_type=F32) for hv in range(2 * G)]
        intra = [jnp.dot(a_ref[pl.ds(r0, C), hv * C:(hv + 1) * C], vnews[hv].astype(BF16),
                         preferred_element_type=F32) for hv in range(2 * G)]
        vnd = [(jnp.exp(gcols[hv][C - 1:C, :] - gcols[hv]) * vnews[hv]).astype(BF16)
               for hv in range(2 * G)]
        upd = [lax.dot_general(kbs[h], jnp.concatenate([vnd[2 * h], vnd[2 * h + 1]], axis=1),
                               (((0,), (0,)), ((), ())), preferred_element_type=F32)
               for h in range(G)]
        for hv in range(2 * G):
            h, e = divmod(hv, 2)
            s_ref[hv] = (s_ref[hv] * jnp.exp(gcols[hv][C - 1:C, :])
                         + upd[h][:, e * HEAD:(e + 1) * HEAD])
            o = egs[hv] * ks[h][:C, e * HEAD:(e + 1) * HEAD] + intra[hv]
            zz = z_ref[pl.ds(r0, C), hv * HEAD:(hv + 1) * HEAD].astype(F32)
            on = o * lax.rsqrt(jnp.mean(o * o, axis=-1, keepdims=True) + RMS_EPS)
            on = on * onw_ref[...] * (zz * _sigmoid(zz))
            o_ref[pl.ds(r0, C), hv * HEAD:(hv + 1) * HEAD] = on.astype(o_ref.dtype)
        return carry

    lax.fori_loop(0, St // C, scan_chunk, 0)


DELTA_HALO = 16


def _delta(proj3, ab3, conv_w, a_log, dt_bias, onorm_w, qk_width, v_width, G, St):
    B, S, _ = proj3.shape
    H = a_log.shape[0]
    n_groups = qk_width // (HEAD * G)
    qw, vw = HEAD * G, 2 * HEAD * G
    k_blk0 = qk_width // qw
    v_blk0 = (2 * qk_width) // vw
    z_blk0 = (2 * qk_width + v_width) // vw
    assert (2 * qk_width) % vw == 0 and (2 * qk_width + v_width) % vw == 0
    assert 2 * H <= LANES and S % CHUNK == 0
    assert G % 2 == 0 and S % St == 0 and St % CHUNK == 0 and CONV_WIDTH - 1 <= DELTA_HALO
    hb = St // DELTA_HALO
    halo = lambda t: jnp.maximum(t * hb - 1, 0)
    pad = lambda p: jnp.zeros((1, LANES), F32).at[0, :H].set(p.astype(F32))
    conv_w = conv_w.astype(F32)
    return pl.pallas_call(
        functools.partial(_delta_kernel, G=G, H=H),
        grid=(B, n_groups, S // St),
        in_specs=[
            pl.BlockSpec((None, St, qw), lambda b, j, t: (b, t, j)),
            pl.BlockSpec((None, St, qw), lambda b, j, t: (b, t, k_blk0 + j)),
            pl.BlockSpec((None, St, vw), lambda b, j, t: (b, t, v_blk0 + j)),
            pl.BlockSpec((None, St, vw), lambda b, j, t: (b, t, z_blk0 + j)),
            pl.BlockSpec((None, St, LANES), lambda b, j, t: (b, t, 0)),
            pl.BlockSpec((None, DELTA_HALO, qw), lambda b, j, t: (b, halo(t), j)),
            pl.BlockSpec((None, DELTA_HALO, qw), lambda b, j, t: (b, halo(t), k_blk0 + j)),
            pl.BlockSpec((None, DELTA_HALO, vw), lambda b, j, t: (b, halo(t), v_blk0 + j)),
            pl.BlockSpec((CONV_WIDTH, qw), lambda b, j, t: (0, j)),
            pl.BlockSpec((CONV_WIDTH, qw), lambda b, j, t: (0, k_blk0 + j)),
            pl.BlockSpec((CONV_WIDTH, vw), lambda b, j, t: (0, v_blk0 + j)),
            pl.BlockSpec((1, LANES), lambda b, j, t: (0, 0)),
            pl.BlockSpec((1, LANES), lambda b, j, t: (0, 0)),
            pl.BlockSpec((1, HEAD), lambda b, j, t: (0, 0)),
        ],
        out_specs=pl.BlockSpec((None, St, vw), lambda b, j, t: (b, t, j)),
        out_shape=jax.ShapeDtypeStruct((B, S, v_width), BF16),
        scratch_shapes=[
            pltpu.VMEM((2 * G, HEAD, HEAD), F32),
            pltpu.VMEM((St, qw), BF16),
            pltpu.VMEM((St, qw), BF16),
            pltpu.VMEM((St, vw), F32),
            pltpu.VMEM((St, qw), BF16),
            pltpu.VMEM((St, qw), BF16),
            pltpu.VMEM((St, LANES), F32),
            pltpu.VMEM((St, LANES), F32),
        ],
        compiler_params=_cparams(("arbitrary", "arbitrary", "arbitrary")),
        name="delta",
    )(proj3, proj3, proj3, proj3, ab3, proj3, proj3, proj3, conv_w, conv_w, conv_w,
      pad(a_log), pad(dt_bias), onorm_w.reshape(1, HEAD).astype(F32))


POOL_HALO = 16


def _pool_kernel(p_ref, w_ref, sc_ref, o_ref, pad_ref, *, rt):
    S = p_ref.shape[0]
    g = pl.program_id(1)
    pad_ref[0:POOL_HALO, :] = jnp.zeros((POOL_HALO, pad_ref.shape[1]), F32)
    pad_ref[POOL_HALO:, :] = p_ref[...].astype(F32)
    win = jnp.left_shift(2, g).astype(F32)

    def body(r, c):
        r0 = pl.multiple_of(r * rt, rt)
        x = pad_ref[pl.ds(r0, rt + POOL_HALO), :]
        y1 = x + pltpu.roll(x, 1, 0)
        y2 = y1 + pltpu.roll(y1, 2, 0)
        y3 = y2 + pltpu.roll(y2, 4, 0)
        y4 = y3 + pltpu.roll(y3, 8, 0)
        ysel = jnp.where(g == 0, y1, jnp.where(g == 1, y2, jnp.where(g == 2, y3, y4)))
        pos = (r0 + 1 + lax.broadcasted_iota(jnp.int32, (rt, 1), 0)).astype(F32)
        cnt = jnp.minimum(pos, win)
        mixed = ysel[POOL_HALO:] / cnt - x[POOL_HALO:]
        out = _bdot(mixed, w_ref[...]) * sc_ref[...]
        o_ref[pl.ds(r0, rt), :] = out.astype(o_ref.dtype)
        return c

    lax.fori_loop(0, S // rt, body, 0)


def _pool(proj3, w_pool, pool_scale, p_off, d_model):
    B, S, _ = proj3.shape
    ng, cg, _ = w_pool.shape
    assert POOL_WINDOWS == tuple(2 << i for i in range(ng)) and max(POOL_WINDOWS) <= POOL_HALO
    assert p_off % cg == 0
    blk0 = p_off // cg
    rt = min(256, S)
    return pl.pallas_call(
        functools.partial(_pool_kernel, rt=rt),
        grid=(B, ng),
        in_specs=[
            pl.BlockSpec((None, S, cg), lambda b, g: (b, 0, blk0 + g)),
            pl.BlockSpec((None, cg, cg), lambda b, g: (g, 0, 0)),
            pl.BlockSpec((1, cg), lambda b, g: (0, g)),
        ],
        out_specs=pl.BlockSpec((None, S, cg), lambda b, g: (b, 0, g)),
        out_shape=jax.ShapeDtypeStruct((B, S, d_model), BF16),
        scratch_shapes=[pltpu.VMEM((S + POOL_HALO, cg), F32)],
        compiler_params=_cparams(("arbitrary", "arbitrary")),
        name="pool",
    )(proj3, w_pool.astype(BF16), pool_scale.reshape(1, -1).astype(F32))


def _branch_kernel(oa_ref, op_ref, wa_ref, wp_ref, ga_ref, gp_ref, bga_ref, bgp_ref, h_ref):
    a = jnp.dot(oa_ref[...], wa_ref[...], preferred_element_type=F32)
    p = jnp.dot(op_ref[...], wp_ref[...], preferred_element_type=F32)
    ga = _sigmoid(ga_ref[...].astype(F32) + bga_ref[...])
    gp = _sigmoid(gp_ref[...].astype(F32) + bgp_ref[...])
    h_ref[...] = (ga * a + gp * p).astype(h_ref.dtype)


def _branch(o_a, o_p, proj, w_a, w_p, b_gate, ga_off, gp_off, tm, tn):
    T, V = o_a.shape
    D = w_a.shape[1]
    assert ga_off % tn == 0 and gp_off % tn == 0
    ga0, gp0 = ga_off // tn, gp_off // tn
    nd = D // tn
    bg = b_gate.reshape(1, 2 * D).astype(F32)
    return pl.pallas_call(
        _branch_kernel,
        grid=(T // tm, nd),
        in_specs=[
            pl.BlockSpec((tm, V), lambda i, j: (i, 0)),
            pl.BlockSpec((tm, D), lambda i, j: (i, 0)),
            pl.BlockSpec((V, tn), lambda i, j: (0, j)),
            pl.BlockSpec((D, tn), lambda i, j: (0, j)),
            pl.BlockSpec((tm, tn), lambda i, j: (i, ga0 + j)),
            pl.BlockSpec((tm, tn), lambda i, j: (i, gp0 + j)),
            pl.BlockSpec((1, tn), lambda i, j: (0, j)),
            pl.BlockSpec((1, tn), lambda i, j: (0, nd + j)),
        ],
        out_specs=pl.BlockSpec((tm, tn), lambda i, j: (i, j)),
        out_shape=jax.ShapeDtypeStruct((T, D), BF16),
        compiler_params=_cparams(("arbitrary", "arbitrary")),
        name="branch",
    )(o_a, o_p, w_a, w_p, proj, proj, bg, bg)


def _pack_pair(lo, hi):
    lo = lax.bitcast_convert_type(lo.astype(BF16).astype(F32), jnp.uint32)
    hi = lax.bitcast_convert_type(hi.astype(BF16).astype(F32), jnp.uint32)
    return jnp.right_shift(lo, jnp.uint32(16)) | (hi & jnp.uint32(0xFFFF0000))


def _pack_halves(x):
    half = x.shape[1] // 2
    return _pack_pair(x[:, :half], x[:, half:])


def _unpack_halves(u):
    lo = lax.bitcast_convert_type(jnp.left_shift(u, jnp.uint32(16)), F32)
    hi = lax.bitcast_convert_type(u & jnp.uint32(0xFFFF0000), F32)
    return lo, hi


def _post_kernel(x_ref, h_ref, wo_ref, nw_ref, rw_ref, rb_ref,
                 x1_ref, xp_ref, idx_ref, gate_ref, *, n_experts):
    x1 = x_ref[...] + jnp.dot(h_ref[...], wo_ref[...], preferred_element_type=F32)
    x1_ref[...] = x1
    xn = x1 * lax.rsqrt(jnp.mean(x1 * x1, axis=-1, keepdims=True) + RMS_EPS) * nw_ref[...]
    xp_ref[...] = _pack_halves(xn)
    logits = jnp.dot(xn, rw_ref[...], preferred_element_type=F32, precision=HIGHEST) + rb_ref[...]
    lane = lax.broadcasted_iota(jnp.int32, logits.shape, 1)
    work = jnp.where(lane < n_experts, logits, -jnp.inf)
    vals, idxs = [], []
    for _ in range(TOP_K):
        m = jnp.max(work, axis=-1, keepdims=True)
        idx = jnp.min(jnp.where(work == m, lane, LANES), axis=-1, keepdims=True)
        vals.append(m)
        idxs.append(idx)
        work = jnp.where(lane == idx, -jnp.inf, work)
    ex = [jnp.exp(v - vals[0]) for v in vals]
    den = ex[0]
    for e in ex[1:]:
        den = den + e
    idx_out = jnp.zeros(logits.shape, jnp.int32)
    gate_out = jnp.zeros(logits.shape, F32)
    for k in range(TOP_K):
        idx_out = jnp.where(lane == k, idxs[k], idx_out)
        gate_out = jnp.where(lane == k, ex[k] / den, gate_out)
    idx_ref[...] = idx_out
    gate_ref[...] = gate_out


def _post(x2, h, w_out, norm2_w, router_w, router_b, tm):
    T, D = x2.shape
    E = router_w.shape[1]
    rw = jnp.zeros((D, LANES), F32).at[:, :E].set(router_w.astype(F32))
    rb = jnp.zeros((1, LANES), F32).at[0, :E].set(router_b.astype(F32))
    return pl.pallas_call(
        functools.partial(_post_kernel, n_experts=E),
        grid=(T // tm,),
        in_specs=[
            pl.BlockSpec((tm, D), lambda i: (i, 0)),
            pl.BlockSpec((tm, D), lambda i: (i, 0)),
            pl.BlockSpec((D, D), lambda i: (0, 0)),
            pl.BlockSpec((1, D), lambda i: (0, 0)),
            pl.BlockSpec((D, LANES), lambda i: (0, 0)),
            pl.BlockSpec((1, LANES), lambda i: (0, 0)),
        ],
        out_specs=[
            pl.BlockSpec((tm, D), lambda i: (i, 0)),
            pl.BlockSpec((tm, D // 2), lambda i: (i, 0)),
            pl.BlockSpec((tm, LANES), lambda i: (i, 0)),
            pl.BlockSpec((tm, LANES), lambda i: (i, 0)),
        ],
        out_shape=[
            jax.ShapeDtypeStruct((T, D), F32),
            jax.ShapeDtypeStruct((T, D // 2), jnp.uint32),
            jax.ShapeDtypeStruct((T, LANES), jnp.int32),
            jax.ShapeDtypeStruct((T, LANES), F32),
        ],
        compiler_params=_cparams(("arbitrary",)),
        name="post",
    )(x2, h, w_out, norm2_w.reshape(1, D).astype(F32), rw, rb)


def _gather_kernel(cur_ref, nxt_ref, x_hbm, o_ref, buf, sem, *, tile):
    i = pl.program_id(0)
    n = pl.num_programs(0)
    slot = lax.rem(i, 2)

    def issue(src_ref, s):
        def body(r, c):
            pltpu.make_async_copy(x_hbm.at[pl.ds(src_ref[0, 0, r], 1)],
                                  buf.at[s, pl.ds(r, 1)], sem.at[s]).start()
            return c
        lax.fori_loop(0, tile, body, 0, unroll=8)

    @pl.when(i == 0)
    def _():
        issue(cur_ref, 0)

    @pl.when(i + 1 < n)
    def _():
        issue(nxt_ref, 1 - slot)

    pltpu.make_async_copy(x_hbm.at[pl.ds(0, tile)], buf.at[slot], sem.at[slot]).wait()
    lo, hi = _unpack_halves(buf[slot])
    half = lo.shape[1]
    o_ref[:, :half] = lo.astype(o_ref.dtype)
    o_ref[:, half:] = hi.astype(o_ref.dtype)


def _gather_rows(xp, row_src, tile):
    R = row_src.shape[0]
    W = xp.shape[1]
    nt = R // tile
    src3 = row_src.reshape(nt, 1, tile)
    return pl.pallas_call(
        functools.partial(_gather_kernel, tile=tile),
        grid=(nt,),
        in_specs=[
            pl.BlockSpec((1, 1, tile), lambda i: (i, 0, 0), memory_space=pltpu.SMEM),
            pl.BlockSpec((1, 1, tile), lambda i: (jnp.minimum(i + 1, nt - 1), 0, 0),
                         memory_space=pltpu.SMEM),
            pl.BlockSpec(memory_space=pl.ANY),
        ],
        out_specs=pl.BlockSpec((tile, 2 * W), lambda i: (i, 0)),
        out_shape=jax.ShapeDtypeStruct((R, 2 * W), BF16),
        scratch_shapes=[pltpu.VMEM((2, tile, W), xp.dtype), pltpu.SemaphoreType.DMA((2,))],
        compiler_params=_cparams(("arbitrary",)),
        name="gather",
    )(src3, src3, xp)


def _new_expert(be_ref, m):
    return jnp.logical_or(m == 0, be_ref[m] != be_ref[jnp.maximum(m - 1, 0)])


CAST_ROWS = 256


def _gmm1_kernel(be_ref, bv_ref, x_ref, w_ref, b_ref, o_ref, wb_ref):
    m = pl.program_id(1)
    valid = bv_ref[m] == 1

    @pl.when(jnp.logical_and(valid, _new_expert(be_ref, m)))
    def _():
        def body(r, c):
            r0 = pl.multiple_of(r * CAST_ROWS, CAST_ROWS)
            wb_ref[pl.ds(r0, CAST_ROWS), :] = w_ref[pl.ds(r0, CAST_ROWS), :].astype(BF16)
            return c
        lax.fori_loop(0, w_ref.shape[0] // CAST_ROWS, body, 0)

    @pl.when(valid)
    def _():
        even = lax.broadcasted_iota(jnp.int32, (x_ref.shape[0], LANES), 1) % 2 == 0
        hfull = jnp.dot(x_ref[...], wb_ref[...], preferred_element_type=F32) + b_ref[...]

        def act_block(c):
            hblk = hfull[:, c * LANES:(c + 1) * LANES]
            x_glu = jnp.minimum(hblk, SWIGLU_LIMIT)
            x_lin = jnp.clip(hblk, -SWIGLU_LIMIT, SWIGLU_LIMIT) + 1.0
            return x_glu * _sigmoid(SWIGLU_ALPHA * x_glu) * pltpu.roll(x_lin, LANES - 1, 1)

        for i in range(o_ref.shape[1] // LANES):
            packed = jnp.where(even, act_block(2 * i), pltpu.roll(act_block(2 * i + 1), 1, 1))
            o_ref[:, i * LANES:(i + 1) * LANES] = packed.astype(o_ref.dtype)

    @pl.when(bv_ref[m] == 0)
    def _():
        o_ref[...] = jnp.zeros_like(o_ref)


def _gmm1(block_expert, block_valid, x_buf, w1, b1, bm, tn):
    R, D = x_buf.shape
    E, _, F2 = w1.shape
    nb = R // bm
    assert D % CAST_ROWS == 0 and tn % (2 * LANES) == 0
    grid_spec = pltpu.PrefetchScalarGridSpec(
        num_scalar_prefetch=2,
        grid=(F2 // tn, nb),
        in_specs=[
            pl.BlockSpec((bm, D), lambda n, m, be, bv: (m, 0)),
            pl.BlockSpec((None, D, tn), lambda n, m, be, bv: (be[m], 0, n)),
            pl.BlockSpec((None, 1, tn), lambda n, m, be, bv: (be[m], 0, n)),
        ],
        out_specs=pl.BlockSpec((bm, tn // 2), lambda n, m, be, bv: (m, n)),
        scratch_shapes=[pltpu.VMEM((D, tn), BF16)],
    )
    return pl.pallas_call(
        _gmm1_kernel,
        grid_spec=grid_spec,
        out_shape=jax.ShapeDtypeStruct((R, F2 // 2), BF16),
        compiler_params=_cparams(("arbitrary", "arbitrary")),
        name="gmm1",
    )(block_expert, block_valid, x_buf, w1, b1)


def _gmm2_kernel(be_ref, bv_ref, a_ref, w_ref, b_ref, o_ref, wb_ref):
    m = pl.program_id(1)
    valid = bv_ref[m] == 1

    @pl.when(jnp.logical_and(valid, _new_expert(be_ref, m)))
    def _():
        half = LANES // 2

        def body(i, c):
            r0 = pl.multiple_of(i * LANES, LANES)
            words = _pack_pair(w_ref[pl.ds(r0, half), :], w_ref[pl.ds(r0 + half, half), :])
            wb_ref[pl.ds(r0, LANES), :] = pltpu.bitcast(words, BF16)
            return c
        lax.fori_loop(0, w_ref.shape[0] // LANES, body, 0)

    @pl.when(valid)
    def _():
        o_ref[...] = jnp.dot(a_ref[...], wb_ref[...], preferred_element_type=F32) + b_ref[...]

    @pl.when(bv_ref[m] == 0)
    def _():
        o_ref[...] = jnp.zeros_like(o_ref)


def _gmm2(block_expert, block_valid, act, w2, b2, bm, tn):
    R, F = act.shape
    E, _, D = w2.shape
    nb = R // bm
    grid_spec = pltpu.PrefetchScalarGridSpec(
        num_scalar_prefetch=2,
        grid=(D // tn, nb),
        in_specs=[
            pl.BlockSpec((bm, F), lambda n, m, be, bv: (m, 0)),
            pl.BlockSpec((None, F, tn), lambda n, m, be, bv: (be[m], 0, n)),
            pl.BlockSpec((None, 1, tn), lambda n, m, be, bv: (be[m], 0, n)),
        ],
        out_specs=pl.BlockSpec((bm, tn), lambda n, m, be, bv: (m, n)),
        scratch_shapes=[pltpu.VMEM((F, tn), BF16)],
    )
    return pl.pallas_call(
        _gmm2_kernel,
        grid_spec=grid_spec,
        out_shape=jax.ShapeDtypeStruct((R, D), F32),
        compiler_params=_cparams(("arbitrary", "arbitrary")),
        name="gmm2",
    )(block_expert, block_valid, act, w2, b2)


def _combine_kernel(dcur_ref, dnext_ref, gate_ref, x1_ref, nw_ref, y_hbm, o_ref, ybuf, sem, *, tm):
    i = pl.program_id(0)
    n = pl.num_programs(0)
    slot = lax.rem(i, 2)

    def copy(d_ref, s, r, k):
        return pltpu.make_async_copy(y_hbm.at[pl.ds(d_ref[0, 0, r * TOP_K + k], 1)],
                                     ybuf.at[s, k, pl.ds(r, 1)], sem.at[s])

    def issue(d_ref, s):
        def body(r, c):
            for k in range(TOP_K):
                copy(d_ref, s, r, k).start()
            return c
        lax.fori_loop(0, tm, body, 0, unroll=4)

    @pl.when(i == 0)
    def _():
        issue(dcur_ref, 0)

    @pl.when(i + 1 < n)
    def _():
        issue(dnext_ref, 1 - slot)

    for k in range(TOP_K):
        pltpu.make_async_copy(y_hbm.at[pl.ds(0, tm)], ybuf.at[slot, k], sem.at[slot]).wait()

    lane = lax.broadcasted_iota(jnp.int32, (tm, LANES), 1)
    gates = gate_ref[...]
    acc = x1_ref[...]
    for k in range(TOP_K):
        gk = jnp.sum(jnp.where(lane == k, gates, 0.0), axis=-1, keepdims=True)
        acc = acc + gk * ybuf[slot, k]
    out = acc * lax.rsqrt(jnp.mean(acc * acc, axis=-1, keepdims=True) + RMS_EPS) * nw_ref[...]
    o_ref[...] = out


def _combine(dest, gates, x1, norm_f_w, y_buf, tm):
    T, D = x1.shape
    nt = T // tm
    d3 = dest.reshape(nt, 1, tm * TOP_K)
    return pl.pallas_call(
        functools.partial(_combine_kernel, tm=tm),
        grid=(nt,),
        in_specs=[
            pl.BlockSpec((1, 1, tm * TOP_K), lambda i: (i, 0, 0), memory_space=pltpu.SMEM),
            pl.BlockSpec((1, 1, tm * TOP_K), lambda i: (jnp.minimum(i + 1, nt - 1), 0, 0),
                         memory_space=pltpu.SMEM),
            pl.BlockSpec((tm, LANES), lambda i: (i, 0)),
            pl.BlockSpec((tm, D), lambda i: (i, 0)),
            pl.BlockSpec((1, D), lambda i: (0, 0)),
            pl.BlockSpec(memory_space=pl.ANY),
        ],
        out_specs=pl.BlockSpec((tm, D), lambda i: (i, 0)),
        out_shape=jax.ShapeDtypeStruct((T, D), F32),
        scratch_shapes=[pltpu.VMEM((2, TOP_K, tm, D), F32), pltpu.SemaphoreType.DMA((2,))],
        compiler_params=_cparams(("arbitrary",)),
        name="combine",
    )(d3, d3, gates, x1, norm_f_w.reshape(1, D).astype(F32), y_buf)


def _routing(top_idx, n_experts, bm):
    T = top_idx.shape[0]
    A = T * TOP_K
    flat_e = top_idx.reshape(A)
    onehot = (flat_e[:, None] == jnp.arange(n_experts, dtype=jnp.int32)[None, :]).astype(jnp.int32)
    csum = jnp.cumsum(onehot, axis=0)
    rank = jnp.sum((csum - onehot) * onehot, axis=1)
    counts = csum[-1]
    padded = (counts + bm - 1) // bm * bm
    pend = jnp.cumsum(padded)
    pstart = pend - padded
    dest = jnp.sum(onehot * pstart[None, :], axis=1) + rank
    R = (A + bm - 1) // bm * bm + n_experts * bm
    nb = R // bm
    row_src = jnp.zeros((R,), jnp.int32).at[dest].set(jnp.arange(A, dtype=jnp.int32) // TOP_K)
    block_start = jnp.arange(nb, dtype=jnp.int32) * bm
    block_expert = jnp.minimum(jnp.searchsorted(pend, block_start, side='right'),
                               n_experts - 1).astype(jnp.int32)
    block_valid = (block_start < pend[-1]).astype(jnp.int32)
    return dest.astype(jnp.int32), row_src, block_expert, block_valid


def _pick(pref, n):
    t = min(pref, n)
    while n % t:
        t //= 2
    return t


def kernel(x, norm1_w, w_in, b_gate, conv_w, A_log, dt_bias, onorm_w, w_branch_a, w_pool,
           pool_scale, w_branch_p, w_out, norm2_w, router_w, router_b, w1, b1, w2, b2, norm_f_w):
    B, S, D = x.shape
    T = B * S
    depth = norm1_w.shape[0]
    x2 = x.reshape(T, D)
    for l in range(depth):
        conv_ch = conv_w.shape[-1]
        v_width = w_branch_a.shape[1]
        qk_width = (conv_ch - v_width) // 2
        n_v = A_log.shape[-1]
        qkvz = conv_ch + v_width
        ab_lo, ab_hi = qkvz, qkvz + 2 * n_v
        E = router_w.shape[-1]

        wl = w_in[l]
        w_main = jnp.concatenate([wl[:, :qkvz], wl[:, ab_hi:]], axis=1).astype(BF16)
        w_ab = jnp.zeros((D, LANES), BF16).at[:, :2 * n_v].set(wl[:, ab_lo:ab_hi].astype(BF16))
        p_off, ga_off, gp_off = qkvz, qkvz + D, qkvz + 2 * D

        proj, ab = _in_proj(x2, norm1_w[l], w_main, w_ab, _pick(1024, T), _pick(1024, w_main.shape[1]))
        proj3 = proj.reshape(B, S, -1)
        G = 8 if (qk_width // HEAD) % 8 == 0 else 2
        o_a = _delta(proj3, ab.reshape(B, S, LANES), conv_w[l], A_log[l], dt_bias[l], onorm_w[l],
                     qk_width, v_width, G, _pick(512, S))
        o_p = _pool(proj3, w_pool[l], pool_scale[l], p_off, D)
        h = _branch(o_a.reshape(T, v_width), o_p.reshape(T, D), proj,
                    w_branch_a[l].astype(BF16), w_branch_p[l].astype(BF16), b_gate[l],
                    ga_off, gp_off, _pick(1024, T), _pick(512, D))
        x1, xp, idx_pad, gate_pad = _post(x2, h, w_out[l].astype(BF16), norm2_w[l],
                                          router_w[l], router_b[l], _pick(512, T))

        bm = 256
        dest, row_src, block_expert, block_valid = _routing(idx_pad[:, :TOP_K], E, bm)
        x_buf = _gather_rows(xp, row_src, _pick(512, row_src.shape[0]))
        act = _gmm1(block_expert, block_valid, x_buf, w1[l].astype(F32),
                    b1[l][:, None, :].astype(F32), bm, _pick(1024, w1.shape[-1]))
        y_buf = _gmm2(block_expert, block_valid, act, w2[l].astype(F32),
                      b2[l][:, None, :].astype(F32), bm, _pick(1024, D))
        is_last = l == depth - 1
        assert is_last, "final norm is fused into the last layer's combine"
        x2 = _combine(dest, gate_pad, x1, norm_f_w, y_buf, _pick(128, T))
    return x2.reshape(B, S, D)
```

```python
import functools
import math

import jax
import jax.numpy as jnp
from jax import lax
from jax.experimental import pallas as pl
from jax.experimental.pallas import tpu as pltpu

F32 = jnp.float32
BF16 = jnp.bfloat16

HEAD = 128
CHUNK = 64
CONV_WIDTH = 4
POOL_WINDOWS = (2, 4, 8, 16)
TOP_K = 4
SWIGLU_ALPHA = 1.702
SWIGLU_LIMIT = 7.0
RMS_EPS = 1e-6
L2_EPS = 1e-6
LANES = 128
HIGHEST = lax.Precision.HIGHEST
VMEM_LIMIT = 56 * 1024 * 1024


def _cparams(sem):
    return pltpu.CompilerParams(dimension_semantics=sem, vmem_limit_bytes=VMEM_LIMIT)


def _sigmoid(x):
    return 1.0 / (1.0 + jnp.exp(-x))


def _bdot(a, b):
    return jnp.dot(a.astype(BF16), b.astype(BF16), preferred_element_type=F32)


def _inproj_kernel(x_ref, nw_ref, w_ref, wab_ref, o_ref, ab_ref, xn_ref, *, rows):
    @pl.when(pl.program_id(1) == 0)
    def _():
        def body(r, c):
            r0 = pl.multiple_of(r * rows, rows)
            x = x_ref[pl.ds(r0, rows), :]
            inv = lax.rsqrt(jnp.mean(x * x, axis=-1, keepdims=True) + RMS_EPS)
            xn_ref[pl.ds(r0, rows), :] = (x * inv * nw_ref[...]).astype(BF16)
            return c
        lax.fori_loop(0, x_ref.shape[0] // rows, body, 0)
        ab_ref[...] = jnp.dot(xn_ref[...], wab_ref[...], preferred_element_type=F32)

    o_ref[...] = jnp.dot(xn_ref[...], w_ref[...], preferred_element_type=F32).astype(o_ref.dtype)


def _in_proj(x2, norm_w, w_main, w_ab, tm, tn):
    T, D = x2.shape
    N = w_main.shape[1]
    rows = min(128, tm)
    return pl.pallas_call(
        functools.partial(_inproj_kernel, rows=rows),
        grid=(T // tm, N // tn),
        in_specs=[
            pl.BlockSpec((tm, D), lambda i, j: (i, 0)),
            pl.BlockSpec((1, D), lambda i, j: (0, 0)),
            pl.BlockSpec((D, tn), lambda i, j: (0, j)),
            pl.BlockSpec((D, LANES), lambda i, j: (0, 0)),
        ],
        out_specs=[
            pl.BlockSpec((tm, tn), lambda i, j: (i, j)),
            pl.BlockSpec((tm, LANES), lambda i, j: (i, 0)),
        ],
        out_shape=[jax.ShapeDtypeStruct((T, N), BF16), jax.ShapeDtypeStruct((T, LANES), F32)],
        scratch_shapes=[pltpu.VMEM((tm, D), BF16)],
        compiler_params=_cparams(("arbitrary", "arbitrary")),
        name="in_proj",
    )(x2, norm_w.reshape(1, D), w_main, w_ab)


def _delta_kernel(q_ref, k_ref, v_ref, z_ref, ab_ref, qp_ref, kp_ref, vp_ref, cwq_ref, cwk_ref, cwv_ref,
                  alog_ref, dtb_ref, onw_ref, o_ref,
                  s_ref, qs_ref, ks_ref, vs_ref, t_ref, a_ref, gc_ref, vl_ref, *, G, H, unroll_a):
    St = q_ref.shape[0]
    C = CHUNK
    WIDE = 4 * C
    NW = G // 2
    t_idx = pl.program_id(2)
    j = pl.program_id(1)
    PRE = qp_ref.shape[0]

    @pl.when(t_idx == 0)
    def _():
        s_ref[...] = jnp.zeros_like(s_ref)

    row = lax.broadcasted_iota(jnp.int32, (C, WIDE), 0)
    lanew = lax.broadcasted_iota(jnp.int32, (C, WIDE), 1)
    colw = lanew % C
    blk = lanew // C
    incl_w = row >= colw
    strict_w = row > colw
    eye_w = jnp.where(row == colw, 1.0, 0.0).astype(F32)
    first_half = (lax.broadcasted_iota(jnp.int32, (C, 2 * C), 1) < C)
    r2 = lax.broadcasted_iota(jnp.int32, (C, C), 0)
    c2 = lax.broadcasted_iota(jnp.int32, (C, C), 1)
    tril = jnp.where(r2 >= c2, 1.0, 0.0).astype(F32)
    lane = lax.broadcasted_iota(jnp.int32, (C, LANES), 1)
    shift = lax.rem(LANES - 2 * G * j, LANES)

    def conv_silu(raw_ref, halo_ref, cw_ref, c, r0, c0, width):
        cur = raw_ref[pl.ds(r0, C), c0:c0 + width].astype(F32)
        rp = pl.multiple_of(jnp.maximum(r0 - PRE, 0), PRE)
        inside = raw_ref[pl.ds(rp, PRE), c0:c0 + width].astype(F32)
        halo = jnp.where(t_idx > 0, halo_ref[:, c0:c0 + width].astype(F32), 0.0)
        prev = jnp.where(c > 0, inside, halo)
        ext = jnp.concatenate([prev, cur], axis=0)
        y = cur * cw_ref[CONV_WIDTH - 1:CONV_WIDTH, c0:c0 + width]
        for i in range(CONV_WIDTH - 1):
            sh = CONV_WIDTH - 1 - i
            y = y + pltpu.roll(ext, sh, 0)[PRE:PRE + C] * cw_ref[i:i + 1, c0:c0 + width]
        return y * _sigmoid(y)

    def pair_cols(x, c0):
        a = jnp.broadcast_to(x[:, c0:c0 + 1], (C, 2 * C))
        b = jnp.broadcast_to(x[:, c0 + 1:c0 + 2], (C, 2 * C))
        return jnp.where(first_half, a, b)

    blk_mask = [jnp.where(blk == e, 1.0, 0.0).astype(BF16) for e in range(4)]

    def block_diag(mw):
        return jnp.concatenate([mw * blk_mask[e] for e in range(4)], axis=0)

    def prep(i, carry):
        chunks = [i * unroll_a + u for u in range(unroll_a)]
        r0s = [pl.multiple_of(c * C, C) for c in chunks]
        gcs, valss = [], []
        for r0 in r0s:
            ab = ab_ref[pl.ds(r0, C), :]
            sp_in = ab + dtb_ref[...]
            softplus = jnp.maximum(sp_in, 0.0) + jnp.log(1.0 + jnp.exp(-jnp.abs(sp_in)))
            gval = -jnp.exp(alog_ref[...]) * softplus
            vals = pltpu.roll(jnp.where(lane < H, gval, _sigmoid(ab)), shift, 1)
            gc = jnp.dot(tril, vals, preferred_element_type=F32, precision=HIGHEST)
            gc_ref[pl.ds(r0, C), :] = gc
            vl_ref[pl.ds(r0, C), :] = vals
            gcs.append(gc)
            valss.append(vals)
        qbs, kbs = [], []
        for c, r0 in zip(chunks, r0s):
            for h in range(G):
                q = conv_silu(q_ref, qp_ref, cwq_ref, c, r0, h * HEAD, HEAD)
                k = conv_silu(k_ref, kp_ref, cwk_ref, c, r0, h * HEAD, HEAD)
                v = conv_silu(v_ref, vp_ref, cwv_ref, c, r0, 2 * h * HEAD, 2 * HEAD)
                q = q * lax.rsqrt(jnp.sum(q * q, axis=-1, keepdims=True) + L2_EPS) * (HEAD ** -0.5)
                k = k * lax.rsqrt(jnp.sum(k * k, axis=-1, keepdims=True) + L2_EPS)
                qb = q.astype(BF16)
                kb = k.astype(BF16)
                qs_ref[pl.ds(r0, C), h * HEAD:(h + 1) * HEAD] = qb
                ks_ref[pl.ds(r0, C), h * HEAD:(h + 1) * HEAD] = kb
                vs_ref[pl.ds(r0, C), 2 * h * HEAD:2 * (h + 1) * HEAD] = v
                qbs.append(qb)
                kbs.append(kb)
        qkk = [lax.dot_general(jnp.concatenate([qb, kb], axis=0), jnp.concatenate([kb, kb], axis=0),
                               (((1,), (1,)), ((), ())), preferred_element_type=F32)
               for qb, kb in zip(qbs, kbs)]
        ps, ms = [], []
        for u, r0 in enumerate(r0s):
            for w in range(NW):
                top, bot = qkk[u * G + 2 * w], qkk[u * G + 2 * w + 1]
                qk_w = jnp.concatenate([top[:C], bot[:C]], axis=1)
                kk_w = jnp.concatenate([top[C:], bot[C:]], axis=1)
                gcol = jnp.concatenate([pair_cols(gcs[u], 4 * w), pair_cols(gcs[u], 4 * w + 2)], axis=1)
                bcol = jnp.concatenate([pair_cols(valss[u], H + 4 * w),
                                        pair_cols(valss[u], H + 4 * w + 2)], axis=1)
                grow = jnp.sum(gcol * eye_w, axis=0, keepdims=True)
                dec = jnp.exp(jnp.where(incl_w, gcol - grow, -jnp.inf))
                lmat = jnp.where(strict_w, kk_w * dec * bcol, 0.0)
                a_ref[pl.ds(r0, C), w * WIDE:(w + 1) * WIDE] = (qk_w * dec).astype(BF16)
                ps.append(eye_w - lmat)
                ms.append(lmat)
        ms = [_bdot(m, block_diag(m.astype(BF16))) for m in ms]
        for _ in range(int(math.log2(C)) - 2):
            outs = [jnp.dot(jnp.concatenate([p, m], axis=0).astype(BF16), block_diag(m.astype(BF16)),
                            preferred_element_type=F32) for p, m in zip(ps, ms)]
            ps = [p + o[:C] for p, o in zip(ps, outs)]
            ms = [o[C:] for o in outs]
        outs = [_bdot(p, block_diag(m.astype(BF16))) for p, m in zip(ps, ms)]
        for u, r0 in enumerate(r0s):
            for w in range(NW):
                t_ref[pl.ds(r0, C), w * WIDE:(w + 1) * WIDE] = (ps[u * NW + w] + outs[u * NW + w]).astype(BF16)
        return carry

    lax.fori_loop(0, St // (C * unroll_a), prep, 0)

    def scan_chunk(c, carry):
        r0 = pl.multiple_of(c * C, C)
        gc = gc_ref[pl.ds(r0, C), :]
        vals = vl_ref[pl.ds(r0, C), :]
        kbs = [ks_ref[pl.ds(r0, C), h * HEAD:(h + 1) * HEAD] for h in range(G)]
        ks = [jnp.dot(jnp.concatenate([qs_ref[pl.ds(r0, C), h * HEAD:(h + 1) * HEAD], kbs[h]], axis=0),
                      jnp.concatenate([s_ref[2 * h], s_ref[2 * h + 1]], axis=1).astype(BF16),
                      preferred_element_type=F32) for h in range(G)]
        egs, gcols, rhss = [], [], []
        for hv in range(2 * G):
            h, e = divmod(hv, 2)
            gcol = jnp.broadcast_to(gc[:, hv:hv + 1], (C, LANES))
            bcol = jnp.broadcast_to(vals[:, H + hv:H + hv + 1], (C, LANES))
            eg = jnp.exp(gcol)
            v = vs_ref[pl.ds(r0, C), hv * HEAD:(hv + 1) * HEAD]
            rhss.append((bcol * (v - eg * ks[h][C:, e * HEAD:(e + 1) * HEAD])).astype(BF16))
            egs.append(eg)
            gcols.append(gcol)
        vnews = [jnp.dot(t_ref[pl.ds(r0, C), hv * C:(hv + 1) * C], rhss[hv],
                         preferred_element_type=F32) for hv in range(2 * G)]
        intra = [jnp.dot(a_ref[pl.ds(r0, C), hv * C:(hv + 1) * C], vnews[hv].astype(BF16),
                         preferred_element_type=F32) for hv in range(2 * G)]
        vnd = [(jnp.exp(gcols[hv][C - 1:C, :] - gcols[hv]) * vnews[hv]).astype(BF16)
               for hv in range(2 * G)]
        upd = [lax.dot_general(kbs[h], jnp.concatenate([vnd[2 * h], vnd[2 * h + 1]], axis=1),
                               (((0,), (0,)), ((), ())), preferred_element_type=F32)
               for h in range(G)]
        for hv in range(2 * G):
            h, e = divmod(hv, 2)
            s_ref[hv] = (s_ref[hv] * jnp.exp(gcols[hv][C - 1:C, :])
                         + upd[h][:, e * HEAD:(e + 1) * HEAD])
            o = egs[hv] * ks[h][:C, e * HEAD:(e + 1) * HEAD] + intra[hv]
            zz = z_ref[pl.ds(r0, C), hv * HEAD:(hv + 1) * HEAD].astype(F32)
            on = o * lax.rsqrt(jnp.mean(o * o, axis=-1, keepdims=True) + RMS_EPS)
            on = on * onw_ref[...] * (zz * _sigmoid(zz))
            o_ref[pl.ds(r0, C), hv * HEAD:(hv + 1) * HEAD] = on.astype(o_ref.dtype)
        return carry

    lax.fori_loop(0, St // C, scan_chunk, 0)


DELTA_HALO = 16


def _delta(proj3, ab3, conv_w, a_log, dt_bias, onorm_w, qk_width, v_width, G, St):
    B, S, _ = proj3.shape
    H = a_log.shape[0]
    n_groups = qk_width // (HEAD * G)
    qw, vw = HEAD * G, 2 * HEAD * G
    k_blk0 = qk_width // qw
    v_blk0 = (2 * qk_width) // vw
    z_blk0 = (2 * qk_width + v_width) // vw
    assert (2 * qk_width) % vw == 0 and (2 * qk_width + v_width) % vw == 0
    assert 2 * H <= LANES and S % CHUNK == 0
    assert G % 2 == 0 and S % St == 0 and St % CHUNK == 0 and CONV_WIDTH - 1 <= DELTA_HALO
    hb = St // DELTA_HALO
    halo = lambda t: jnp.maximum(t * hb - 1, 0)
    pad = lambda p: jnp.zeros((1, LANES), F32).at[0, :H].set(p.astype(F32))
    conv_w = conv_w.astype(F32)
    return pl.pallas_call(
        functools.partial(_delta_kernel, G=G, H=H, unroll_a=2 if (St // CHUNK) % 2 == 0 else 1),
        grid=(B, n_groups, S // St),
        in_specs=[
            pl.BlockSpec((None, St, qw), lambda b, j, t: (b, t, j)),
            pl.BlockSpec((None, St, qw), lambda b, j, t: (b, t, k_blk0 + j)),
            pl.BlockSpec((None, St, vw), lambda b, j, t: (b, t, v_blk0 + j)),
            pl.BlockSpec((None, St, vw), lambda b, j, t: (b, t, z_blk0 + j)),
            pl.BlockSpec((None, St, LANES), lambda b, j, t: (b, t, 0)),
            pl.BlockSpec((None, DELTA_HALO, qw), lambda b, j, t: (b, halo(t), j)),
            pl.BlockSpec((None, DELTA_HALO, qw), lambda b, j, t: (b, halo(t), k_blk0 + j)),
            pl.BlockSpec((None, DELTA_HALO, vw), lambda b, j, t: (b, halo(t), v_blk0 + j)),
            pl.BlockSpec((CONV_WIDTH, qw), lambda b, j, t: (0, j)),
            pl.BlockSpec((CONV_WIDTH, qw), lambda b, j, t: (0, k_blk0 + j)),
            pl.BlockSpec((CONV_WIDTH, vw), lambda b, j, t: (0, v_blk0 + j)),
            pl.BlockSpec((1, LANES), lambda b, j, t: (0, 0)),
            pl.BlockSpec((1, LANES), lambda b, j, t: (0, 0)),
            pl.BlockSpec((1, HEAD), lambda b, j, t: (0, 0)),
        ],
        out_specs=pl.BlockSpec((None, St, vw), lambda b, j, t: (b, t, j)),
        out_shape=jax.ShapeDtypeStruct((B, S, v_width), BF16),
        scratch_shapes=[
            pltpu.VMEM((2 * G, HEAD, HEAD), F32),
            pltpu.VMEM((St, qw), BF16),
            pltpu.VMEM((St, qw), BF16),
            pltpu.VMEM((St, vw), F32),
            pltpu.VMEM((St, qw), BF16),
            pltpu.VMEM((St, qw), BF16),
            pltpu.VMEM((St, LANES), F32),
            pltpu.VMEM((St, LANES), F32),
        ],
        compiler_params=_cparams(("arbitrary", "arbitrary", "arbitrary")),
        name="delta",
    )(proj3, proj3, proj3, proj3, ab3, proj3, proj3, proj3, conv_w, conv_w, conv_w,
      pad(a_log), pad(dt_bias), onorm_w.reshape(1, HEAD).astype(F32))


POOL_HALO = 16


def _pool_kernel(p_ref, w_ref, sc_ref, o_ref, pad_ref, *, rt):
    S = p_ref.shape[0]
    g = pl.program_id(1)
    pad_ref[0:POOL_HALO, :] = jnp.zeros((POOL_HALO, pad_ref.shape[1]), F32)
    pad_ref[POOL_HALO:, :] = p_ref[...].astype(F32)
    win = jnp.left_shift(2, g).astype(F32)

    def body(r, c):
        r0 = pl.multiple_of(r * rt, rt)
        x = pad_ref[pl.ds(r0, rt + POOL_HALO), :]
        y1 = x + pltpu.roll(x, 1, 0)
        y2 = y1 + pltpu.roll(y1, 2, 0)
        y3 = y2 + pltpu.roll(y2, 4, 0)
        y4 = y3 + pltpu.roll(y3, 8, 0)
        ysel = jnp.where(g == 0, y1, jnp.where(g == 1, y2, jnp.where(g == 2, y3, y4)))
        pos = (r0 + 1 + lax.broadcasted_iota(jnp.int32, (rt, 1), 0)).astype(F32)
        cnt = jnp.minimum(pos, win)
        mixed = ysel[POOL_HALO:] / cnt - x[POOL_HALO:]
        out = _bdot(mixed, w_ref[...]) * sc_ref[...]
        o_ref[pl.ds(r0, rt), :] = out.astype(o_ref.dtype)
        return c

    lax.fori_loop(0, S // rt, body, 0)


def _pool(proj3, w_pool, pool_scale, p_off, d_model):
    B, S, _ = proj3.shape
    ng, cg, _ = w_pool.shape
    assert POOL_WINDOWS == tuple(2 << i for i in range(ng)) and max(POOL_WINDOWS) <= POOL_HALO
    assert p_off % cg == 0
    blk0 = p_off // cg
    rt = min(256, S)
    return pl.pallas_call(
        functools.partial(_pool_kernel, rt=rt),
        grid=(B, ng),
        in_specs=[
            pl.BlockSpec((None, S, cg), lambda b, g: (b, 0, blk0 + g)),
            pl.BlockSpec((None, cg, cg), lambda b, g: (g, 0, 0)),
            pl.BlockSpec((1, cg), lambda b, g: (0, g)),
        ],
        out_specs=pl.BlockSpec((None, S, cg), lambda b, g: (b, 0, g)),
        out_shape=jax.ShapeDtypeStruct((B, S, d_model), BF16),
        scratch_shapes=[pltpu.VMEM((S + POOL_HALO, cg), F32)],
        compiler_params=_cparams(("arbitrary", "arbitrary")),
        name="pool",
    )(proj3, w_pool.astype(BF16), pool_scale.reshape(1, -1).astype(F32))


def _branch_kernel(oa_ref, op_ref, wa_ref, wp_ref, ga_ref, gp_ref, bga_ref, bgp_ref, h_ref):
    a = jnp.dot(oa_ref[...], wa_ref[...], preferred_element_type=F32)
    p = jnp.dot(op_ref[...], wp_ref[...], preferred_element_type=F32)
    ga = _sigmoid(ga_ref[...].astype(F32) + bga_ref[...])
    gp = _sigmoid(gp_ref[...].astype(F32) + bgp_ref[...])
    h_ref[...] = (ga * a + gp * p).astype(h_ref.dtype)


def _branch(o_a, o_p, proj, w_a, w_p, b_gate, ga_off, gp_off, tm, tn):
    T, V = o_a.shape
    D = w_a.shape[1]
    assert ga_off % tn == 0 and gp_off % tn == 0
    ga0, gp0 = ga_off // tn, gp_off // tn
    nd = D // tn
    bg = b_gate.reshape(1, 2 * D).astype(F32)
    return pl.pallas_call(
        _branch_kernel,
        grid=(T // tm, nd),
        in_specs=[
            pl.BlockSpec((tm, V), lambda i, j: (i, 0)),
            pl.BlockSpec((tm, D), lambda i, j: (i, 0)),
            pl.BlockSpec((V, tn), lambda i, j: (0, j)),
            pl.BlockSpec((D, tn), lambda i, j: (0, j)),
            pl.BlockSpec((tm, tn), lambda i, j: (i, ga0 + j)),
            pl.BlockSpec((tm, tn), lambda i, j: (i, gp0 + j)),
            pl.BlockSpec((1, tn), lambda i, j: (0, j)),
            pl.BlockSpec((1, tn), lambda i, j: (0, nd + j)),
        ],
        out_specs=pl.BlockSpec((tm, tn), lambda i, j: (i, j)),
        out_shape=jax.ShapeDtypeStruct((T, D), BF16),
        compiler_params=_cparams(("arbitrary", "arbitrary")),
        name="branch",
    )(o_a, o_p, w_a, w_p, proj, proj, bg, bg)


def _pack_pair(lo, hi):
    lo = lax.bitcast_convert_type(lo.astype(BF16).astype(F32), jnp.uint32)
    hi = lax.bitcast_convert_type(hi.astype(BF16).astype(F32), jnp.uint32)
    return jnp.right_shift(lo, jnp.uint32(16)) | (hi & jnp.uint32(0xFFFF0000))


def _pack_halves(x):
    half = x.shape[1] // 2
    return _pack_pair(x[:, :half], x[:, half:])


def _unpack_halves(u):
    lo = lax.bitcast_convert_type(jnp.left_shift(u, jnp.uint32(16)), F32)
    hi = lax.bitcast_convert_type(u & jnp.uint32(0xFFFF0000), F32)
    return lo, hi


def _split_bf16(x):
    hi = x.astype(BF16)
    return hi, (x - hi.astype(F32)).astype(BF16)


def _post_kernel(x_ref, h_ref, wo_ref, nw_ref, rwh_ref, rwl_ref, rb_ref,
                 x1_ref, xp_ref, idx_ref, gate_ref, *, n_experts):
    x1 = x_ref[...] + jnp.dot(h_ref[...], wo_ref[...], preferred_element_type=F32)
    x1_ref[...] = x1
    xn = x1 * lax.rsqrt(jnp.mean(x1 * x1, axis=-1, keepdims=True) + RMS_EPS) * nw_ref[...]
    xp_ref[...] = _pack_halves(xn)
    xh, xl = _split_bf16(xn)
    logits = (jnp.dot(xh, rwh_ref[...], preferred_element_type=F32)
              + jnp.dot(xh, rwl_ref[...], preferred_element_type=F32)
              + jnp.dot(xl, rwh_ref[...], preferred_element_type=F32) + rb_ref[...])
    lane = lax.broadcasted_iota(jnp.int32, logits.shape, 1)
    work = jnp.where(lane < n_experts, logits, -jnp.inf)
    vals, idxs = [], []
    for _ in range(TOP_K):
        m = jnp.max(work, axis=-1, keepdims=True)
        idx = jnp.min(jnp.where(work == m, lane, LANES), axis=-1, keepdims=True)
        vals.append(m)
        idxs.append(idx)
        work = jnp.where(lane == idx, -jnp.inf, work)
    ex = [jnp.exp(v - vals[0]) for v in vals]
    den = ex[0]
    for e in ex[1:]:
        den = den + e
    idx_out = jnp.zeros(logits.shape, jnp.int32)
    gate_out = jnp.zeros(logits.shape, F32)
    for k in range(TOP_K):
        idx_out = jnp.where(lane == k, idxs[k], idx_out)
        gate_out = jnp.where(lane == k, ex[k] / den, gate_out)
    idx_ref[...] = idx_out
    gate_ref[...] = gate_out


def _post(x2, h, w_out, norm2_w, router_w, router_b, tm):
    T, D = x2.shape
    E = router_w.shape[1]
    rw = jnp.zeros((D, LANES), F32).at[:, :E].set(router_w.astype(F32))
    rwh, rwl = _split_bf16(rw)
    rb = jnp.zeros((1, LANES), F32).at[0, :E].set(router_b.astype(F32))
    return pl.pallas_call(
        functools.partial(_post_kernel, n_experts=E),
        grid=(T // tm,),
        in_specs=[
            pl.BlockSpec((tm, D), lambda i: (i, 0)),
            pl.BlockSpec((tm, D), lambda i: (i, 0)),
            pl.BlockSpec((D, D), lambda i: (0, 0)),
            pl.BlockSpec((1, D), lambda i: (0, 0)),
            pl.BlockSpec((D, LANES), lambda i: (0, 0)),
            pl.BlockSpec((D, LANES), lambda i: (0, 0)),
            pl.BlockSpec((1, LANES), lambda i: (0, 0)),
        ],
        out_specs=[
            pl.BlockSpec((tm, D), lambda i: (i, 0)),
            pl.BlockSpec((tm, D // 2), lambda i: (i, 0)),
            pl.BlockSpec((tm, LANES), lambda i: (i, 0)),
            pl.BlockSpec((tm, LANES), lambda i: (i, 0)),
        ],
        out_shape=[
            jax.ShapeDtypeStruct((T, D), F32),
            jax.ShapeDtypeStruct((T, D // 2), jnp.uint32),
            jax.ShapeDtypeStruct((T, LANES), jnp.int32),
            jax.ShapeDtypeStruct((T, LANES), F32),
        ],
        compiler_params=_cparams(("arbitrary",)),
        name="post",
    )(x2, h, w_out, norm2_w.reshape(1, D).astype(F32), rwh, rwl, rb)


ISSUE_UNROLL = 8


def _gather_kernel(cur_ref, nxt_ref, x_hbm, o_ref, buf, sem, *, tile):
    i = pl.program_id(0)
    n = pl.num_programs(0)
    slot = lax.rem(i, 2)

    def issue(src_ref, s):
        def body(g, c):
            for u in range(ISSUE_UNROLL):
                r = g * ISSUE_UNROLL + u
                pltpu.make_async_copy(x_hbm.at[pl.ds(src_ref[0, 0, r], 1)],
                                      buf.at[s, pl.ds(r, 1)], sem.at[s]).start(priority=u % 2)
            return c
        lax.fori_loop(0, tile // ISSUE_UNROLL, body, 0)

    @pl.when(i == 0)
    def _():
        issue(cur_ref, 0)

    @pl.when(i + 1 < n)
    def _():
        issue(nxt_ref, 1 - slot)

    pltpu.make_async_copy(x_hbm.at[pl.ds(0, tile)], buf.at[slot], sem.at[slot]).wait()
    lo, hi = _unpack_halves(buf[slot])
    half = lo.shape[1]
    o_ref[:, :half] = lo.astype(o_ref.dtype)
    o_ref[:, half:] = hi.astype(o_ref.dtype)


def _gather_rows(xp, row_src, tile):
    R = row_src.shape[0]
    W = xp.shape[1]
    nt = R // tile
    src3 = row_src.reshape(nt, 1, tile)
    return pl.pallas_call(
        functools.partial(_gather_kernel, tile=tile),
        grid=(nt,),
        in_specs=[
            pl.BlockSpec((1, 1, tile), lambda i: (i, 0, 0), memory_space=pltpu.SMEM),
            pl.BlockSpec((1, 1, tile), lambda i: (jnp.minimum(i + 1, nt - 1), 0, 0),
                         memory_space=pltpu.SMEM),
            pl.BlockSpec(memory_space=pl.ANY),
        ],
        out_specs=pl.BlockSpec((tile, 2 * W), lambda i: (i, 0)),
        out_shape=jax.ShapeDtypeStruct((R, 2 * W), BF16),
        scratch_shapes=[pltpu.VMEM((2, tile, W), xp.dtype), pltpu.SemaphoreType.DMA((2,))],
        compiler_params=_cparams(("arbitrary",)),
        name="gather",
    )(src3, src3, xp)


def _new_expert(be_ref, m):
    return jnp.logical_or(m == 0, be_ref[m] != be_ref[jnp.maximum(m - 1, 0)])


CAST_ROWS = 256


def _gmm1_kernel(be_ref, bv_ref, x_ref, w_ref, b_ref, o_ref, wb_ref):
    m = pl.program_id(1)
    valid = bv_ref[m] == 1

    @pl.when(jnp.logical_and(valid, _new_expert(be_ref, m)))
    def _():
        def body(r, c):
            r0 = pl.multiple_of(r * CAST_ROWS, CAST_ROWS)
            wb_ref[pl.ds(r0, CAST_ROWS), :] = w_ref[pl.ds(r0, CAST_ROWS), :].astype(BF16)
            return c
        lax.fori_loop(0, w_ref.shape[0] // CAST_ROWS, body, 0)

    @pl.when(valid)
    def _():
        even = lax.broadcasted_iota(jnp.int32, (x_ref.shape[0], LANES), 1) % 2 == 0
        x = x_ref[...]

        def act_block(hblk):
            x_glu = jnp.minimum(hblk, SWIGLU_LIMIT)
            x_lin = jnp.clip(hblk, -SWIGLU_LIMIT, SWIGLU_LIMIT) + 1.0
            return x_glu * _sigmoid(SWIGLU_ALPHA * x_glu) * pltpu.roll(x_lin, LANES - 1, 1)

        for i in range(o_ref.shape[1] // LANES):
            sl = slice(2 * i * LANES, 2 * (i + 1) * LANES)
            hs = jnp.dot(x, wb_ref[:, sl], preferred_element_type=F32) + b_ref[:, sl]
            packed = jnp.where(even, act_block(hs[:, :LANES]), pltpu.roll(act_block(hs[:, LANES:]), 1, 1))
            o_ref[:, i * LANES:(i + 1) * LANES] = packed.astype(o_ref.dtype)

    @pl.when(bv_ref[m] == 0)
    def _():
        o_ref[...] = jnp.zeros_like(o_ref)


def _gmm1(block_expert, block_valid, x_buf, w1, b1, bm, tn):
    R, D = x_buf.shape
    E, _, F2 = w1.shape
    nb = R // bm
    assert D % CAST_ROWS == 0 and tn % (2 * LANES) == 0
    grid_spec = pltpu.PrefetchScalarGridSpec(
        num_scalar_prefetch=2,
        grid=(F2 // tn, nb),
        in_specs=[
            pl.BlockSpec((bm, D), lambda n, m, be, bv: (m, 0)),
            pl.BlockSpec((None, D, tn), lambda n, m, be, bv: (be[m], 0, n)),
            pl.BlockSpec((None, 1, tn), lambda n, m, be, bv: (be[m], 0, n)),
        ],
        out_specs=pl.BlockSpec((bm, tn // 2), lambda n, m, be, bv: (m, n)),
        scratch_shapes=[pltpu.VMEM((D, tn), BF16)],
    )
    return pl.pallas_call(
        _gmm1_kernel,
        grid_spec=grid_spec,
        out_shape=jax.ShapeDtypeStruct((R, F2 // 2), BF16),
        compiler_params=_cparams(("arbitrary", "arbitrary")),
        name="gmm1",
    )(block_expert, block_valid, x_buf, w1, b1)


def _gmm2_kernel(be_ref, bv_ref, a_ref, w_ref, b_ref, o_ref, wb_ref):
    m = pl.program_id(1)
    valid = bv_ref[m] == 1

    @pl.when(jnp.logical_and(valid, _new_expert(be_ref, m)))
    def _():
        half = LANES // 2

        def body(i, c):
            r0 = pl.multiple_of(i * LANES, LANES)
            words = _pack_pair(w_ref[pl.ds(r0, half), :], w_ref[pl.ds(r0 + half, half), :])
            wb_ref[pl.ds(r0, LANES), :] = pltpu.bitcast(words, BF16)
            return c
        lax.fori_loop(0, w_ref.shape[0] // LANES, body, 0)

    @pl.when(valid)
    def _():
        o_ref[...] = jnp.dot(a_ref[...], wb_ref[...], preferred_element_type=F32) + b_ref[...]

    @pl.when(bv_ref[m] == 0)
    def _():
        o_ref[...] = jnp.zeros_like(o_ref)


def _gmm2(block_expert, block_valid, act, w2, b2, bm, tn):
    R, F = act.shape
    E, _, D = w2.shape
    nb = R // bm
    grid_spec = pltpu.PrefetchScalarGridSpec(
        num_scalar_prefetch=2,
        grid=(D // tn, nb),
        in_specs=[
            pl.BlockSpec((bm, F), lambda n, m, be, bv: (m, 0)),
            pl.BlockSpec((None, F, tn), lambda n, m, be, bv: (be[m], 0, n)),
            pl.BlockSpec((None, 1, tn), lambda n, m, be, bv: (be[m], 0, n)),
        ],
        out_specs=pl.BlockSpec((bm, tn), lambda n, m, be, bv: (m, n)),
        scratch_shapes=[pltpu.VMEM((F, tn), BF16)],
    )
    return pl.pallas_call(
        _gmm2_kernel,
        grid_spec=grid_spec,
        out_shape=jax.ShapeDtypeStruct((R, D), F32),
        compiler_params=_cparams(("arbitrary", "arbitrary")),
        name="gmm2",
    )(block_expert, block_valid, act, w2, b2)


def _combine_kernel(dcur_ref, dnext_ref, gate_ref, x1_ref, nw_ref, y_hbm, o_ref, ybuf, sem, *, tm):
    i = pl.program_id(0)
    n = pl.num_programs(0)
    slot = lax.rem(i, 2)

    def copy(d_ref, s, r, k):
        return pltpu.make_async_copy(y_hbm.at[pl.ds(d_ref[0, 0, r * TOP_K + k], 1)],
                                     ybuf.at[s, k, pl.ds(r, 1)], sem.at[s])

    def issue(d_ref, s):
        def body(g, c):
            for u in range(ISSUE_UNROLL // TOP_K):
                for k in range(TOP_K):
                    copy(d_ref, s, g * (ISSUE_UNROLL // TOP_K) + u, k).start(priority=k % 2)
            return c
        lax.fori_loop(0, tm // (ISSUE_UNROLL // TOP_K), body, 0)

    @pl.when(i == 0)
    def _():
        issue(dcur_ref, 0)

    @pl.when(i + 1 < n)
    def _():
        issue(dnext_ref, 1 - slot)

    for k in range(TOP_K):
        pltpu.make_async_copy(y_hbm.at[pl.ds(0, tm)], ybuf.at[slot, k], sem.at[slot]).wait()

    lane = lax.broadcasted_iota(jnp.int32, (tm, LANES), 1)
    gates = gate_ref[...]
    acc = x1_ref[...]
    for k in range(TOP_K):
        gk = jnp.sum(jnp.where(lane == k, gates, 0.0), axis=-1, keepdims=True)
        acc = acc + gk * ybuf[slot, k]
    out = acc * lax.rsqrt(jnp.mean(acc * acc, axis=-1, keepdims=True) + RMS_EPS) * nw_ref[...]
    o_ref[...] = out


def _combine(dest, gates, x1, norm_f_w, y_buf, tm):
    T, D = x1.shape
    nt = T // tm
    d3 = dest.reshape(nt, 1, tm * TOP_K)
    return pl.pallas_call(
        functools.partial(_combine_kernel, tm=tm),
        grid=(nt,),
        in_specs=[
            pl.BlockSpec((1, 1, tm * TOP_K), lambda i: (i, 0, 0), memory_space=pltpu.SMEM),
            pl.BlockSpec((1, 1, tm * TOP_K), lambda i: (jnp.minimum(i + 1, nt - 1), 0, 0),
                         memory_space=pltpu.SMEM),
            pl.BlockSpec((tm, LANES), lambda i: (i, 0)),
            pl.BlockSpec((tm, D), lambda i: (i, 0)),
            pl.BlockSpec((1, D), lambda i: (0, 0)),
            pl.BlockSpec(memory_space=pl.ANY),
        ],
        out_specs=pl.BlockSpec((tm, D), lambda i: (i, 0)),
        out_shape=jax.ShapeDtypeStruct((T, D), F32),
        scratch_shapes=[pltpu.VMEM((2, TOP_K, tm, D), F32), pltpu.SemaphoreType.DMA((2,))],
        compiler_params=_cparams(("arbitrary",)),
        name="combine",
    )(d3, d3, gates, x1, norm_f_w.reshape(1, D).astype(F32), y_buf)


def _routing(top_idx, n_experts, bm):
    T = top_idx.shape[0]
    A = T * TOP_K
    flat_e = top_idx.reshape(A)
    onehot = (flat_e[:, None] == jnp.arange(n_experts, dtype=jnp.int32)[None, :]).astype(jnp.int32)
    csum = jnp.cumsum(onehot, axis=0)
    rank = jnp.sum((csum - onehot) * onehot, axis=1)
    counts = csum[-1]
    padded = (counts + bm - 1) // bm * bm
    pend = jnp.cumsum(padded)
    pstart = pend - padded
    dest = jnp.sum(onehot * pstart[None, :], axis=1) + rank
    R = (A + bm - 1) // bm * bm + n_experts * bm
    nb = R // bm
    row_src = jnp.zeros((R,), jnp.int32).at[dest].set(jnp.arange(A, dtype=jnp.int32) // TOP_K)
    block_start = jnp.arange(nb, dtype=jnp.int32) * bm
    block_expert = jnp.minimum(jnp.sum((block_start[:, None] >= pend[None, :]).astype(jnp.int32), axis=1),
                               n_experts - 1)
    block_valid = (block_start < pend[-1]).astype(jnp.int32)
    return dest.astype(jnp.int32), row_src, block_expert, block_valid


def _pick(pref, n):
    t = min(pref, n)
    while n % t:
        t //= 2
    return t


def kernel(x, norm1_w, w_in, b_gate, conv_w, A_log, dt_bias, onorm_w, w_branch_a, w_pool,
           pool_scale, w_branch_p, w_out, norm2_w, router_w, router_b, w1, b1, w2, b2, norm_f_w):
    B, S, D = x.shape
    T = B * S
    depth = norm1_w.shape[0]
    x2 = x.reshape(T, D)
    for l in range(depth):
        conv_ch = conv_w.shape[-1]
        v_width = w_branch_a.shape[1]
        qk_width = (conv_ch - v_width) // 2
        n_v = A_log.shape[-1]
        qkvz = conv_ch + v_width
        ab_lo, ab_hi = qkvz, qkvz + 2 * n_v
        E = router_w.shape[-1]

        wl = w_in[l]
        w_main = jnp.concatenate([wl[:, :qkvz], wl[:, ab_hi:]], axis=1).astype(BF16)
        w_ab = jnp.zeros((D, LANES), BF16).at[:, :2 * n_v].set(wl[:, ab_lo:ab_hi].astype(BF16))
        p_off, ga_off, gp_off = qkvz, qkvz + D, qkvz + 2 * D

        proj, ab = _in_proj(x2, norm1_w[l], w_main, w_ab, _pick(1024, T), _pick(1024, w_main.shape[1]))
        proj3 = proj.reshape(B, S, -1)
        G = 8 if (qk_width // HEAD) % 8 == 0 else 2
        o_a = _delta(proj3, ab.reshape(B, S, LANES), conv_w[l], A_log[l], dt_bias[l], onorm_w[l],
                     qk_width, v_width, G, _pick(512, S))
        o_p = _pool(proj3, w_pool[l], pool_scale[l], p_off, D)
        h = _branch(o_a.reshape(T, v_width), o_p.reshape(T, D), proj,
                    w_branch_a[l].astype(BF16), w_branch_p[l].astype(BF16), b_gate[l],
                    ga_off, gp_off, _pick(1024, T), _pick(512, D))
        x1, xp, idx_pad, gate_pad = _post(x2, h, w_out[l].astype(BF16), norm2_w[l],
                                          router_w[l], router_b[l], _pick(512, T))

        bm = 512
        dest, row_src, block_expert, block_valid = _routing(idx_pad[:, :TOP_K], E, bm)
        x_buf = _gather_rows(xp, row_src, _pick(512, row_src.shape[0]))
        act = _gmm1(block_expert, block_valid, x_buf, w1[l].astype(F32),
                    b1[l][:, None, :].astype(F32), bm, _pick(1024, w1.shape[-1]))
        y_buf = _gmm2(block_expert, block_valid, act, w2[l].astype(F32),
                      b2[l][:, None, :].astype(F32), bm, _pick(1024, D))
        is_last = l == depth - 1
        assert is_last, "final norm is fused into the last layer's combine"
        x2 = _combine(dest, gate_pad, x1, norm_f_w, y_buf, _pick(128, T))
    return x2.reshape(B, S, D)
```

```python
import functools
import math

import jax
import jax.numpy as jnp
from jax import lax
from jax.experimental import pallas as pl
from jax.experimental.pallas import tpu as pltpu

F32 = jnp.float32
BF16 = jnp.bfloat16

HEAD = 128
CHUNK = 64
CONV_WIDTH = 4
POOL_WINDOWS = (2, 4, 8, 16)
TOP_K = 4
SWIGLU_ALPHA = 1.702
SWIGLU_LIMIT = 7.0
RMS_EPS = 1e-6
L2_EPS = 1e-6
LANES = 128
HIGHEST = lax.Precision.HIGHEST
VMEM_LIMIT = 56 * 1024 * 1024


def _cparams(sem):
    return pltpu.CompilerParams(dimension_semantics=sem, vmem_limit_bytes=VMEM_LIMIT)


def _sigmoid(x):
    return 1.0 / (1.0 + jnp.exp(-x))


def _bdot(a, b):
    return jnp.dot(a.astype(BF16), b.astype(BF16), preferred_element_type=F32)


def _inproj_kernel(x_ref, nw_ref, w_ref, wab_ref, o_ref, ab_ref, xn_ref, *, rows):
    @pl.when(pl.program_id(1) == 0)
    def _():
        def body(r, c):
            r0 = pl.multiple_of(r * rows, rows)
            x = x_ref[pl.ds(r0, rows), :]
            inv = lax.rsqrt(jnp.mean(x * x, axis=-1, keepdims=True) + RMS_EPS)
            xn_ref[pl.ds(r0, rows), :] = (x * inv * nw_ref[...]).astype(BF16)
            return c
        lax.fori_loop(0, x_ref.shape[0] // rows, body, 0)
        ab_ref[...] = jnp.dot(xn_ref[...], wab_ref[...], preferred_element_type=F32)

    o_ref[...] = jnp.dot(xn_ref[...], w_ref[...], preferred_element_type=F32).astype(o_ref.dtype)


def _in_proj(x2, norm_w, w_main, w_ab, tm, tn):
    T, D = x2.shape
    N = w_main.shape[1]
    rows = min(128, tm)
    return pl.pallas_call(
        functools.partial(_inproj_kernel, rows=rows),
        grid=(T // tm, N // tn),
        in_specs=[
            pl.BlockSpec((tm, D), lambda i, j: (i, 0)),
            pl.BlockSpec((1, D), lambda i, j: (0, 0)),
            pl.BlockSpec((D, tn), lambda i, j: (0, j)),
            pl.BlockSpec((D, LANES), lambda i, j: (0, 0)),
        ],
        out_specs=[
            pl.BlockSpec((tm, tn), lambda i, j: (i, j)),
            pl.BlockSpec((tm, LANES), lambda i, j: (i, 0)),
        ],
        out_shape=[jax.ShapeDtypeStruct((T, N), BF16), jax.ShapeDtypeStruct((T, LANES), F32)],
        scratch_shapes=[pltpu.VMEM((tm, D), BF16)],
        compiler_params=_cparams(("arbitrary", "arbitrary")),
        name="in_proj",
    )(x2, norm_w.reshape(1, D), w_main, w_ab)


def _delta_kernel(q_ref, k_ref, v_ref, z_ref, ab_ref, qp_ref, kp_ref, vp_ref, cwq_ref, cwk_ref, cwv_ref,
                  alog_ref, dtb_ref, onw_ref, o_ref,
                  s_ref, qs_ref, ks_ref, vs_ref, t_ref, a_ref, gc_ref, vl_ref, *, G, H, unroll_a):
    St = q_ref.shape[0]
    C = CHUNK
    WIDE = 4 * C
    NW = G // 2
    t_idx = pl.program_id(2)
    j = pl.program_id(1)
    PRE = qp_ref.shape[0]

    @pl.when(t_idx == 0)
    def _():
        s_ref[...] = jnp.zeros_like(s_ref)

    row = lax.broadcasted_iota(jnp.int32, (C, WIDE), 0)
    lanew = lax.broadcasted_iota(jnp.int32, (C, WIDE), 1)
    colw = lanew % C
    blk = lanew // C
    incl_w = row >= colw
    strict_w = row > colw
    eye_w = jnp.where(row == colw, 1.0, 0.0).astype(F32)
    first_half = (lax.broadcasted_iota(jnp.int32, (C, 2 * C), 1) < C)
    r2 = lax.broadcasted_iota(jnp.int32, (C, C), 0)
    c2 = lax.broadcasted_iota(jnp.int32, (C, C), 1)
    tril = jnp.where(r2 >= c2, 1.0, 0.0).astype(F32)
    lane = lax.broadcasted_iota(jnp.int32, (C, LANES), 1)
    shift = lax.rem(LANES - 2 * G * j, LANES)

    def conv_silu(raw_ref, halo_ref, cw_ref, c, r0, c0, width):
        cur = raw_ref[pl.ds(r0, C), c0:c0 + width].astype(F32)
        rp = pl.multiple_of(jnp.maximum(r0 - PRE, 0), PRE)
        inside = raw_ref[pl.ds(rp, PRE), c0:c0 + width].astype(F32)
        halo = jnp.where(t_idx > 0, halo_ref[:, c0:c0 + width].astype(F32), 0.0)
        prev = jnp.where(c > 0, inside, halo)
        ext = jnp.concatenate([prev, cur], axis=0)
        y = cur * cw_ref[CONV_WIDTH - 1:CONV_WIDTH, c0:c0 + width]
        for i in range(CONV_WIDTH - 1):
            sh = CONV_WIDTH - 1 - i
            y = y + pltpu.roll(ext, sh, 0)[PRE:PRE + C] * cw_ref[i:i + 1, c0:c0 + width]
        return y * _sigmoid(y)

    def pair_cols(x, c0):
        a = jnp.broadcast_to(x[:, c0:c0 + 1], (C, 2 * C))
        b = jnp.broadcast_to(x[:, c0 + 1:c0 + 2], (C, 2 * C))
        return jnp.where(first_half, a, b)

    blk_mask = [jnp.where(blk == e, 1.0, 0.0).astype(BF16) for e in range(4)]

    def block_diag(mw):
        return jnp.concatenate([mw * blk_mask[e] for e in range(4)], axis=0)

    def prep(i, carry):
        chunks = [i * unroll_a + u for u in range(unroll_a)]
        r0s = [pl.multiple_of(c * C, C) for c in chunks]
        gcs, valss = [], []
        for r0 in r0s:
            ab = ab_ref[pl.ds(r0, C), :]
            sp_in = ab + dtb_ref[...]
            softplus = jnp.maximum(sp_in, 0.0) + jnp.log(1.0 + jnp.exp(-jnp.abs(sp_in)))
            gval = -jnp.exp(alog_ref[...]) * softplus
            vals = pltpu.roll(jnp.where(lane < H, gval, _sigmoid(ab)), shift, 1)
            gc = jnp.dot(tril, vals, preferred_element_type=F32, precision=HIGHEST)
            gc_ref[pl.ds(r0, C), :] = gc
            vl_ref[pl.ds(r0, C), :] = vals
            gcs.append(gc)
            valss.append(vals)
        qbs, kbs = [], []
        for c, r0 in zip(chunks, r0s):
            for h in range(G):
                q = conv_silu(q_ref, qp_ref, cwq_ref, c, r0, h * HEAD, HEAD)
                k = conv_silu(k_ref, kp_ref, cwk_ref, c, r0, h * HEAD, HEAD)
                v = conv_silu(v_ref, vp_ref, cwv_ref, c, r0, 2 * h * HEAD, 2 * HEAD)
                q = q * lax.rsqrt(jnp.sum(q * q, axis=-1, keepdims=True) + L2_EPS) * (HEAD ** -0.5)
                k = k * lax.rsqrt(jnp.sum(k * k, axis=-1, keepdims=True) + L2_EPS)
                qb = q.astype(BF16)
                kb = k.astype(BF16)
                qs_ref[pl.ds(r0, C), h * HEAD:(h + 1) * HEAD] = qb
                ks_ref[pl.ds(r0, C), h * HEAD:(h + 1) * HEAD] = kb
                vs_ref[pl.ds(r0, C), 2 * h * HEAD:2 * (h + 1) * HEAD] = v
                qbs.append(qb)
                kbs.append(kb)
        qkk = [lax.dot_general(jnp.concatenate([qb, kb], axis=0), jnp.concatenate([kb, kb], axis=0),
                               (((1,), (1,)), ((), ())), preferred_element_type=F32)
               for qb, kb in zip(qbs, kbs)]
        ps, ms = [], []
        for u, r0 in enumerate(r0s):
            for w in range(NW):
                top, bot = qkk[u * G + 2 * w], qkk[u * G + 2 * w + 1]
                qk_w = jnp.concatenate([top[:C], bot[:C]], axis=1)
                kk_w = jnp.concatenate([top[C:], bot[C:]], axis=1)
                gcol = jnp.concatenate([pair_cols(gcs[u], 4 * w), pair_cols(gcs[u], 4 * w + 2)], axis=1)
                bcol = jnp.concatenate([pair_cols(valss[u], H + 4 * w),
                                        pair_cols(valss[u], H + 4 * w + 2)], axis=1)
                grow = jnp.sum(gcol * eye_w, axis=0, keepdims=True)
                dec = jnp.exp(jnp.where(incl_w, gcol - grow, -jnp.inf))
                lmat = jnp.where(strict_w, kk_w * dec * bcol, 0.0)
                a_ref[pl.ds(r0, C), w * WIDE:(w + 1) * WIDE] = (qk_w * dec).astype(BF16)
                ps.append(eye_w - lmat)
                ms.append(lmat)
        ms = [_bdot(m, block_diag(m.astype(BF16))) for m in ms]
        for _ in range(int(math.log2(C)) - 2):
            outs = [jnp.dot(jnp.concatenate([p, m], axis=0).astype(BF16), block_diag(m.astype(BF16)),
                            preferred_element_type=F32) for p, m in zip(ps, ms)]
            ps = [p + o[:C] for p, o in zip(ps, outs)]
            ms = [o[C:] for o in outs]
        outs = [_bdot(p, block_diag(m.astype(BF16))) for p, m in zip(ps, ms)]
        for u, r0 in enumerate(r0s):
            for w in range(NW):
                t_ref[pl.ds(r0, C), w * WIDE:(w + 1) * WIDE] = (ps[u * NW + w] + outs[u * NW + w]).astype(BF16)
        return carry

    lax.fori_loop(0, St // (C * unroll_a), prep, 0)

    def scan_chunk(c, carry):
        r0 = pl.multiple_of(c * C, C)
        gc = gc_ref[pl.ds(r0, C), :]
        vals = vl_ref[pl.ds(r0, C), :]
        kbs = [ks_ref[pl.ds(r0, C), h * HEAD:(h + 1) * HEAD] for h in range(G)]
        ks = [jnp.dot(jnp.concatenate([qs_ref[pl.ds(r0, C), h * HEAD:(h + 1) * HEAD], kbs[h]], axis=0),
                      jnp.concatenate([s_ref[2 * h], s_ref[2 * h + 1]], axis=1).astype(BF16),
                      preferred_element_type=F32) for h in range(G)]
        egs, gcols, rhss = [], [], []
        for hv in range(2 * G):
            h, e = divmod(hv, 2)
            gcol = jnp.broadcast_to(gc[:, hv:hv + 1], (C, LANES))
            bcol = jnp.broadcast_to(vals[:, H + hv:H + hv + 1], (C, LANES))
            eg = jnp.exp(gcol)
            v = vs_ref[pl.ds(r0, C), hv * HEAD:(hv + 1) * HEAD]
            rhss.append((bcol * (v - eg * ks[h][C:, e * HEAD:(e + 1) * HEAD])).astype(BF16))
            egs.append(eg)
            gcols.append(gcol)
        vnews = [jnp.dot(t_ref[pl.ds(r0, C), hv * C:(hv + 1) * C], rhss[hv],
                         preferred_element_type=F32) for hv in range(2 * G)]
        intra = [jnp.dot(a_ref[pl.ds(r0, C), hv * C:(hv + 1) * C], vnews[hv].astype(BF16),
                         preferred_element_type=F32) for hv in range(2 * G)]
        vnd = [(jnp.exp(gcols[hv][C - 1:C, :] - gcols[hv]) * vnews[hv]).astype(BF16)
               for hv in range(2 * G)]
        upd = [lax.dot_general(kbs[h], jnp.concatenate([vnd[2 * h], vnd[2 * h + 1]], axis=1),
                               (((0,), (0,)), ((), ())), preferred_element_type=F32)
               for h in range(G)]
        for hv in range(2 * G):
            h, e = divmod(hv, 2)
            s_ref[hv] = (s_ref[hv] * jnp.exp(gcols[hv][C - 1:C, :])
                         + upd[h][:, e * HEAD:(e + 1) * HEAD])
            o = egs[hv] * ks[h][:C, e * HEAD:(e + 1) * HEAD] + intra[hv]
            zz = z_ref[pl.ds(r0, C), hv * HEAD:(hv + 1) * HEAD].astype(F32)
            on = o * lax.rsqrt(jnp.mean(o * o, axis=-1, keepdims=True) + RMS_EPS)
            on = on * onw_ref[...] * (zz * _sigmoid(zz))
            o_ref[pl.ds(r0, C), hv * HEAD:(hv + 1) * HEAD] = on.astype(o_ref.dtype)
        return carry

    lax.fori_loop(0, St // C, scan_chunk, 0)


DELTA_HALO = 16


def _delta(proj3, ab3, conv_w, a_log, dt_bias, onorm_w, qk_width, v_width, G, St):
    B, S, _ = proj3.shape
    H = a_log.shape[0]
    n_groups = qk_width // (HEAD * G)
    qw, vw = HEAD * G, 2 * HEAD * G
    k_blk0 = qk_width // qw
    v_blk0 = (2 * qk_width) // vw
    z_blk0 = (2 * qk_width + v_width) // vw
    assert (2 * qk_width) % vw == 0 and (2 * qk_width + v_width) % vw == 0
    assert 2 * H <= LANES and S % CHUNK == 0
    assert G % 2 == 0 and S % St == 0 and St % CHUNK == 0 and CONV_WIDTH - 1 <= DELTA_HALO
    hb = St // DELTA_HALO
    halo = lambda t: jnp.maximum(t * hb - 1, 0)
    pad = lambda p: jnp.zeros((1, LANES), F32).at[0, :H].set(p.astype(F32))
    conv_w = conv_w.astype(F32)
    return pl.pallas_call(
        functools.partial(_delta_kernel, G=G, H=H, unroll_a=2 if (St // CHUNK) % 2 == 0 else 1),
        grid=(B, n_groups, S // St),
        in_specs=[
            pl.BlockSpec((None, St, qw), lambda b, j, t: (b, t, j)),
            pl.BlockSpec((None, St, qw), lambda b, j, t: (b, t, k_blk0 + j)),
            pl.BlockSpec((None, St, vw), lambda b, j, t: (b, t, v_blk0 + j)),
            pl.BlockSpec((None, St, vw), lambda b, j, t: (b, t, z_blk0 + j)),
            pl.BlockSpec((None, St, LANES), lambda b, j, t: (b, t, 0)),
            pl.BlockSpec((None, DELTA_HALO, qw), lambda b, j, t: (b, halo(t), j)),
            pl.BlockSpec((None, DELTA_HALO, qw), lambda b, j, t: (b, halo(t), k_blk0 + j)),
            pl.BlockSpec((None, DELTA_HALO, vw), lambda b, j, t: (b, halo(t), v_blk0 + j)),
            pl.BlockSpec((CONV_WIDTH, qw), lambda b, j, t: (0, j)),
            pl.BlockSpec((CONV_WIDTH, qw), lambda b, j, t: (0, k_blk0 + j)),
            pl.BlockSpec((CONV_WIDTH, vw), lambda b, j, t: (0, v_blk0 + j)),
            pl.BlockSpec((1, LANES), lambda b, j, t: (0, 0)),
            pl.BlockSpec((1, LANES), lambda b, j, t: (0, 0)),
            pl.BlockSpec((1, HEAD), lambda b, j, t: (0, 0)),
        ],
        out_specs=pl.BlockSpec((None, St, vw), lambda b, j, t: (b, t, j)),
        out_shape=jax.ShapeDtypeStruct((B, S, v_width), BF16),
        scratch_shapes=[
            pltpu.VMEM((2 * G, HEAD, HEAD), F32),
            pltpu.VMEM((St, qw), BF16),
            pltpu.VMEM((St, qw), BF16),
            pltpu.VMEM((St, vw), F32),
            pltpu.VMEM((St, qw), BF16),
            pltpu.VMEM((St, qw), BF16),
            pltpu.VMEM((St, LANES), F32),
            pltpu.VMEM((St, LANES), F32),
        ],
        compiler_params=_cparams(("arbitrary", "arbitrary", "arbitrary")),
        name="delta",
    )(proj3, proj3, proj3, proj3, ab3, proj3, proj3, proj3, conv_w, conv_w, conv_w,
      pad(a_log), pad(dt_bias), onorm_w.reshape(1, HEAD).astype(F32))


POOL_HALO = 16


def _pool_kernel(p_ref, w_ref, sc_ref, o_ref, pad_ref, *, rt):
    S = p_ref.shape[0]
    g = pl.program_id(1)
    pad_ref[0:POOL_HALO, :] = jnp.zeros((POOL_HALO, pad_ref.shape[1]), F32)
    pad_ref[POOL_HALO:, :] = p_ref[...].astype(F32)
    win = jnp.left_shift(2, g).astype(F32)

    def body(r, c):
        r0 = pl.multiple_of(r * rt, rt)
        x = pad_ref[pl.ds(r0, rt + POOL_HALO), :]
        y1 = x + pltpu.roll(x, 1, 0)
        y2 = y1 + pltpu.roll(y1, 2, 0)
        y3 = y2 + pltpu.roll(y2, 4, 0)
        y4 = y3 + pltpu.roll(y3, 8, 0)
        ysel = jnp.where(g == 0, y1, jnp.where(g == 1, y2, jnp.where(g == 2, y3, y4)))
        pos = (r0 + 1 + lax.broadcasted_iota(jnp.int32, (rt, 1), 0)).astype(F32)
        cnt = jnp.minimum(pos, win)
        mixed = ysel[POOL_HALO:] / cnt - x[POOL_HALO:]
        out = _bdot(mixed, w_ref[...]) * sc_ref[...]
        o_ref[pl.ds(r0, rt), :] = out.astype(o_ref.dtype)
        return c

    lax.fori_loop(0, S // rt, body, 0)


def _pool(proj3, w_pool, pool_scale, p_off, d_model):
    B, S, _ = proj3.shape
    ng, cg, _ = w_pool.shape
    assert POOL_WINDOWS == tuple(2 << i for i in range(ng)) and max(POOL_WINDOWS) <= POOL_HALO
    assert p_off % cg == 0
    blk0 = p_off // cg
    rt = min(256, S)
    return pl.pallas_call(
        functools.partial(_pool_kernel, rt=rt),
        grid=(B, ng),
        in_specs=[
            pl.BlockSpec((None, S, cg), lambda b, g: (b, 0, blk0 + g)),
            pl.BlockSpec((None, cg, cg), lambda b, g: (g, 0, 0)),
            pl.BlockSpec((1, cg), lambda b, g: (0, g)),
        ],
        out_specs=pl.BlockSpec((None, S, cg), lambda b, g: (b, 0, g)),
        out_shape=jax.ShapeDtypeStruct((B, S, d_model), BF16),
        scratch_shapes=[pltpu.VMEM((S + POOL_HALO, cg), F32)],
        compiler_params=_cparams(("arbitrary", "arbitrary")),
        name="pool",
    )(proj3, w_pool.astype(BF16), pool_scale.reshape(1, -1).astype(F32))


def _branch_kernel(oa_ref, op_ref, wa_ref, wp_ref, ga_ref, gp_ref, bga_ref, bgp_ref, h_ref):
    a = jnp.dot(oa_ref[...], wa_ref[...], preferred_element_type=F32)
    p = jnp.dot(op_ref[...], wp_ref[...], preferred_element_type=F32)
    ga = _sigmoid(ga_ref[...].astype(F32) + bga_ref[...])
    gp = _sigmoid(gp_ref[...].astype(F32) + bgp_ref[...])
    h_ref[...] = (ga * a + gp * p).astype(h_ref.dtype)


def _branch(o_a, o_p, proj, w_a, w_p, b_gate, ga_off, gp_off, tm, tn):
    T, V = o_a.shape
    D = w_a.shape[1]
    assert ga_off % tn == 0 and gp_off % tn == 0
    ga0, gp0 = ga_off // tn, gp_off // tn
    nd = D // tn
    bg = b_gate.reshape(1, 2 * D).astype(F32)
    return pl.pallas_call(
        _branch_kernel,
        grid=(T // tm, nd),
        in_specs=[
            pl.BlockSpec((tm, V), lambda i, j: (i, 0)),
            pl.BlockSpec((tm, D), lambda i, j: (i, 0)),
            pl.BlockSpec((V, tn), lambda i, j: (0, j)),
            pl.BlockSpec((D, tn), lambda i, j: (0, j)),
            pl.BlockSpec((tm, tn), lambda i, j: (i, ga0 + j)),
            pl.BlockSpec((tm, tn), lambda i, j: (i, gp0 + j)),
            pl.BlockSpec((1, tn), lambda i, j: (0, j)),
            pl.BlockSpec((1, tn), lambda i, j: (0, nd + j)),
        ],
        out_specs=pl.BlockSpec((tm, tn), lambda i, j: (i, j)),
        out_shape=jax.ShapeDtypeStruct((T, D), BF16),
        compiler_params=_cparams(("arbitrary", "arbitrary")),
        name="branch",
    )(o_a, o_p, w_a, w_p, proj, proj, bg, bg)


def _pack_pair(lo, hi):
    lo = lax.bitcast_convert_type(lo.astype(BF16).astype(F32), jnp.uint32)
    hi = lax.bitcast_convert_type(hi.astype(BF16).astype(F32), jnp.uint32)
    return jnp.right_shift(lo, jnp.uint32(16)) | (hi & jnp.uint32(0xFFFF0000))


def _pack_halves(x):
    half = x.shape[1] // 2
    return _pack_pair(x[:, :half], x[:, half:])


def _unpack_halves(u):
    lo = lax.bitcast_convert_type(jnp.left_shift(u, jnp.uint32(16)), F32)
    hi = lax.bitcast_convert_type(u & jnp.uint32(0xFFFF0000), F32)
    return lo, hi


def _split_bf16(x):
    hi = x.astype(BF16)
    return hi, (x - hi.astype(F32)).astype(BF16)


def _post_kernel(x_ref, h_ref, wo_ref, nw_ref, rwh_ref, rwl_ref, rb_ref,
                 x1_ref, xp_ref, idx_ref, gate_ref, *, n_experts):
    x1 = x_ref[...] + jnp.dot(h_ref[...], wo_ref[...], preferred_element_type=F32)
    x1_ref[...] = x1
    xn = x1 * lax.rsqrt(jnp.mean(x1 * x1, axis=-1, keepdims=True) + RMS_EPS) * nw_ref[...]
    xp_ref[...] = _pack_halves(xn)
    xh, xl = _split_bf16(xn)
    logits = (jnp.dot(xh, rwh_ref[...], preferred_element_type=F32)
              + jnp.dot(xh, rwl_ref[...], preferred_element_type=F32)
              + jnp.dot(xl, rwh_ref[...], preferred_element_type=F32) + rb_ref[...])
    lane = lax.broadcasted_iota(jnp.int32, logits.shape, 1)
    work = jnp.where(lane < n_experts, logits, -jnp.inf)
    vals, idxs = [], []
    for _ in range(TOP_K):
        m = jnp.max(work, axis=-1, keepdims=True)
        idx = jnp.min(jnp.where(work == m, lane, LANES), axis=-1, keepdims=True)
        vals.append(m)
        idxs.append(idx)
        work = jnp.where(lane == idx, -jnp.inf, work)
    ex = [jnp.exp(v - vals[0]) for v in vals]
    den = ex[0]
    for e in ex[1:]:
        den = den + e
    idx_out = jnp.zeros(logits.shape, jnp.int32)
    gate_out = jnp.zeros(logits.shape, F32)
    for k in range(TOP_K):
        idx_out = jnp.where(lane == k, idxs[k], idx_out)
        gate_out = jnp.where(lane == k, ex[k] / den, gate_out)
    idx_ref[...] = idx_out
    gate_ref[...] = gate_out


def _post(x2, h, w_out, norm2_w, router_w, router_b, tm):
    T, D = x2.shape
    E = router_w.shape[1]
    rw = jnp.zeros((D, LANES), F32).at[:, :E].set(router_w.astype(F32))
    rwh, rwl = _split_bf16(rw)
    rb = jnp.zeros((1, LANES), F32).at[0, :E].set(router_b.astype(F32))
    return pl.pallas_call(
        functools.partial(_post_kernel, n_experts=E),
        grid=(T // tm,),
        in_specs=[
            pl.BlockSpec((tm, D), lambda i: (i, 0)),
            pl.BlockSpec((tm, D), lambda i: (i, 0)),
            pl.BlockSpec((D, D), lambda i: (0, 0)),
            pl.BlockSpec((1, D), lambda i: (0, 0)),
            pl.BlockSpec((D, LANES), lambda i: (0, 0)),
            pl.BlockSpec((D, LANES), lambda i: (0, 0)),
            pl.BlockSpec((1, LANES), lambda i: (0, 0)),
        ],
        out_specs=[
            pl.BlockSpec((tm, D), lambda i: (i, 0)),
            pl.BlockSpec((tm, D // 2), lambda i: (i, 0)),
            pl.BlockSpec((tm, LANES), lambda i: (i, 0)),
            pl.BlockSpec((tm, LANES), lambda i: (i, 0)),
        ],
        out_shape=[
            jax.ShapeDtypeStruct((T, D), F32),
            jax.ShapeDtypeStruct((T, D // 2), jnp.uint32),
            jax.ShapeDtypeStruct((T, LANES), jnp.int32),
            jax.ShapeDtypeStruct((T, LANES), F32),
        ],
        compiler_params=_cparams(("arbitrary",)),
        name="post",
    )(x2, h, w_out, norm2_w.reshape(1, D).astype(F32), rwh, rwl, rb)


ISSUE_UNROLL = 8


def _dispatch_kernel(dest_ref, xp_ref, zero_hbm, o_hbm, sem, *, tm):
    del zero_hbm

    def body(g, c):
        for u in range(ISSUE_UNROLL // TOP_K):
            r = g * (ISSUE_UNROLL // TOP_K) + u
            for k in range(TOP_K):
                pltpu.make_async_copy(xp_ref.at[pl.ds(r, 1)],
                                      o_hbm.at[pl.ds(dest_ref[0, 0, r * TOP_K + k], 1)], sem).start()
        return c
    lax.fori_loop(0, tm // (ISSUE_UNROLL // TOP_K), body, 0)
    for _ in range(TOP_K):
        pltpu.make_async_copy(xp_ref, o_hbm.at[pl.ds(0, tm)], sem).wait()


def _dispatch_rows(xp, dest, n_rows, tm):
    T, W = xp.shape
    nt = T // tm
    return pl.pallas_call(
        functools.partial(_dispatch_kernel, tm=tm),
        grid=(nt,),
        in_specs=[
            pl.BlockSpec((1, 1, tm * TOP_K), lambda i: (i, 0, 0), memory_space=pltpu.SMEM),
            pl.BlockSpec((tm, W), lambda i: (i, 0)),
            pl.BlockSpec(memory_space=pl.ANY),
        ],
        out_specs=pl.BlockSpec(memory_space=pl.ANY),
        out_shape=jax.ShapeDtypeStruct((n_rows, W), xp.dtype),
        scratch_shapes=[pltpu.SemaphoreType.DMA(())],
        input_output_aliases={2: 0},
        compiler_params=_cparams(("arbitrary",)),
        name="dispatch",
    )(dest.reshape(nt, 1, tm * TOP_K), xp, jnp.zeros((n_rows, W), xp.dtype))


def _new_expert(be_ref, m):
    return jnp.logical_or(m == 0, be_ref[m] != be_ref[jnp.maximum(m - 1, 0)])


CAST_ROWS = 256


def _gmm1_kernel(be_ref, bv_ref, x_ref, w_ref, b_ref, o_ref, wb_ref):
    m = pl.program_id(1)
    valid = bv_ref[m] == 1

    @pl.when(jnp.logical_and(valid, _new_expert(be_ref, m)))
    def _():
        def body(r, c):
            r0 = pl.multiple_of(r * CAST_ROWS, CAST_ROWS)
            wb_ref[pl.ds(r0, CAST_ROWS), :] = w_ref[pl.ds(r0, CAST_ROWS), :].astype(BF16)
            return c
        lax.fori_loop(0, w_ref.shape[0] // CAST_ROWS, body, 0)

    @pl.when(valid)
    def _():
        even = lax.broadcasted_iota(jnp.int32, (x_ref.shape[0], LANES), 1) % 2 == 0
        lo, hi = _unpack_halves(x_ref[...])
        x = jnp.concatenate([lo.astype(BF16), hi.astype(BF16)], axis=1)

        def act_block(hblk):
            x_glu = jnp.minimum(hblk, SWIGLU_LIMIT)
            x_lin = jnp.clip(hblk, -SWIGLU_LIMIT, SWIGLU_LIMIT) + 1.0
            return x_glu * _sigmoid(SWIGLU_ALPHA * x_glu) * pltpu.roll(x_lin, LANES - 1, 1)

        for i in range(o_ref.shape[1] // LANES):
            sl = slice(2 * i * LANES, 2 * (i + 1) * LANES)
            hs = jnp.dot(x, wb_ref[:, sl], preferred_element_type=F32) + b_ref[:, sl]
            packed = jnp.where(even, act_block(hs[:, :LANES]), pltpu.roll(act_block(hs[:, LANES:]), 1, 1))
            o_ref[:, i * LANES:(i + 1) * LANES] = packed.astype(o_ref.dtype)

    @pl.when(bv_ref[m] == 0)
    def _():
        o_ref[...] = jnp.zeros_like(o_ref)


def _gmm1(block_expert, block_valid, x_buf, w1, b1, bm, tn):
    R, half = x_buf.shape
    E, D, F2 = w1.shape
    nb = R // bm
    assert D == 2 * half and D % CAST_ROWS == 0 and tn % (2 * LANES) == 0
    grid_spec = pltpu.PrefetchScalarGridSpec(
        num_scalar_prefetch=2,
        grid=(F2 // tn, nb),
        in_specs=[
            pl.BlockSpec((bm, half), lambda n, m, be, bv: (m, 0)),
            pl.BlockSpec((None, D, tn), lambda n, m, be, bv: (be[m], 0, n)),
            pl.BlockSpec((None, 1, tn), lambda n, m, be, bv: (be[m], 0, n)),
        ],
        out_specs=pl.BlockSpec((bm, tn // 2), lambda n, m, be, bv: (m, n)),
        scratch_shapes=[pltpu.VMEM((D, tn), BF16)],
    )
    return pl.pallas_call(
        _gmm1_kernel,
        grid_spec=grid_spec,
        out_shape=jax.ShapeDtypeStruct((R, F2 // 2), BF16),
        compiler_params=_cparams(("arbitrary", "arbitrary")),
        name="gmm1",
    )(block_expert, block_valid, x_buf, w1, b1)


def _gmm2_kernel(be_ref, bv_ref, a_ref, w_ref, b_ref, o_ref, wb_ref):
    m = pl.program_id(1)
    valid = bv_ref[m] == 1

    @pl.when(jnp.logical_and(valid, _new_expert(be_ref, m)))
    def _():
        half = LANES // 2

        def body(i, c):
            r0 = pl.multiple_of(i * LANES, LANES)
            words = _pack_pair(w_ref[pl.ds(r0, half), :], w_ref[pl.ds(r0 + half, half), :])
            wb_ref[pl.ds(r0, LANES), :] = pltpu.bitcast(words, BF16)
            return c
        lax.fori_loop(0, w_ref.shape[0] // LANES, body, 0)

    @pl.when(valid)
    def _():
        y = jnp.dot(a_ref[...], wb_ref[...], preferred_element_type=F32) + b_ref[...]
        o_ref[...] = _pack_halves(y)

    @pl.when(bv_ref[m] == 0)
    def _():
        o_ref[...] = jnp.zeros_like(o_ref)


def _gmm2(block_expert, block_valid, act, w2, b2, bm, tn):
    R, F = act.shape
    E, _, D = w2.shape
    nb = R // bm
    assert tn % (2 * LANES) == 0
    grid_spec = pltpu.PrefetchScalarGridSpec(
        num_scalar_prefetch=2,
        grid=(D // tn, nb),
        in_specs=[
            pl.BlockSpec((bm, F), lambda n, m, be, bv: (m, 0)),
            pl.BlockSpec((None, F, tn), lambda n, m, be, bv: (be[m], 0, n)),
            pl.BlockSpec((None, 1, tn), lambda n, m, be, bv: (be[m], 0, n)),
        ],
        out_specs=pl.BlockSpec((bm, tn // 2), lambda n, m, be, bv: (m, n)),
        scratch_shapes=[pltpu.VMEM((F, tn), BF16)],
    )
    return pl.pallas_call(
        _gmm2_kernel,
        grid_spec=grid_spec,
        out_shape=jax.ShapeDtypeStruct((R, D // 2), jnp.uint32),
        compiler_params=_cparams(("arbitrary", "arbitrary")),
        name="gmm2",
    )(block_expert, block_valid, act, w2, b2)


def _combine_kernel(dcur_ref, dnext_ref, gate_ref, x1_ref, nw_ref, y_hbm, o_ref, ybuf, sem, *, tm, y_tile):
    i = pl.program_id(0)
    n = pl.num_programs(0)
    slot = lax.rem(i, 2)

    def copy(d_ref, s, r, k):
        return pltpu.make_async_copy(y_hbm.at[pl.ds(d_ref[0, 0, r * TOP_K + k], 1)],
                                     ybuf.at[s, k, pl.ds(r, 1)], sem.at[s])

    def issue(d_ref, s):
        def body(g, c):
            for u in range(ISSUE_UNROLL // TOP_K):
                for k in range(TOP_K):
                    copy(d_ref, s, g * (ISSUE_UNROLL // TOP_K) + u, k).start()
            return c
        lax.fori_loop(0, tm // (ISSUE_UNROLL // TOP_K), body, 0)

    @pl.when(i == 0)
    def _():
        issue(dcur_ref, 0)

    @pl.when(i + 1 < n)
    def _():
        issue(dnext_ref, 1 - slot)

    for k in range(TOP_K):
        pltpu.make_async_copy(y_hbm.at[pl.ds(0, tm)], ybuf.at[slot, k], sem.at[slot]).wait()

    lane = lax.broadcasted_iota(jnp.int32, (tm, LANES), 1)
    gates = gate_ref[...]
    acc = x1_ref[...]
    hw = y_tile // 2
    for k in range(TOP_K):
        gk = jnp.sum(jnp.where(lane == k, gates, 0.0), axis=-1, keepdims=True)
        parts = []
        for jt in range(ybuf.shape[-1] // hw):
            lo, hi = _unpack_halves(ybuf[slot, k, :, jt * hw:(jt + 1) * hw])
            parts += [lo, hi]
        acc = acc + gk * jnp.concatenate(parts, axis=1)
    out = acc * lax.rsqrt(jnp.mean(acc * acc, axis=-1, keepdims=True) + RMS_EPS) * nw_ref[...]
    o_ref[...] = out


def _combine(dest, gates, x1, norm_f_w, y_buf, tm, y_tile):
    T, D = x1.shape
    nt = T // tm
    d3 = dest.reshape(nt, 1, tm * TOP_K)
    return pl.pallas_call(
        functools.partial(_combine_kernel, tm=tm, y_tile=y_tile),
        grid=(nt,),
        in_specs=[
            pl.BlockSpec((1, 1, tm * TOP_K), lambda i: (i, 0, 0), memory_space=pltpu.SMEM),
            pl.BlockSpec((1, 1, tm * TOP_K), lambda i: (jnp.minimum(i + 1, nt - 1), 0, 0),
                         memory_space=pltpu.SMEM),
            pl.BlockSpec((tm, LANES), lambda i: (i, 0)),
            pl.BlockSpec((tm, D), lambda i: (i, 0)),
            pl.BlockSpec((1, D), lambda i: (0, 0)),
            pl.BlockSpec(memory_space=pl.ANY),
        ],
        out_specs=pl.BlockSpec((tm, D), lambda i: (i, 0)),
        out_shape=jax.ShapeDtypeStruct((T, D), F32),
        scratch_shapes=[pltpu.VMEM((2, TOP_K, tm, D // 2), jnp.uint32), pltpu.SemaphoreType.DMA((2,))],
        compiler_params=_cparams(("arbitrary",)),
        name="combine",
    )(d3, d3, gates, x1, norm_f_w.reshape(1, D).astype(F32), y_buf)


def _routing(top_idx, n_experts, bm):
    T = top_idx.shape[0]
    A = T * TOP_K
    flat_e = top_idx.reshape(A)
    onehot = (flat_e[:, None] == jnp.arange(n_experts, dtype=jnp.int32)[None, :]).astype(jnp.int32)
    csum = jnp.cumsum(onehot, axis=0)
    rank = jnp.sum((csum - onehot) * onehot, axis=1)
    counts = csum[-1]
    padded = (counts + bm - 1) // bm * bm
    pend = jnp.cumsum(padded)
    pstart = pend - padded
    dest = jnp.sum(onehot * pstart[None, :], axis=1) + rank
    R = (A + bm - 1) // bm * bm + n_experts * bm
    nb = R // bm
    block_start = jnp.arange(nb, dtype=jnp.int32) * bm
    block_expert = jnp.minimum(jnp.sum((block_start[:, None] >= pend[None, :]).astype(jnp.int32), axis=1),
                               n_experts - 1)
    block_valid = (block_start < pend[-1]).astype(jnp.int32)
    return dest.astype(jnp.int32), R, block_expert, block_valid


def _pick(pref, n):
    t = min(pref, n)
    while n % t:
        t //= 2
    return t


def kernel(x, norm1_w, w_in, b_gate, conv_w, A_log, dt_bias, onorm_w, w_branch_a, w_pool,
           pool_scale, w_branch_p, w_out, norm2_w, router_w, router_b, w1, b1, w2, b2, norm_f_w):
    B, S, D = x.shape
    T = B * S
    depth = norm1_w.shape[0]
    x2 = x.reshape(T, D)
    for l in range(depth):
        conv_ch = conv_w.shape[-1]
        v_width = w_branch_a.shape[1]
        qk_width = (conv_ch - v_width) // 2
        n_v = A_log.shape[-1]
        qkvz = conv_ch + v_width
        ab_lo, ab_hi = qkvz, qkvz + 2 * n_v
        E = router_w.shape[-1]

        wl = w_in[l]
        w_main = jnp.concatenate([wl[:, :qkvz], wl[:, ab_hi:]], axis=1).astype(BF16)
        w_ab = jnp.zeros((D, LANES), BF16).at[:, :2 * n_v].set(wl[:, ab_lo:ab_hi].astype(BF16))
        p_off, ga_off, gp_off = qkvz, qkvz + D, qkvz + 2 * D

        proj, ab = _in_proj(x2, norm1_w[l], w_main, w_ab, _pick(1024, T), _pick(1024, w_main.shape[1]))
        proj3 = proj.reshape(B, S, -1)
        G = 8 if (qk_width // HEAD) % 8 == 0 else 2
        o_a = _delta(proj3, ab.reshape(B, S, LANES), conv_w[l], A_log[l], dt_bias[l], onorm_w[l],
                     qk_width, v_width, G, _pick(512, S))
        o_p = _pool(proj3, w_pool[l], pool_scale[l], p_off, D)
        h = _branch(o_a.reshape(T, v_width), o_p.reshape(T, D), proj,
                    w_branch_a[l].astype(BF16), w_branch_p[l].astype(BF16), b_gate[l],
                    ga_off, gp_off, _pick(1024, T), _pick(512, D))
        x1, xp, idx_pad, gate_pad = _post(x2, h, w_out[l].astype(BF16), norm2_w[l],
                                          router_w[l], router_b[l], _pick(512, T))

        bm = 512
        y_tile = _pick(1024, D)
        dest, n_rows, block_expert, block_valid = _routing(idx_pad[:, :TOP_K], E, bm)
        x_buf = _dispatch_rows(xp, dest, n_rows, _pick(512, T))
        act = _gmm1(block_expert, block_valid, x_buf, w1[l].astype(F32),
                    b1[l][:, None, :].astype(F32), bm, _pick(2048, w1.shape[-1]))
        y_buf = _gmm2(block_expert, block_valid, act, w2[l].astype(F32),
                      b2[l][:, None, :].astype(F32), bm, y_tile)
        is_last = l == depth - 1
        assert is_last, "final norm is fused into the last layer's combine"
        x2 = _combine(dest, gate_pad, x1, norm_f_w, y_buf, _pick(128, T), y_tile)
    return x2.reshape(B, S, D)
```

```python
import functools
import math

import jax
import jax.numpy as jnp
from jax import lax
from jax.experimental import pallas as pl
from jax.experimental.pallas import tpu as pltpu

F32 = jnp.float32
BF16 = jnp.bfloat16

HEAD = 128
CHUNK = 64
CONV_WIDTH = 4
POOL_WINDOWS = (2, 4, 8, 16)
TOP_K = 4
SWIGLU_ALPHA = 1.702
SWIGLU_LIMIT = 7.0
RMS_EPS = 1e-6
L2_EPS = 1e-6
LANES = 128
HIGHEST = lax.Precision.HIGHEST
VMEM_LIMIT = 56 * 1024 * 1024


def _cparams(sem):
    return pltpu.CompilerParams(dimension_semantics=sem, vmem_limit_bytes=VMEM_LIMIT)


def _sigmoid(x):
    return 1.0 / (1.0 + jnp.exp(-x))


def _bdot(a, b):
    return jnp.dot(a.astype(BF16), b.astype(BF16), preferred_element_type=F32)


CONV_HALO = 8


def _inproj_kernel(x_ref, nw_ref, w_ref, wab_ref, cw_ref, o_ref, ab_ref, xn_ref, halo_ref, *,
                   rows, n_q, n_k, n_conv, tiles_per_seq):
    i = pl.program_id(0)
    j = pl.program_id(1)
    tm, tn = o_ref.shape

    @pl.when(j == 0)
    def _():
        def body(r, c):
            r0 = pl.multiple_of(r * rows, rows)
            x = x_ref[pl.ds(r0, rows), :]
            inv = lax.rsqrt(jnp.mean(x * x, axis=-1, keepdims=True) + RMS_EPS)
            xn_ref[pl.ds(r0, rows), :] = (x * inv * nw_ref[...]).astype(BF16)
            return c
        lax.fori_loop(0, tm // rows, body, 0)
        ab_ref[...] = jnp.dot(xn_ref[...], wab_ref[...], preferred_element_type=F32)

    @pl.when(j >= n_conv)
    def _():
        o_ref[...] = jnp.dot(xn_ref[...], w_ref[...], preferred_element_type=F32).astype(o_ref.dtype)

    @pl.when(j < n_conv)
    def _():
        acc = jnp.dot(xn_ref[...], w_ref[...], preferred_element_type=F32)
        jc = jnp.minimum(j, n_conv - 1)
        first = lax.rem(i, tiles_per_seq) == 0
        is_q = j < n_q
        is_qk = j < n_q + n_k
        qscale = jnp.where(is_q, HEAD ** -0.5, 1.0).astype(F32)
        for c in range(tn // LANES):
            sl = slice(c * LANES, (c + 1) * LANES)
            a = acc[:, sl]
            prev = jnp.where(first, 0.0, halo_ref[jc, :, sl])
            head = jnp.concatenate([prev, a[:CONV_HALO]], axis=0)
            y = a * cw_ref[CONV_WIDTH - 1:CONV_WIDTH, sl]
            y0 = a[:CONV_HALO] * cw_ref[CONV_WIDTH - 1:CONV_WIDTH, sl]
            for t in range(CONV_WIDTH - 1):
                sh = CONV_WIDTH - 1 - t
                wt = cw_ref[t:t + 1, sl]
                y = y + pltpu.roll(a, sh, 0) * wt
                y0 = y0 + pltpu.roll(head, sh, 0)[CONV_HALO:] * wt
            y = jnp.concatenate([y0, y[CONV_HALO:]], axis=0)
            y = y * _sigmoid(y)
            inv = lax.rsqrt(jnp.sum(y * y, axis=-1, keepdims=True) + L2_EPS) * qscale
            y = y * jnp.where(is_qk, inv, 1.0)
            o_ref[:, sl] = y.astype(o_ref.dtype)
            halo_ref[jc, :, sl] = a[tm - CONV_HALO:]


def _in_proj(x2, norm_w, w_main, w_ab, conv_w, qk_width, v_width, seq_len, tm, tn):
    T, D = x2.shape
    N = w_main.shape[1]
    rows = min(128, tm)
    assert qk_width % tn == 0 and v_width % tn == 0 and tn % HEAD == 0 and seq_len % tm == 0
    assert CONV_WIDTH - 1 <= CONV_HALO <= tm
    n_q = n_k = qk_width // tn
    n_conv = n_q + n_k + v_width // tn
    return pl.pallas_call(
        functools.partial(_inproj_kernel, rows=rows, n_q=n_q, n_k=n_k, n_conv=n_conv,
                          tiles_per_seq=seq_len // tm),
        grid=(T // tm, N // tn),
        in_specs=[
            pl.BlockSpec((tm, D), lambda i, j: (i, 0)),
            pl.BlockSpec((1, D), lambda i, j: (0, 0)),
            pl.BlockSpec((D, tn), lambda i, j: (0, j)),
            pl.BlockSpec((D, LANES), lambda i, j: (0, 0)),
            pl.BlockSpec((CONV_WIDTH, tn), lambda i, j: (0, jnp.minimum(j, n_conv - 1))),
        ],
        out_specs=[
            pl.BlockSpec((tm, tn), lambda i, j: (i, j)),
            pl.BlockSpec((tm, LANES), lambda i, j: (i, 0)),
        ],
        out_shape=[jax.ShapeDtypeStruct((T, N), BF16), jax.ShapeDtypeStruct((T, LANES), F32)],
        scratch_shapes=[pltpu.VMEM((tm, D), BF16), pltpu.VMEM((n_conv, CONV_HALO, tn), F32)],
        compiler_params=_cparams(("arbitrary", "arbitrary")),
        name="in_proj",
    )(x2, norm_w.reshape(1, D), w_main, w_ab, conv_w.astype(F32))


def _delta_kernel(q_ref, k_ref, v_ref, z_ref, ab_ref, alog_ref, dtb_ref, onw_ref, o_ref,
                  s_ref, t_ref, a_ref, gc_ref, vl_ref, *, G, H, unroll_a):
    St = q_ref.shape[0]
    C = CHUNK
    WIDE = 4 * C
    NW = G // 2
    t_idx = pl.program_id(2)
    j = pl.program_id(1)

    @pl.when(t_idx == 0)
    def _():
        s_ref[...] = jnp.zeros_like(s_ref)

    row = lax.broadcasted_iota(jnp.int32, (C, WIDE), 0)
    lanew = lax.broadcasted_iota(jnp.int32, (C, WIDE), 1)
    colw = lanew % C
    blk = lanew // C
    incl_w = row >= colw
    strict_w = row > colw
    eye_w = jnp.where(row == colw, 1.0, 0.0).astype(F32)
    first_half = (lax.broadcasted_iota(jnp.int32, (C, 2 * C), 1) < C)
    r2 = lax.broadcasted_iota(jnp.int32, (C, C), 0)
    c2 = lax.broadcasted_iota(jnp.int32, (C, C), 1)
    tril = jnp.where(r2 >= c2, 1.0, 0.0).astype(F32)
    lane = lax.broadcasted_iota(jnp.int32, (C, LANES), 1)
    shift = lax.rem(LANES - 2 * G * j, LANES)

    def pair_cols(x, c0):
        a = jnp.broadcast_to(x[:, c0:c0 + 1], (C, 2 * C))
        b = jnp.broadcast_to(x[:, c0 + 1:c0 + 2], (C, 2 * C))
        return jnp.where(first_half, a, b)

    blk_mask = [jnp.where(blk == e, 1.0, 0.0).astype(BF16) for e in range(4)]

    def block_diag(mw):
        return jnp.concatenate([mw * blk_mask[e] for e in range(4)], axis=0)

    def prep(i, carry):
        chunks = [i * unroll_a + u for u in range(unroll_a)]
        r0s = [pl.multiple_of(c * C, C) for c in chunks]
        gcs, valss = [], []
        for r0 in r0s:
            ab = ab_ref[pl.ds(r0, C), :]
            sp_in = ab + dtb_ref[...]
            softplus = jnp.maximum(sp_in, 0.0) + jnp.log(1.0 + jnp.exp(-jnp.abs(sp_in)))
            gval = -jnp.exp(alog_ref[...]) * softplus
            vals = pltpu.roll(jnp.where(lane < H, gval, _sigmoid(ab)), shift, 1)
            gc = jnp.dot(tril, vals, preferred_element_type=F32, precision=HIGHEST)
            gc_ref[pl.ds(r0, C), :] = gc
            vl_ref[pl.ds(r0, C), :] = vals
            gcs.append(gc)
            valss.append(vals)
        qbs = [q_ref[pl.ds(r0, C), h * HEAD:(h + 1) * HEAD] for r0 in r0s for h in range(G)]
        kbs = [k_ref[pl.ds(r0, C), h * HEAD:(h + 1) * HEAD] for r0 in r0s for h in range(G)]
        qkk = [lax.dot_general(jnp.concatenate([qb, kb], axis=0), jnp.concatenate([kb, kb], axis=0),
                               (((1,), (1,)), ((), ())), preferred_element_type=F32)
               for qb, kb in zip(qbs, kbs)]
        ps, ms = [], []
        for u, r0 in enumerate(r0s):
            for w in range(NW):
                top, bot = qkk[u * G + 2 * w], qkk[u * G + 2 * w + 1]
                qk_w = jnp.concatenate([top[:C], bot[:C]], axis=1)
                kk_w = jnp.concatenate([top[C:], bot[C:]], axis=1)
                gcol = jnp.concatenate([pair_cols(gcs[u], 4 * w), pair_cols(gcs[u], 4 * w + 2)], axis=1)
                bcol = jnp.concatenate([pair_cols(valss[u], H + 4 * w),
                                        pair_cols(valss[u], H + 4 * w + 2)], axis=1)
                grow = jnp.sum(gcol * eye_w, axis=0, keepdims=True)
                dec = jnp.exp(jnp.where(incl_w, gcol - grow, -jnp.inf))
                lmat = jnp.where(strict_w, kk_w * dec * bcol, 0.0)
                a_ref[pl.ds(r0, C), w * WIDE:(w + 1) * WIDE] = (qk_w * dec).astype(BF16)
                ps.append(eye_w - lmat)
                ms.append(lmat)
        ms = [_bdot(m, block_diag(m.astype(BF16))) for m in ms]
        for _ in range(int(math.log2(C)) - 2):
            outs = [jnp.dot(jnp.concatenate([p, m], axis=0).astype(BF16), block_diag(m.astype(BF16)),
                            preferred_element_type=F32) for p, m in zip(ps, ms)]
            ps = [p + o[:C] for p, o in zip(ps, outs)]
            ms = [o[C:] for o in outs]
        outs = [_bdot(p, block_diag(m.astype(BF16))) for p, m in zip(ps, ms)]
        for u, r0 in enumerate(r0s):
            for w in range(NW):
                t_ref[pl.ds(r0, C), w * WIDE:(w + 1) * WIDE] = (ps[u * NW + w] + outs[u * NW + w]).astype(BF16)
        return carry

    lax.fori_loop(0, St // (C * unroll_a), prep, 0)

    def scan_chunk(c, carry):
        r0 = pl.multiple_of(c * C, C)
        gc = gc_ref[pl.ds(r0, C), :]
        vals = vl_ref[pl.ds(r0, C), :]
        kbs = [k_ref[pl.ds(r0, C), h * HEAD:(h + 1) * HEAD] for h in range(G)]
        ks = [jnp.dot(jnp.concatenate([q_ref[pl.ds(r0, C), h * HEAD:(h + 1) * HEAD], kbs[h]], axis=0),
                      jnp.concatenate([s_ref[2 * h], s_ref[2 * h + 1]], axis=1).astype(BF16),
                      preferred_element_type=F32) for h in range(G)]
        egs, gcols, rhss = [], [], []
        for hv in range(2 * G):
            h, e = divmod(hv, 2)
            gcol = jnp.broadcast_to(gc[:, hv:hv + 1], (C, LANES))
            bcol = jnp.broadcast_to(vals[:, H + hv:H + hv + 1], (C, LANES))
            eg = jnp.exp(gcol)
            v = v_ref[pl.ds(r0, C), hv * HEAD:(hv + 1) * HEAD].astype(F32)
            rhss.append((bcol * (v - eg * ks[h][C:, e * HEAD:(e + 1) * HEAD])).astype(BF16))
            egs.append(eg)
            gcols.append(gcol)
        vnews = [jnp.dot(t_ref[pl.ds(r0, C), hv * C:(hv + 1) * C], rhss[hv],
                         preferred_element_type=F32) for hv in range(2 * G)]
        intra = [jnp.dot(a_ref[pl.ds(r0, C), hv * C:(hv + 1) * C], vnews[hv].astype(BF16),
                         preferred_element_type=F32) for hv in range(2 * G)]
        vnd = [(jnp.exp(gcols[hv][C - 1:C, :] - gcols[hv]) * vnews[hv]).astype(BF16)
               for hv in range(2 * G)]
        upd = [lax.dot_general(kbs[h], jnp.concatenate([vnd[2 * h], vnd[2 * h + 1]], axis=1),
                               (((0,), (0,)), ((), ())), preferred_element_type=F32)
               for h in range(G)]
        for hv in range(2 * G):
            h, e = divmod(hv, 2)
            s_ref[hv] = (s_ref[hv] * jnp.exp(gcols[hv][C - 1:C, :])
                         + upd[h][:, e * HEAD:(e + 1) * HEAD])
            o = egs[hv] * ks[h][:C, e * HEAD:(e + 1) * HEAD] + intra[hv]
            zz = z_ref[pl.ds(r0, C), hv * HEAD:(hv + 1) * HEAD].astype(F32)
            on = o * lax.rsqrt(jnp.mean(o * o, axis=-1, keepdims=True) + RMS_EPS)
            on = on * onw_ref[...] * (zz * _sigmoid(zz))
            o_ref[pl.ds(r0, C), hv * HEAD:(hv + 1) * HEAD] = on.astype(o_ref.dtype)
        return carry

    lax.fori_loop(0, St // C, scan_chunk, 0)


def _delta(proj3, ab3, a_log, dt_bias, onorm_w, qk_width, v_width, G, St):
    B, S, _ = proj3.shape
    H = a_log.shape[0]
    n_groups = qk_width // (HEAD * G)
    qw, vw = HEAD * G, 2 * HEAD * G
    k_blk0 = qk_width // qw
    v_blk0 = (2 * qk_width) // vw
    z_blk0 = (2 * qk_width + v_width) // vw
    assert (2 * qk_width) % vw == 0 and (2 * qk_width + v_width) % vw == 0
    assert 2 * H <= LANES and S % CHUNK == 0
    assert G % 2 == 0 and S % St == 0 and St % CHUNK == 0
    pad = lambda p: jnp.zeros((1, LANES), F32).at[0, :H].set(p.astype(F32))
    return pl.pallas_call(
        functools.partial(_delta_kernel, G=G, H=H, unroll_a=2 if (St // CHUNK) % 2 == 0 else 1),
        grid=(B, n_groups, S // St),
        in_specs=[
            pl.BlockSpec((None, St, qw), lambda b, j, t: (b, t, j)),
            pl.BlockSpec((None, St, qw), lambda b, j, t: (b, t, k_blk0 + j)),
            pl.BlockSpec((None, St, vw), lambda b, j, t: (b, t, v_blk0 + j)),
            pl.BlockSpec((None, St, vw), lambda b, j, t: (b, t, z_blk0 + j)),
            pl.BlockSpec((None, St, LANES), lambda b, j, t: (b, t, 0)),
            pl.BlockSpec((1, LANES), lambda b, j, t: (0, 0)),
            pl.BlockSpec((1, LANES), lambda b, j, t: (0, 0)),
            pl.BlockSpec((1, HEAD), lambda b, j, t: (0, 0)),
        ],
        out_specs=pl.BlockSpec((None, St, vw), lambda b, j, t: (b, t, j)),
        out_shape=jax.ShapeDtypeStruct((B, S, v_width), BF16),
        scratch_shapes=[
            pltpu.VMEM((2 * G, HEAD, HEAD), F32),
            pltpu.VMEM((St, qw), BF16),
            pltpu.VMEM((St, qw), BF16),
            pltpu.VMEM((St, LANES), F32),
            pltpu.VMEM((St, LANES), F32),
        ],
        compiler_params=_cparams(("arbitrary", "arbitrary", "arbitrary")),
        name="delta",
    )(proj3, proj3, proj3, proj3, ab3, pad(a_log), pad(dt_bias), onorm_w.reshape(1, HEAD).astype(F32))


POOL_HALO = 16


def _pool_kernel(p_ref, w_ref, sc_ref, o_ref, pad_ref, *, rt):
    S = p_ref.shape[0]
    g = pl.program_id(1)
    pad_ref[0:POOL_HALO, :] = jnp.zeros((POOL_HALO, pad_ref.shape[1]), F32)
    pad_ref[POOL_HALO:, :] = p_ref[...].astype(F32)
    win = jnp.left_shift(2, g).astype(F32)

    def body(r, c):
        r0 = pl.multiple_of(r * rt, rt)
        x = pad_ref[pl.ds(r0, rt + POOL_HALO), :]
        y1 = x + pltpu.roll(x, 1, 0)
        y2 = y1 + pltpu.roll(y1, 2, 0)
        y3 = y2 + pltpu.roll(y2, 4, 0)
        y4 = y3 + pltpu.roll(y3, 8, 0)
        ysel = jnp.where(g == 0, y1, jnp.where(g == 1, y2, jnp.where(g == 2, y3, y4)))
        pos = (r0 + 1 + lax.broadcasted_iota(jnp.int32, (rt, 1), 0)).astype(F32)
        cnt = jnp.minimum(pos, win)
        mixed = ysel[POOL_HALO:] / cnt - x[POOL_HALO:]
        out = _bdot(mixed, w_ref[...]) * sc_ref[...]
        o_ref[pl.ds(r0, rt), :] = out.astype(o_ref.dtype)
        return c

    lax.fori_loop(0, S // rt, body, 0)


def _pool(proj3, w_pool, pool_scale, p_off, d_model):
    B, S, _ = proj3.shape
    ng, cg, _ = w_pool.shape
    assert POOL_WINDOWS == tuple(2 << i for i in range(ng)) and max(POOL_WINDOWS) <= POOL_HALO
    assert p_off % cg == 0
    blk0 = p_off // cg
    rt = min(256, S)
    return pl.pallas_call(
        functools.partial(_pool_kernel, rt=rt),
        grid=(B, ng),
        in_specs=[
            pl.BlockSpec((None, S, cg), lambda b, g: (b, 0, blk0 + g)),
            pl.BlockSpec((None, cg, cg), lambda b, g: (g, 0, 0)),
            pl.BlockSpec((1, cg), lambda b, g: (0, g)),
        ],
        out_specs=pl.BlockSpec((None, S, cg), lambda b, g: (b, 0, g)),
        out_shape=jax.ShapeDtypeStruct((B, S, d_model), BF16),
        scratch_shapes=[pltpu.VMEM((S + POOL_HALO, cg), F32)],
        compiler_params=_cparams(("arbitrary", "arbitrary")),
        name="pool",
    )(proj3, w_pool.astype(BF16), pool_scale.reshape(1, -1).astype(F32))


def _branch_kernel(oa_ref, op_ref, wa_ref, wp_ref, ga_ref, gp_ref, bga_ref, bgp_ref, h_ref):
    a = jnp.dot(oa_ref[...], wa_ref[...], preferred_element_type=F32)
    p = jnp.dot(op_ref[...], wp_ref[...], preferred_element_type=F32)
    ga = _sigmoid(ga_ref[...].astype(F32) + bga_ref[...])
    gp = _sigmoid(gp_ref[...].astype(F32) + bgp_ref[...])
    h_ref[...] = (ga * a + gp * p).astype(h_ref.dtype)


def _branch(o_a, o_p, proj, w_a, w_p, b_gate, ga_off, gp_off, tm, tn):
    T, V = o_a.shape
    D = w_a.shape[1]
    assert ga_off % tn == 0 and gp_off % tn == 0
    ga0, gp0 = ga_off // tn, gp_off // tn
    nd = D // tn
    bg = b_gate.reshape(1, 2 * D).astype(F32)
    return pl.pallas_call(
        _branch_kernel,
        grid=(T // tm, nd),
        in_specs=[
            pl.BlockSpec((tm, V), lambda i, j: (i, 0)),
            pl.BlockSpec((tm, D), lambda i, j: (i, 0)),
            pl.BlockSpec((V, tn), lambda i, j: (0, j)),
            pl.BlockSpec((D, tn), lambda i, j: (0, j)),
            pl.BlockSpec((tm, tn), lambda i, j: (i, ga0 + j)),
            pl.BlockSpec((tm, tn), lambda i, j: (i, gp0 + j)),
            pl.BlockSpec((1, tn), lambda i, j: (0, j)),
            pl.BlockSpec((1, tn), lambda i, j: (0, nd + j)),
        ],
        out_specs=pl.BlockSpec((tm, tn), lambda i, j: (i, j)),
        out_shape=jax.ShapeDtypeStruct((T, D), BF16),
        compiler_params=_cparams(("arbitrary", "arbitrary")),
        name="branch",
    )(o_a, o_p, w_a, w_p, proj, proj, bg, bg)


def _pack_pair(lo, hi):
    lo = lax.bitcast_convert_type(lo.astype(BF16).astype(F32), jnp.uint32)
    hi = lax.bitcast_convert_type(hi.astype(BF16).astype(F32), jnp.uint32)
    return jnp.right_shift(lo, jnp.uint32(16)) | (hi & jnp.uint32(0xFFFF0000))


def _pack_halves(x):
    half = x.shape[1] // 2
    return _pack_pair(x[:, :half], x[:, half:])


def _unpack_halves(u):
    lo = lax.bitcast_convert_type(jnp.left_shift(u, jnp.uint32(16)), F32)
    hi = lax.bitcast_convert_type(u & jnp.uint32(0xFFFF0000), F32)
    return lo, hi


def _split_bf16(x):
    hi = x.astype(BF16)
    return hi, (x - hi.astype(F32)).astype(BF16)


def _post_kernel(x_ref, h_ref, wo_ref, nw_ref, rwh_ref, rwl_ref, rb_ref,
                 x1_ref, xp_ref, idx_ref, gate_ref, *, n_experts):
    x1 = x_ref[...] + jnp.dot(h_ref[...], wo_ref[...], preferred_element_type=F32)
    x1_ref[...] = x1
    xn = x1 * lax.rsqrt(jnp.mean(x1 * x1, axis=-1, keepdims=True) + RMS_EPS) * nw_ref[...]
    xp_ref[...] = _pack_halves(xn)
    xh, xl = _split_bf16(xn)
    logits = (jnp.dot(xh, rwh_ref[...], preferred_element_type=F32)
              + jnp.dot(xh, rwl_ref[...], preferred_element_type=F32)
              + jnp.dot(xl, rwh_ref[...], preferred_element_type=F32) + rb_ref[...])
    lane = lax.broadcasted_iota(jnp.int32, logits.shape, 1)
    work = jnp.where(lane < n_experts, logits, -jnp.inf)
    vals, idxs = [], []
    for _ in range(TOP_K):
        m = jnp.max(work, axis=-1, keepdims=True)
        idx = jnp.min(jnp.where(work == m, lane, LANES), axis=-1, keepdims=True)
        vals.append(m)
        idxs.append(idx)
        work = jnp.where(lane == idx, -jnp.inf, work)
    ex = [jnp.exp(v - vals[0]) for v in vals]
    den = ex[0]
    for e in ex[1:]:
        den = den + e
    idx_out = jnp.zeros(logits.shape, jnp.int32)
    gate_out = jnp.zeros(logits.shape, F32)
    for k in range(TOP_K):
        idx_out = jnp.where(lane == k, idxs[k], idx_out)
        gate_out = jnp.where(lane == k, ex[k] / den, gate_out)
    idx_ref[...] = idx_out
    gate_ref[...] = gate_out


def _post(x2, h, w_out, norm2_w, router_w, router_b, tm):
    T, D = x2.shape
    E = router_w.shape[1]
    rw = jnp.zeros((D, LANES), F32).at[:, :E].set(router_w.astype(F32))
    rwh, rwl = _split_bf16(rw)
    rb = jnp.zeros((1, LANES), F32).at[0, :E].set(router_b.astype(F32))
    return pl.pallas_call(
        functools.partial(_post_kernel, n_experts=E),
        grid=(T // tm,),
        in_specs=[
            pl.BlockSpec((tm, D), lambda i: (i, 0)),
            pl.BlockSpec((tm, D), lambda i: (i, 0)),
            pl.BlockSpec((D, D), lambda i: (0, 0)),
            pl.BlockSpec((1, D), lambda i: (0, 0)),
            pl.BlockSpec((D, LANES), lambda i: (0, 0)),
            pl.BlockSpec((D, LANES), lambda i: (0, 0)),
            pl.BlockSpec((1, LANES), lambda i: (0, 0)),
        ],
        out_specs=[
            pl.BlockSpec((tm, D), lambda i: (i, 0)),
            pl.BlockSpec((tm, D // 2), lambda i: (i, 0)),
            pl.BlockSpec((tm, LANES), lambda i: (i, 0)),
            pl.BlockSpec((tm, LANES), lambda i: (i, 0)),
        ],
        out_shape=[
            jax.ShapeDtypeStruct((T, D), F32),
            jax.ShapeDtypeStruct((T, D // 2), jnp.uint32),
            jax.ShapeDtypeStruct((T, LANES), jnp.int32),
            jax.ShapeDtypeStruct((T, LANES), F32),
        ],
        compiler_params=_cparams(("arbitrary",)),
        name="post",
    )(x2, h, w_out, norm2_w.reshape(1, D).astype(F32), rwh, rwl, rb)


ISSUE_UNROLL = 8


def _dispatch_kernel(dest_ref, xp_ref, zero_hbm, o_hbm, sem, *, tm):
    del zero_hbm

    def body(g, c):
        for u in range(ISSUE_UNROLL // TOP_K):
            r = g * (ISSUE_UNROLL // TOP_K) + u
            for k in range(TOP_K):
                pltpu.make_async_copy(xp_ref.at[pl.ds(r, 1)],
                                      o_hbm.at[pl.ds(dest_ref[0, 0, r * TOP_K + k], 1)], sem).start()
        return c
    lax.fori_loop(0, tm // (ISSUE_UNROLL // TOP_K), body, 0)
    for _ in range(TOP_K):
        pltpu.make_async_copy(xp_ref, o_hbm.at[pl.ds(0, tm)], sem).wait()


def _dispatch_rows(xp, dest, n_rows, tm):
    T, W = xp.shape
    nt = T // tm
    return pl.pallas_call(
        functools.partial(_dispatch_kernel, tm=tm),
        grid=(nt,),
        in_specs=[
            pl.BlockSpec((1, 1, tm * TOP_K), lambda i: (i, 0, 0), memory_space=pltpu.SMEM),
            pl.BlockSpec((tm, W), lambda i: (i, 0)),
            pl.BlockSpec(memory_space=pl.ANY),
        ],
        out_specs=pl.BlockSpec(memory_space=pl.ANY),
        out_shape=jax.ShapeDtypeStruct((n_rows, W), xp.dtype),
        scratch_shapes=[pltpu.SemaphoreType.DMA(())],
        input_output_aliases={2: 0},
        compiler_params=_cparams(("arbitrary",)),
        name="dispatch",
    )(dest.reshape(nt, 1, tm * TOP_K), xp, jnp.zeros((n_rows, W), xp.dtype))


def _new_expert(be_ref, m):
    return jnp.logical_or(m == 0, be_ref[m] != be_ref[jnp.maximum(m - 1, 0)])


CAST_ROWS = 256


def _gmm1_kernel(be_ref, bv_ref, x_ref, w_ref, b_ref, o_ref, wb_ref):
    m = pl.program_id(1)
    valid = bv_ref[m] == 1

    @pl.when(jnp.logical_and(valid, _new_expert(be_ref, m)))
    def _():
        def body(r, c):
            r0 = pl.multiple_of(r * CAST_ROWS, CAST_ROWS)
            wb_ref[pl.ds(r0, CAST_ROWS), :] = w_ref[pl.ds(r0, CAST_ROWS), :].astype(BF16)
            return c
        lax.fori_loop(0, w_ref.shape[0] // CAST_ROWS, body, 0)

    @pl.when(valid)
    def _():
        even = lax.broadcasted_iota(jnp.int32, (x_ref.shape[0], LANES), 1) % 2 == 0
        lo, hi = _unpack_halves(x_ref[...])
        x = jnp.concatenate([lo.astype(BF16), hi.astype(BF16)], axis=1)

        def act_block(hblk):
            x_glu = jnp.minimum(hblk, SWIGLU_LIMIT)
            x_lin = jnp.clip(hblk, -SWIGLU_LIMIT, SWIGLU_LIMIT) + 1.0
            return x_glu * _sigmoid(SWIGLU_ALPHA * x_glu) * pltpu.roll(x_lin, LANES - 1, 1)

        for i in range(o_ref.shape[1] // LANES):
            sl = slice(2 * i * LANES, 2 * (i + 1) * LANES)
            hs = jnp.dot(x, wb_ref[:, sl], preferred_element_type=F32) + b_ref[:, sl]
            packed = jnp.where(even, act_block(hs[:, :LANES]), pltpu.roll(act_block(hs[:, LANES:]), 1, 1))
            o_ref[:, i * LANES:(i + 1) * LANES] = packed.astype(o_ref.dtype)

    @pl.when(bv_ref[m] == 0)
    def _():
        o_ref[...] = jnp.zeros_like(o_ref)


def _gmm1(block_expert, block_valid, x_buf, w1, b1, bm, tn):
    R, half = x_buf.shape
    E, D, F2 = w1.shape
    nb = R // bm
    assert D == 2 * half and D % CAST_ROWS == 0 and tn % (2 * LANES) == 0
    grid_spec = pltpu.PrefetchScalarGridSpec(
        num_scalar_prefetch=2,
        grid=(F2 // tn, nb),
        in_specs=[
            pl.BlockSpec((bm, half), lambda n, m, be, bv: (m, 0)),
            pl.BlockSpec((None, D, tn), lambda n, m, be, bv: (be[m], 0, n)),
            pl.BlockSpec((None, 1, tn), lambda n, m, be, bv: (be[m], 0, n)),
        ],
        out_specs=pl.BlockSpec((bm, tn // 2), lambda n, m, be, bv: (m, n)),
        scratch_shapes=[pltpu.VMEM((D, tn), BF16)],
    )
    return pl.pallas_call(
        _gmm1_kernel,
        grid_spec=grid_spec,
        out_shape=jax.ShapeDtypeStruct((R, F2 // 2), BF16),
        compiler_params=_cparams(("arbitrary", "arbitrary")),
        name="gmm1",
    )(block_expert, block_valid, x_buf, w1, b1)


def _gmm2_kernel(be_ref, bv_ref, a_ref, w_ref, b_ref, o_ref, wb_ref):
    m = pl.program_id(1)
    valid = bv_ref[m] == 1

    @pl.when(jnp.logical_and(valid, _new_expert(be_ref, m)))
    def _():
        half = LANES // 2

        def body(i, c):
            r0 = pl.multiple_of(i * LANES, LANES)
            words = _pack_pair(w_ref[pl.ds(r0, half), :], w_ref[pl.ds(r0 + half, half), :])
            wb_ref[pl.ds(r0, LANES), :] = pltpu.bitcast(words, BF16)
            return c
        lax.fori_loop(0, w_ref.shape[0] // LANES, body, 0)

    @pl.when(valid)
    def _():
        y = jnp.dot(a_ref[...], wb_ref[...], preferred_element_type=F32) + b_ref[...]
        o_ref[...] = _pack_halves(y)

    @pl.when(bv_ref[m] == 0)
    def _():
        o_ref[...] = jnp.zeros_like(o_ref)


def _gmm2(block_expert, block_valid, act, w2, b2, bm, tn):
    R, F = act.shape
    E, _, D = w2.shape
    nb = R // bm
    assert tn % (2 * LANES) == 0
    grid_spec = pltpu.PrefetchScalarGridSpec(
        num_scalar_prefetch=2,
        grid=(D // tn, nb),
        in_specs=[
            pl.BlockSpec((bm, F), lambda n, m, be, bv: (m, 0)),
            pl.BlockSpec((None, F, tn), lambda n, m, be, bv: (be[m], 0, n)),
            pl.BlockSpec((None, 1, tn), lambda n, m, be, bv: (be[m], 0, n)),
        ],
        out_specs=pl.BlockSpec((bm, tn // 2), lambda n, m, be, bv: (m, n)),
        scratch_shapes=[pltpu.VMEM((F, tn), BF16)],
    )
    return pl.pallas_call(
        _gmm2_kernel,
        grid_spec=grid_spec,
        out_shape=jax.ShapeDtypeStruct((R, D // 2), jnp.uint32),
        compiler_params=_cparams(("arbitrary", "arbitrary")),
        name="gmm2",
    )(block_expert, block_valid, act, w2, b2)


def _combine_kernel(dcur_ref, dnext_ref, gate_ref, x1_ref, nw_ref, y_hbm, o_ref, ybuf, sem, *, tm, y_tile):
    i = pl.program_id(0)
    n = pl.num_programs(0)
    slot = lax.rem(i, 2)

    def copy(d_ref, s, r, k):
        return pltpu.make_async_copy(y_hbm.at[pl.ds(d_ref[0, 0, r * TOP_K + k], 1)],
                                     ybuf.at[s, k, pl.ds(r, 1)], sem.at[s])

    def issue(d_ref, s):
        def body(g, c):
            for u in range(ISSUE_UNROLL // TOP_K):
                for k in range(TOP_K):
                    copy(d_ref, s, g * (ISSUE_UNROLL // TOP_K) + u, k).start()
            return c
        lax.fori_loop(0, tm // (ISSUE_UNROLL // TOP_K), body, 0)

    @pl.when(i == 0)
    def _():
        issue(dcur_ref, 0)

    @pl.when(i + 1 < n)
    def _():
        issue(dnext_ref, 1 - slot)

    for k in range(TOP_K):
        pltpu.make_async_copy(y_hbm.at[pl.ds(0, tm)], ybuf.at[slot, k], sem.at[slot]).wait()

    lane = lax.broadcasted_iota(jnp.int32, (tm, LANES), 1)
    gates = gate_ref[...]
    acc = x1_ref[...]
    hw = y_tile // 2
    for k in range(TOP_K):
        gk = jnp.sum(jnp.where(lane == k, gates, 0.0), axis=-1, keepdims=True)
        parts = []
        for jt in range(ybuf.shape[-1] // hw):
            lo, hi = _unpack_halves(ybuf[slot, k, :, jt * hw:(jt + 1) * hw])
            parts += [lo, hi]
        acc = acc + gk * jnp.concatenate(parts, axis=1)
    out = acc * lax.rsqrt(jnp.mean(acc * acc, axis=-1, keepdims=True) + RMS_EPS) * nw_ref[...]
    o_ref[...] = out


def _combine(dest, gates, x1, norm_f_w, y_buf, tm, y_tile):
    T, D = x1.shape
    nt = T // tm
    d3 = dest.reshape(nt, 1, tm * TOP_K)
    return pl.pallas_call(
        functools.partial(_combine_kernel, tm=tm, y_tile=y_tile),
        grid=(nt,),
        in_specs=[
            pl.BlockSpec((1, 1, tm * TOP_K), lambda i: (i, 0, 0), memory_space=pltpu.SMEM),
            pl.BlockSpec((1, 1, tm * TOP_K), lambda i: (jnp.minimum(i + 1, nt - 1), 0, 0),
                         memory_space=pltpu.SMEM),
            pl.BlockSpec((tm, LANES), lambda i: (i, 0)),
            pl.BlockSpec((tm, D), lambda i: (i, 0)),
            pl.BlockSpec((1, D), lambda i: (0, 0)),
            pl.BlockSpec(memory_space=pl.ANY),
        ],
        out_specs=pl.BlockSpec((tm, D), lambda i: (i, 0)),
        out_shape=jax.ShapeDtypeStruct((T, D), F32),
        scratch_shapes=[pltpu.VMEM((2, TOP_K, tm, D // 2), jnp.uint32), pltpu.SemaphoreType.DMA((2,))],
        compiler_params=_cparams(("arbitrary",)),
        name="combine",
    )(d3, d3, gates, x1, norm_f_w.reshape(1, D).astype(F32), y_buf)


def _routing(top_idx, n_experts, bm):
    T = top_idx.shape[0]
    A = T * TOP_K
    flat_e = top_idx.reshape(A)
    onehot = (flat_e[:, None] == jnp.arange(n_experts, dtype=jnp.int32)[None, :]).astype(jnp.int32)
    csum = jnp.cumsum(onehot, axis=0)
    rank = jnp.sum((csum - onehot) * onehot, axis=1)
    counts = csum[-1]
    padded = (counts + bm - 1) // bm * bm
    pend = jnp.cumsum(padded)
    pstart = pend - padded
    dest = jnp.sum(onehot * pstart[None, :], axis=1) + rank
    R = (A + bm - 1) // bm * bm + n_experts * bm
    nb = R // bm
    block_start = jnp.arange(nb, dtype=jnp.int32) * bm
    block_expert = jnp.minimum(jnp.sum((block_start[:, None] >= pend[None, :]).astype(jnp.int32), axis=1),
                               n_experts - 1)
    block_valid = (block_start < pend[-1]).astype(jnp.int32)
    return dest.astype(jnp.int32), R, block_expert, block_valid


def _pick(pref, n):
    t = min(pref, n)
    while n % t:
        t //= 2
    return t


def kernel(x, norm1_w, w_in, b_gate, conv_w, A_log, dt_bias, onorm_w, w_branch_a, w_pool,
           pool_scale, w_branch_p, w_out, norm2_w, router_w, router_b, w1, b1, w2, b2, norm_f_w):
    B, S, D = x.shape
    T = B * S
    depth = norm1_w.shape[0]
    x2 = x.reshape(T, D)
    for l in range(depth):
        conv_ch = conv_w.shape[-1]
        v_width = w_branch_a.shape[1]
        qk_width = (conv_ch - v_width) // 2
        n_v = A_log.shape[-1]
        qkvz = conv_ch + v_width
        ab_lo, ab_hi = qkvz, qkvz + 2 * n_v
        E = router_w.shape[-1]

        wl = w_in[l]
        w_main = jnp.concatenate([wl[:, :qkvz], wl[:, ab_hi:]], axis=1).astype(BF16)
        w_ab = jnp.zeros((D, LANES), BF16).at[:, :2 * n_v].set(wl[:, ab_lo:ab_hi].astype(BF16))
        p_off, ga_off, gp_off = qkvz, qkvz + D, qkvz + 2 * D

        tn_in = _pick(1024, math.gcd(qk_width, v_width, w_main.shape[1]))
        tm_in = _pick(1024, S // 2 if S >= 2 * CONV_HALO else S)
        proj, ab = _in_proj(x2, norm1_w[l], w_main, w_ab, conv_w[l], qk_width, v_width, S, tm_in, tn_in)
        proj3 = proj.reshape(B, S, -1)
        G = 8 if (qk_width // HEAD) % 8 == 0 else 2
        o_a = _delta(proj3, ab.reshape(B, S, LANES), A_log[l], dt_bias[l], onorm_w[l],
                     qk_width, v_width, G, _pick(512, S))
        o_p = _pool(proj3, w_pool[l], pool_scale[l], p_off, D)
        h = _branch(o_a.reshape(T, v_width), o_p.reshape(T, D), proj,
                    w_branch_a[l].astype(BF16), w_branch_p[l].astype(BF16), b_gate[l],
                    ga_off, gp_off, _pick(1024, T), _pick(512, D))
        x1, xp, idx_pad, gate_pad = _post(x2, h, w_out[l].astype(BF16), norm2_w[l],
                                          router_w[l], router_b[l], _pick(512, T))

        bm = 512
        y_tile = _pick(1024, D)
        dest, n_rows, block_expert, block_valid = _routing(idx_pad[:, :TOP_K], E, bm)
        x_buf = _dispatch_rows(xp, dest, n_rows, _pick(512, T))
        act = _gmm1(block_expert, block_valid, x_buf, w1[l].astype(F32),
                    b1[l][:, None, :].astype(F32), bm, _pick(2048, w1.shape[-1]))
        y_buf = _gmm2(block_expert, block_valid, act, w2[l].astype(F32),
                      b2[l][:, None, :].astype(F32), bm, y_tile)
        is_last = l == depth - 1
        assert is_last, "final norm is fused into the last layer's combine"
        x2 = _combine(dest, gate_pad, x1, norm_f_w, y_buf, _pick(128, T), y_tile)
    return x2.reshape(B, S, D)
```

```python
import functools
import math

import jax
import jax.numpy as jnp
from jax import lax
from jax.experimental import pallas as pl
from jax.experimental.pallas import tpu as pltpu

F32 = jnp.float32
BF16 = jnp.bfloat16

HEAD = 128
CHUNK = 64
CONV_WIDTH = 4
POOL_WINDOWS = (2, 4, 8, 16)
TOP_K = 4
SWIGLU_ALPHA = 1.702
SWIGLU_LIMIT = 7.0
RMS_EPS = 1e-6
L2_EPS = 1e-6
LANES = 128
HIGHEST = lax.Precision.HIGHEST
VMEM_LIMIT = 56 * 1024 * 1024


def _cparams(sem):
    return pltpu.CompilerParams(dimension_semantics=sem, vmem_limit_bytes=VMEM_LIMIT)


def _sigmoid(x):
    return 1.0 / (1.0 + jnp.exp(-x))


def _bdot(a, b):
    return jnp.dot(a.astype(BF16), b.astype(BF16), preferred_element_type=F32)


CONV_HALO = 8


def _inproj_kernel(x_ref, nw_ref, w_ref, wab_ref, cw_ref, o_ref, ab_ref, xn_ref, halo_ref, *,
                   rows, n_q, n_k, n_conv, tiles_per_seq):
    i = pl.program_id(0)
    j = pl.program_id(1)
    tm, tn = o_ref.shape

    @pl.when(j == 0)
    def _():
        def body(r, c):
            r0 = pl.multiple_of(r * rows, rows)
            x = x_ref[pl.ds(r0, rows), :]
            inv = lax.rsqrt(jnp.mean(x * x, axis=-1, keepdims=True) + RMS_EPS)
            xn_ref[pl.ds(r0, rows), :] = (x * inv * nw_ref[...]).astype(BF16)
            return c
        lax.fori_loop(0, tm // rows, body, 0)
        ab_ref[...] = jnp.dot(xn_ref[...], wab_ref[...], preferred_element_type=F32)

    @pl.when(j >= n_conv)
    def _():
        o_ref[...] = jnp.dot(xn_ref[...], w_ref[...], preferred_element_type=F32).astype(o_ref.dtype)

    @pl.when(j < n_conv)
    def _():
        acc = jnp.dot(xn_ref[...], w_ref[...], preferred_element_type=F32)
        jc = jnp.minimum(j, n_conv - 1)
        first = lax.rem(i, tiles_per_seq) == 0
        is_q = j < n_q
        is_qk = j < n_q + n_k
        qscale = jnp.where(is_q, HEAD ** -0.5, 1.0).astype(F32)
        for c in range(tn // LANES):
            sl = slice(c * LANES, (c + 1) * LANES)
            a = acc[:, sl]
            prev = jnp.where(first, 0.0, halo_ref[jc, :, sl])
            head = jnp.concatenate([prev, a[:CONV_HALO]], axis=0)
            y = a * cw_ref[CONV_WIDTH - 1:CONV_WIDTH, sl]
            y0 = a[:CONV_HALO] * cw_ref[CONV_WIDTH - 1:CONV_WIDTH, sl]
            for t in range(CONV_WIDTH - 1):
                sh = CONV_WIDTH - 1 - t
                wt = cw_ref[t:t + 1, sl]
                y = y + pltpu.roll(a, sh, 0) * wt
                y0 = y0 + pltpu.roll(head, sh, 0)[CONV_HALO:] * wt
            y = jnp.concatenate([y0, y[CONV_HALO:]], axis=0)
            y = y * _sigmoid(y)
            inv = lax.rsqrt(jnp.sum(y * y, axis=-1, keepdims=True) + L2_EPS) * qscale
            y = y * jnp.where(is_qk, inv, 1.0)
            o_ref[:, sl] = y.astype(o_ref.dtype)
            halo_ref[jc, :, sl] = a[tm - CONV_HALO:]


def _in_proj(x2, norm_w, w_main, w_ab, conv_w, qk_width, v_width, seq_len, tm, tn):
    T, D = x2.shape
    N = w_main.shape[1]
    rows = min(128, tm)
    assert qk_width % tn == 0 and v_width % tn == 0 and tn % HEAD == 0 and seq_len % tm == 0
    assert CONV_WIDTH - 1 <= CONV_HALO <= tm
    n_q = n_k = qk_width // tn
    n_conv = n_q + n_k + v_width // tn
    return pl.pallas_call(
        functools.partial(_inproj_kernel, rows=rows, n_q=n_q, n_k=n_k, n_conv=n_conv,
                          tiles_per_seq=seq_len // tm),
        grid=(T // tm, N // tn),
        in_specs=[
            pl.BlockSpec((tm, D), lambda i, j: (i, 0)),
            pl.BlockSpec((1, D), lambda i, j: (0, 0)),
            pl.BlockSpec((D, tn), lambda i, j: (0, j)),
            pl.BlockSpec((D, LANES), lambda i, j: (0, 0)),
            pl.BlockSpec((CONV_WIDTH, tn), lambda i, j: (0, jnp.minimum(j, n_conv - 1))),
        ],
        out_specs=[
            pl.BlockSpec((tm, tn), lambda i, j: (i, j)),
            pl.BlockSpec((tm, LANES), lambda i, j: (i, 0)),
        ],
        out_shape=[jax.ShapeDtypeStruct((T, N), BF16), jax.ShapeDtypeStruct((T, LANES), F32)],
        scratch_shapes=[pltpu.VMEM((tm, D), BF16), pltpu.VMEM((n_conv, CONV_HALO, tn), F32)],
        compiler_params=_cparams(("arbitrary", "arbitrary")),
        name="in_proj",
    )(x2, norm_w.reshape(1, D), w_main, w_ab, conv_w.astype(F32))


def _delta_kernel(q_ref, k_ref, v_ref, z_ref, ab_ref, alog_ref, dtb_ref, onw_ref, o_ref,
                  s_ref, t_ref, a_ref, gc_ref, vl_ref, *, G, H, unroll_a):
    St = q_ref.shape[0]
    C = CHUNK
    WIDE = 4 * C
    NW = G // 2
    t_idx = pl.program_id(2)
    j = pl.program_id(1)

    @pl.when(t_idx == 0)
    def _():
        s_ref[...] = jnp.zeros_like(s_ref)

    row = lax.broadcasted_iota(jnp.int32, (C, WIDE), 0)
    lanew = lax.broadcasted_iota(jnp.int32, (C, WIDE), 1)
    colw = lanew % C
    blk = lanew // C
    incl_w = row >= colw
    strict_w = row > colw
    eye_w = jnp.where(row == colw, 1.0, 0.0).astype(F32)
    first_half = (lax.broadcasted_iota(jnp.int32, (C, 2 * C), 1) < C)
    r2 = lax.broadcasted_iota(jnp.int32, (C, C), 0)
    c2 = lax.broadcasted_iota(jnp.int32, (C, C), 1)
    tril = jnp.where(r2 >= c2, 1.0, 0.0).astype(F32)
    lane = lax.broadcasted_iota(jnp.int32, (C, LANES), 1)
    shift = lax.rem(LANES - 2 * G * j, LANES)

    def pair_cols(x, c0):
        a = jnp.broadcast_to(x[:, c0:c0 + 1], (C, 2 * C))
        b = jnp.broadcast_to(x[:, c0 + 1:c0 + 2], (C, 2 * C))
        return jnp.where(first_half, a, b)

    blk_mask = [jnp.where(blk == e, 1.0, 0.0).astype(BF16) for e in range(4)]

    def block_diag(mw):
        return jnp.concatenate([mw * blk_mask[e] for e in range(4)], axis=0)

    def prep(i, carry):
        chunks = [i * unroll_a + u for u in range(unroll_a)]
        r0s = [pl.multiple_of(c * C, C) for c in chunks]
        gcs, valss = [], []
        for r0 in r0s:
            ab = ab_ref[pl.ds(r0, C), :]
            sp_in = ab + dtb_ref[...]
            softplus = jnp.maximum(sp_in, 0.0) + jnp.log(1.0 + jnp.exp(-jnp.abs(sp_in)))
            gval = -jnp.exp(alog_ref[...]) * softplus
            vals = pltpu.roll(jnp.where(lane < H, gval, _sigmoid(ab)), shift, 1)
            gc = jnp.dot(tril, vals, preferred_element_type=F32, precision=HIGHEST)
            gc_ref[pl.ds(r0, C), :] = gc
            vl_ref[pl.ds(r0, C), :] = vals
            gcs.append(gc)
            valss.append(vals)
        qbs = [q_ref[pl.ds(r0, C), h * HEAD:(h + 1) * HEAD] for r0 in r0s for h in range(G)]
        kbs = [k_ref[pl.ds(r0, C), h * HEAD:(h + 1) * HEAD] for r0 in r0s for h in range(G)]
        qkk = [lax.dot_general(jnp.concatenate([qb, kb], axis=0), jnp.concatenate([kb, kb], axis=0),
                               (((1,), (1,)), ((), ())), preferred_element_type=F32)
               for qb, kb in zip(qbs, kbs)]
        ps, ms = [], []
        for u, r0 in enumerate(r0s):
            for w in range(NW):
                top, bot = qkk[u * G + 2 * w], qkk[u * G + 2 * w + 1]
                qk_w = jnp.concatenate([top[:C], bot[:C]], axis=1)
                kk_w = jnp.concatenate([top[C:], bot[C:]], axis=1)
                gcol = jnp.concatenate([pair_cols(gcs[u], 4 * w), pair_cols(gcs[u], 4 * w + 2)], axis=1)
                bcol = jnp.concatenate([pair_cols(valss[u], H + 4 * w),
                                        pair_cols(valss[u], H + 4 * w + 2)], axis=1)
                grow = jnp.sum(gcol * eye_w, axis=0, keepdims=True)
                dec = jnp.exp(jnp.where(incl_w, gcol - grow, -jnp.inf))
                lmat = jnp.where(strict_w, kk_w * dec * bcol, 0.0)
                a_ref[pl.ds(r0, C), w * WIDE:(w + 1) * WIDE] = (qk_w * dec).astype(BF16)
                ps.append(eye_w - lmat)
                ms.append(lmat)
        ms = [_bdot(m, block_diag(m.astype(BF16))) for m in ms]
        for _ in range(int(math.log2(C)) - 2):
            outs = [jnp.dot(jnp.concatenate([p, m], axis=0).astype(BF16), block_diag(m.astype(BF16)),
                            preferred_element_type=F32) for p, m in zip(ps, ms)]
            ps = [p + o[:C] for p, o in zip(ps, outs)]
            ms = [o[C:] for o in outs]
        outs = [_bdot(p, block_diag(m.astype(BF16))) for p, m in zip(ps, ms)]
        for u, r0 in enumerate(r0s):
            for w in range(NW):
                t_ref[pl.ds(r0, C), w * WIDE:(w + 1) * WIDE] = (ps[u * NW + w] + outs[u * NW + w]).astype(BF16)
        return carry

    lax.fori_loop(0, St // (C * unroll_a), prep, 0)

    def scan_chunk(c, carry):
        r0 = pl.multiple_of(c * C, C)
        gc = gc_ref[pl.ds(r0, C), :]
        vals = vl_ref[pl.ds(r0, C), :]
        kbs = [k_ref[pl.ds(r0, C), h * HEAD:(h + 1) * HEAD] for h in range(G)]
        ks = [jnp.dot(jnp.concatenate([q_ref[pl.ds(r0, C), h * HEAD:(h + 1) * HEAD], kbs[h]], axis=0),
                      jnp.concatenate([s_ref[2 * h], s_ref[2 * h + 1]], axis=1).astype(BF16),
                      preferred_element_type=F32) for h in range(G)]
        egs, gcols, rhss = [], [], []
        for hv in range(2 * G):
            h, e = divmod(hv, 2)
            gcol = jnp.broadcast_to(gc[:, hv:hv + 1], (C, LANES))
            bcol = jnp.broadcast_to(vals[:, H + hv:H + hv + 1], (C, LANES))
            eg = jnp.exp(gcol)
            v = v_ref[pl.ds(r0, C), hv * HEAD:(hv + 1) * HEAD].astype(F32)
            rhss.append((bcol * (v - eg * ks[h][C:, e * HEAD:(e + 1) * HEAD])).astype(BF16))
            egs.append(eg)
            gcols.append(gcol)
        vnews = [jnp.dot(t_ref[pl.ds(r0, C), hv * C:(hv + 1) * C], rhss[hv],
                         preferred_element_type=F32) for hv in range(2 * G)]
        intra = [jnp.dot(a_ref[pl.ds(r0, C), hv * C:(hv + 1) * C], vnews[hv].astype(BF16),
                         preferred_element_type=F32) for hv in range(2 * G)]
        vnd = [(jnp.exp(gcols[hv][C - 1:C, :] - gcols[hv]) * vnews[hv]).astype(BF16)
               for hv in range(2 * G)]
        upd = [lax.dot_general(kbs[h], jnp.concatenate([vnd[2 * h], vnd[2 * h + 1]], axis=1),
                               (((0,), (0,)), ((), ())), preferred_element_type=F32)
               for h in range(G)]
        for hv in range(2 * G):
            h, e = divmod(hv, 2)
            s_ref[hv] = (s_ref[hv] * jnp.exp(gcols[hv][C - 1:C, :])
                         + upd[h][:, e * HEAD:(e + 1) * HEAD])
            o = egs[hv] * ks[h][:C, e * HEAD:(e + 1) * HEAD] + intra[hv]
            zz = z_ref[pl.ds(r0, C), hv * HEAD:(hv + 1) * HEAD].astype(F32)
            on = o * lax.rsqrt(jnp.mean(o * o, axis=-1, keepdims=True) + RMS_EPS)
            on = on * onw_ref[...] * (zz * _sigmoid(zz))
            o_ref[pl.ds(r0, C), hv * HEAD:(hv + 1) * HEAD] = on.astype(o_ref.dtype)
        return carry

    lax.fori_loop(0, St // C, scan_chunk, 0)


def _delta(proj3, ab3, a_log, dt_bias, onorm_w, qk_width, v_width, G, St):
    B, S, _ = proj3.shape
    H = a_log.shape[0]
    n_groups = qk_width // (HEAD * G)
    qw, vw = HEAD * G, 2 * HEAD * G
    k_blk0 = qk_width // qw
    v_blk0 = (2 * qk_width) // vw
    z_blk0 = (2 * qk_width + v_width) // vw
    assert (2 * qk_width) % vw == 0 and (2 * qk_width + v_width) % vw == 0
    assert 2 * H <= LANES and S % CHUNK == 0
    assert G % 2 == 0 and S % St == 0 and St % CHUNK == 0
    pad = lambda p: jnp.zeros((1, LANES), F32).at[0, :H].set(p.astype(F32))
    return pl.pallas_call(
        functools.partial(_delta_kernel, G=G, H=H, unroll_a=_pick(4, St // CHUNK)),
        grid=(B, n_groups, S // St),
        in_specs=[
            pl.BlockSpec((None, St, qw), lambda b, j, t: (b, t, j)),
            pl.BlockSpec((None, St, qw), lambda b, j, t: (b, t, k_blk0 + j)),
            pl.BlockSpec((None, St, vw), lambda b, j, t: (b, t, v_blk0 + j)),
            pl.BlockSpec((None, St, vw), lambda b, j, t: (b, t, z_blk0 + j)),
            pl.BlockSpec((None, St, LANES), lambda b, j, t: (b, t, 0)),
            pl.BlockSpec((1, LANES), lambda b, j, t: (0, 0)),
            pl.BlockSpec((1, LANES), lambda b, j, t: (0, 0)),
            pl.BlockSpec((1, HEAD), lambda b, j, t: (0, 0)),
        ],
        out_specs=pl.BlockSpec((None, St, vw), lambda b, j, t: (b, t, j)),
        out_shape=jax.ShapeDtypeStruct((B, S, v_width), BF16),
        scratch_shapes=[
            pltpu.VMEM((2 * G, HEAD, HEAD), F32),
            pltpu.VMEM((St, qw), BF16),
            pltpu.VMEM((St, qw), BF16),
            pltpu.VMEM((St, LANES), F32),
            pltpu.VMEM((St, LANES), F32),
        ],
        compiler_params=_cparams(("arbitrary", "arbitrary", "arbitrary")),
        name="delta",
    )(proj3, proj3, proj3, proj3, ab3, pad(a_log), pad(dt_bias), onorm_w.reshape(1, HEAD).astype(F32))


POOL_HALO = 16


def _pool_kernel(p_ref, w_ref, sc_ref, o_ref, pad_ref, *, rt):
    S = p_ref.shape[0]
    g = pl.program_id(1)
    pad_ref[0:POOL_HALO, :] = jnp.zeros((POOL_HALO, pad_ref.shape[1]), F32)
    pad_ref[POOL_HALO:, :] = p_ref[...].astype(F32)
    win = jnp.left_shift(2, g).astype(F32)

    def body(r, c):
        r0 = pl.multiple_of(r * rt, rt)
        x = pad_ref[pl.ds(r0, rt + POOL_HALO), :]
        y1 = x + pltpu.roll(x, 1, 0)
        y2 = y1 + pltpu.roll(y1, 2, 0)
        y3 = y2 + pltpu.roll(y2, 4, 0)
        y4 = y3 + pltpu.roll(y3, 8, 0)
        ysel = jnp.where(g == 0, y1, jnp.where(g == 1, y2, jnp.where(g == 2, y3, y4)))
        pos = (r0 + 1 + lax.broadcasted_iota(jnp.int32, (rt, 1), 0)).astype(F32)
        cnt = jnp.minimum(pos, win)
        mixed = ysel[POOL_HALO:] / cnt - x[POOL_HALO:]
        out = _bdot(mixed, w_ref[...]) * sc_ref[...]
        o_ref[pl.ds(r0, rt), :] = out.astype(o_ref.dtype)
        return c

    lax.fori_loop(0, S // rt, body, 0)


def _pool(proj3, w_pool, pool_scale, p_off, d_model):
    B, S, _ = proj3.shape
    ng, cg, _ = w_pool.shape
    assert POOL_WINDOWS == tuple(2 << i for i in range(ng)) and max(POOL_WINDOWS) <= POOL_HALO
    assert p_off % cg == 0
    blk0 = p_off // cg
    rt = min(256, S)
    return pl.pallas_call(
        functools.partial(_pool_kernel, rt=rt),
        grid=(B, ng),
        in_specs=[
            pl.BlockSpec((None, S, cg), lambda b, g: (b, 0, blk0 + g)),
            pl.BlockSpec((None, cg, cg), lambda b, g: (g, 0, 0)),
            pl.BlockSpec((1, cg), lambda b, g: (0, g)),
        ],
        out_specs=pl.BlockSpec((None, S, cg), lambda b, g: (b, 0, g)),
        out_shape=jax.ShapeDtypeStruct((B, S, d_model), BF16),
        scratch_shapes=[pltpu.VMEM((S + POOL_HALO, cg), F32)],
        compiler_params=_cparams(("arbitrary", "arbitrary")),
        name="pool",
    )(proj3, w_pool.astype(BF16), pool_scale.reshape(1, -1).astype(F32))


def _branch_kernel(oa_ref, op_ref, wa_ref, wp_ref, ga_ref, gp_ref, bga_ref, bgp_ref, h_ref):
    a = jnp.dot(oa_ref[...], wa_ref[...], preferred_element_type=F32)
    p = jnp.dot(op_ref[...], wp_ref[...], preferred_element_type=F32)
    ga = _sigmoid(ga_ref[...].astype(F32) + bga_ref[...])
    gp = _sigmoid(gp_ref[...].astype(F32) + bgp_ref[...])
    h_ref[...] = (ga * a + gp * p).astype(h_ref.dtype)


def _branch(o_a, o_p, proj, w_a, w_p, b_gate, ga_off, gp_off, tm, tn):
    T, V = o_a.shape
    D = w_a.shape[1]
    assert ga_off % tn == 0 and gp_off % tn == 0
    ga0, gp0 = ga_off // tn, gp_off // tn
    nd = D // tn
    bg = b_gate.reshape(1, 2 * D).astype(F32)
    return pl.pallas_call(
        _branch_kernel,
        grid=(T // tm, nd),
        in_specs=[
            pl.BlockSpec((tm, V), lambda i, j: (i, 0)),
            pl.BlockSpec((tm, D), lambda i, j: (i, 0)),
            pl.BlockSpec((V, tn), lambda i, j: (0, j)),
            pl.BlockSpec((D, tn), lambda i, j: (0, j)),
            pl.BlockSpec((tm, tn), lambda i, j: (i, ga0 + j)),
            pl.BlockSpec((tm, tn), lambda i, j: (i, gp0 + j)),
            pl.BlockSpec((1, tn), lambda i, j: (0, j)),
            pl.BlockSpec((1, tn), lambda i, j: (0, nd + j)),
        ],
        out_specs=pl.BlockSpec((tm, tn), lambda i, j: (i, j)),
        out_shape=jax.ShapeDtypeStruct((T, D), BF16),
        compiler_params=_cparams(("arbitrary", "arbitrary")),
        name="branch",
    )(o_a, o_p, w_a, w_p, proj, proj, bg, bg)


def _pack_pair(lo, hi):
    lo = lax.bitcast_convert_type(lo.astype(BF16).astype(F32), jnp.uint32)
    hi = lax.bitcast_convert_type(hi.astype(BF16).astype(F32), jnp.uint32)
    return jnp.right_shift(lo, jnp.uint32(16)) | (hi & jnp.uint32(0xFFFF0000))


def _pack_halves(x):
    half = x.shape[1] // 2
    return _pack_pair(x[:, :half], x[:, half:])


def _unpack_halves(u):
    lo = lax.bitcast_convert_type(jnp.left_shift(u, jnp.uint32(16)), F32)
    hi = lax.bitcast_convert_type(u & jnp.uint32(0xFFFF0000), F32)
    return lo, hi


def _split_bf16(x):
    hi = x.astype(BF16)
    return hi, (x - hi.astype(F32)).astype(BF16)


def _post_kernel(x_ref, h_ref, wo_ref, nw_ref, rwh_ref, rwl_ref, rb_ref,
                 x1_ref, xp_ref, idx_ref, gate_ref, *, n_experts):
    x1 = x_ref[...] + jnp.dot(h_ref[...], wo_ref[...], preferred_element_type=F32)
    x1_ref[...] = x1
    xn = x1 * lax.rsqrt(jnp.mean(x1 * x1, axis=-1, keepdims=True) + RMS_EPS) * nw_ref[...]
    xp_ref[...] = _pack_halves(xn)
    xh, xl = _split_bf16(xn)
    logits = (jnp.dot(xh, rwh_ref[...], preferred_element_type=F32)
              + jnp.dot(xh, rwl_ref[...], preferred_element_type=F32)
              + jnp.dot(xl, rwh_ref[...], preferred_element_type=F32) + rb_ref[...])
    lane = lax.broadcasted_iota(jnp.int32, logits.shape, 1)
    work = jnp.where(lane < n_experts, logits, -jnp.inf)
    vals, idxs = [], []
    for _ in range(TOP_K):
        m = jnp.max(work, axis=-1, keepdims=True)
        idx = jnp.min(jnp.where(work == m, lane, LANES), axis=-1, keepdims=True)
        vals.append(m)
        idxs.append(idx)
        work = jnp.where(lane == idx, -jnp.inf, work)
    ex = [jnp.exp(v - vals[0]) for v in vals]
    den = ex[0]
    for e in ex[1:]:
        den = den + e
    idx_out = jnp.zeros(logits.shape, jnp.int32)
    gate_out = jnp.zeros(logits.shape, F32)
    for k in range(TOP_K):
        idx_out = jnp.where(lane == k, idxs[k], idx_out)
        gate_out = jnp.where(lane == k, ex[k] / den, gate_out)
    idx_ref[...] = idx_out
    gate_ref[...] = gate_out


def _post(x2, h, w_out, norm2_w, router_w, router_b, tm):
    T, D = x2.shape
    E = router_w.shape[1]
    rw = jnp.zeros((D, LANES), F32).at[:, :E].set(router_w.astype(F32))
    rwh, rwl = _split_bf16(rw)
    rb = jnp.zeros((1, LANES), F32).at[0, :E].set(router_b.astype(F32))
    return pl.pallas_call(
        functools.partial(_post_kernel, n_experts=E),
        grid=(T // tm,),
        in_specs=[
            pl.BlockSpec((tm, D), lambda i: (i, 0)),
            pl.BlockSpec((tm, D), lambda i: (i, 0)),
            pl.BlockSpec((D, D), lambda i: (0, 0)),
            pl.BlockSpec((1, D), lambda i: (0, 0)),
            pl.BlockSpec((D, LANES), lambda i: (0, 0)),
            pl.BlockSpec((D, LANES), lambda i: (0, 0)),
            pl.BlockSpec((1, LANES), lambda i: (0, 0)),
        ],
        out_specs=[
            pl.BlockSpec((tm, D), lambda i: (i, 0)),
            pl.BlockSpec((tm, D // 2), lambda i: (i, 0)),
            pl.BlockSpec((tm, LANES), lambda i: (i, 0)),
            pl.BlockSpec((tm, LANES), lambda i: (i, 0)),
        ],
        out_shape=[
            jax.ShapeDtypeStruct((T, D), F32),
            jax.ShapeDtypeStruct((T, D // 2), jnp.uint32),
            jax.ShapeDtypeStruct((T, LANES), jnp.int32),
            jax.ShapeDtypeStruct((T, LANES), F32),
        ],
        compiler_params=_cparams(("arbitrary",)),
        name="post",
    )(x2, h, w_out, norm2_w.reshape(1, D).astype(F32), rwh, rwl, rb)


ISSUE_UNROLL = 8


def _dispatch_kernel(dest_ref, xp_ref, zero_hbm, o_hbm, sem, *, tm):
    del zero_hbm

    def body(g, c):
        for u in range(ISSUE_UNROLL // TOP_K):
            r = g * (ISSUE_UNROLL // TOP_K) + u
            for k in range(TOP_K):
                pltpu.make_async_copy(xp_ref.at[pl.ds(r, 1)],
                                      o_hbm.at[pl.ds(dest_ref[0, 0, r * TOP_K + k], 1)], sem).start()
        return c
    lax.fori_loop(0, tm // (ISSUE_UNROLL // TOP_K), body, 0)
    for _ in range(TOP_K):
        pltpu.make_async_copy(xp_ref, o_hbm.at[pl.ds(0, tm)], sem).wait()


def _dispatch_rows(xp, dest, n_rows, tm):
    T, W = xp.shape
    nt = T // tm
    return pl.pallas_call(
        functools.partial(_dispatch_kernel, tm=tm),
        grid=(nt,),
        in_specs=[
            pl.BlockSpec((1, 1, tm * TOP_K), lambda i: (i, 0, 0), memory_space=pltpu.SMEM),
            pl.BlockSpec((tm, W), lambda i: (i, 0)),
            pl.BlockSpec(memory_space=pl.ANY),
        ],
        out_specs=pl.BlockSpec(memory_space=pl.ANY),
        out_shape=jax.ShapeDtypeStruct((n_rows, W), xp.dtype),
        scratch_shapes=[pltpu.SemaphoreType.DMA(())],
        input_output_aliases={2: 0},
        compiler_params=_cparams(("arbitrary",)),
        name="dispatch",
    )(dest.reshape(nt, 1, tm * TOP_K), xp, jnp.zeros((n_rows, W), xp.dtype))


def _new_expert(be_ref, m):
    return jnp.logical_or(m == 0, be_ref[m] != be_ref[jnp.maximum(m - 1, 0)])


CAST_ROWS = 256


def _gmm1_kernel(be_ref, bv_ref, x_ref, w_ref, b_ref, o_ref, wb_ref):
    m = pl.program_id(1)
    valid = bv_ref[m] == 1

    @pl.when(jnp.logical_and(valid, _new_expert(be_ref, m)))
    def _():
        def body(r, c):
            r0 = pl.multiple_of(r * CAST_ROWS, CAST_ROWS)
            wb_ref[pl.ds(r0, CAST_ROWS), :] = w_ref[pl.ds(r0, CAST_ROWS), :].astype(BF16)
            return c
        lax.fori_loop(0, w_ref.shape[0] // CAST_ROWS, body, 0)

    @pl.when(valid)
    def _():
        even = lax.broadcasted_iota(jnp.int32, (x_ref.shape[0], LANES), 1) % 2 == 0
        lo, hi = _unpack_halves(x_ref[...])
        x = jnp.concatenate([lo.astype(BF16), hi.astype(BF16)], axis=1)

        def act_block(hblk):
            x_glu = jnp.minimum(hblk, SWIGLU_LIMIT)
            x_lin = jnp.clip(hblk, -SWIGLU_LIMIT, SWIGLU_LIMIT) + 1.0
            return x_glu * _sigmoid(SWIGLU_ALPHA * x_glu) * pltpu.roll(x_lin, LANES - 1, 1)

        for i in range(o_ref.shape[1] // LANES):
            sl = slice(2 * i * LANES, 2 * (i + 1) * LANES)
            hs = jnp.dot(x, wb_ref[:, sl], preferred_element_type=F32) + b_ref[:, sl]
            packed = jnp.where(even, act_block(hs[:, :LANES]), pltpu.roll(act_block(hs[:, LANES:]), 1, 1))
            o_ref[:, i * LANES:(i + 1) * LANES] = packed.astype(o_ref.dtype)

    @pl.when(bv_ref[m] == 0)
    def _():
        o_ref[...] = jnp.zeros_like(o_ref)


def _gmm1(block_expert, block_valid, x_buf, w1, b1, bm, tn):
    R, half = x_buf.shape
    E, D, F2 = w1.shape
    nb = R // bm
    assert D == 2 * half and D % CAST_ROWS == 0 and tn % (2 * LANES) == 0
    grid_spec = pltpu.PrefetchScalarGridSpec(
        num_scalar_prefetch=2,
        grid=(F2 // tn, nb),
        in_specs=[
            pl.BlockSpec((bm, half), lambda n, m, be, bv: (m, 0)),
            pl.BlockSpec((None, D, tn), lambda n, m, be, bv: (be[m], 0, n)),
            pl.BlockSpec((None, 1, tn), lambda n, m, be, bv: (be[m], 0, n)),
        ],
        out_specs=pl.BlockSpec((bm, tn // 2), lambda n, m, be, bv: (m, n)),
        scratch_shapes=[pltpu.VMEM((D, tn), BF16)],
    )
    return pl.pallas_call(
        _gmm1_kernel,
        grid_spec=grid_spec,
        out_shape=jax.ShapeDtypeStruct((R, F2 // 2), BF16),
        compiler_params=_cparams(("arbitrary", "arbitrary")),
        name="gmm1",
    )(block_expert, block_valid, x_buf, w1, b1)


def _gmm2_kernel(be_ref, bv_ref, a_ref, w_ref, b_ref, o_ref, wb_ref):
    m = pl.program_id(1)
    valid = bv_ref[m] == 1

    @pl.when(jnp.logical_and(valid, _new_expert(be_ref, m)))
    def _():
        half = LANES // 2

        def body(i, c):
            r0 = pl.multiple_of(i * LANES, LANES)
            words = _pack_pair(w_ref[pl.ds(r0, half), :], w_ref[pl.ds(r0 + half, half), :])
            wb_ref[pl.ds(r0, LANES), :] = pltpu.bitcast(words, BF16)
            return c
        lax.fori_loop(0, w_ref.shape[0] // LANES, body, 0)

    @pl.when(valid)
    def _():
        y = jnp.dot(a_ref[...], wb_ref[...], preferred_element_type=F32) + b_ref[...]
        o_ref[...] = _pack_halves(y)

    @pl.when(bv_ref[m] == 0)
    def _():
        o_ref[...] = jnp.zeros_like(o_ref)


def _gmm2(block_expert, block_valid, act, w2, b2, bm, tn):
    R, F = act.shape
    E, _, D = w2.shape
    nb = R // bm
    assert tn % (2 * LANES) == 0
    grid_spec = pltpu.PrefetchScalarGridSpec(
        num_scalar_prefetch=2,
        grid=(D // tn, nb),
        in_specs=[
            pl.BlockSpec((bm, F), lambda n, m, be, bv: (m, 0)),
            pl.BlockSpec((None, F, tn), lambda n, m, be, bv: (be[m], 0, n)),
            pl.BlockSpec((None, 1, tn), lambda n, m, be, bv: (be[m], 0, n)),
        ],
        out_specs=pl.BlockSpec((bm, tn // 2), lambda n, m, be, bv: (m, n)),
        scratch_shapes=[pltpu.VMEM((F, tn), BF16)],
    )
    return pl.pallas_call(
        _gmm2_kernel,
        grid_spec=grid_spec,
        out_shape=jax.ShapeDtypeStruct((R, D // 2), jnp.uint32),
        compiler_params=_cparams(("arbitrary", "arbitrary")),
        name="gmm2",
    )(block_expert, block_valid, act, w2, b2)


def _combine_kernel(dcur_ref, dnext_ref, gate_ref, x1_ref, nw_ref, y_hbm, o_ref, ybuf, sem, *, tm, y_tile):
    i = pl.program_id(0)
    n = pl.num_programs(0)
    slot = lax.rem(i, 2)

    def copy(d_ref, s, r, k):
        return pltpu.make_async_copy(y_hbm.at[pl.ds(d_ref[0, 0, r * TOP_K + k], 1)],
                                     ybuf.at[s, k, pl.ds(r, 1)], sem.at[s])

    def issue(d_ref, s):
        def body(g, c):
            for u in range(ISSUE_UNROLL // TOP_K):
                for k in range(TOP_K):
                    copy(d_ref, s, g * (ISSUE_UNROLL // TOP_K) + u, k).start()
            return c
        lax.fori_loop(0, tm // (ISSUE_UNROLL // TOP_K), body, 0)

    @pl.when(i == 0)
    def _():
        issue(dcur_ref, 0)

    @pl.when(i + 1 < n)
    def _():
        issue(dnext_ref, 1 - slot)

    for k in range(TOP_K):
        pltpu.make_async_copy(y_hbm.at[pl.ds(0, tm)], ybuf.at[slot, k], sem.at[slot]).wait()

    lane = lax.broadcasted_iota(jnp.int32, (tm, LANES), 1)
    gates = gate_ref[...]
    acc = x1_ref[...]
    hw = y_tile // 2
    for k in range(TOP_K):
        gk = jnp.sum(jnp.where(lane == k, gates, 0.0), axis=-1, keepdims=True)
        parts = []
        for jt in range(ybuf.shape[-1] // hw):
            lo, hi = _unpack_halves(ybuf[slot, k, :, jt * hw:(jt + 1) * hw])
            parts += [lo, hi]
        acc = acc + gk * jnp.concatenate(parts, axis=1)
    out = acc * lax.rsqrt(jnp.mean(acc * acc, axis=-1, keepdims=True) + RMS_EPS) * nw_ref[...]
    o_ref[...] = out


def _combine(dest, gates, x1, norm_f_w, y_buf, tm, y_tile):
    T, D = x1.shape
    nt = T // tm
    d3 = dest.reshape(nt, 1, tm * TOP_K)
    return pl.pallas_call(
        functools.partial(_combine_kernel, tm=tm, y_tile=y_tile),
        grid=(nt,),
        in_specs=[
            pl.BlockSpec((1, 1, tm * TOP_K), lambda i: (i, 0, 0), memory_space=pltpu.SMEM),
            pl.BlockSpec((1, 1, tm * TOP_K), lambda i: (jnp.minimum(i + 1, nt - 1), 0, 0),
                         memory_space=pltpu.SMEM),
            pl.BlockSpec((tm, LANES), lambda i: (i, 0)),
            pl.BlockSpec((tm, D), lambda i: (i, 0)),
            pl.BlockSpec((1, D), lambda i: (0, 0)),
            pl.BlockSpec(memory_space=pl.ANY),
        ],
        out_specs=pl.BlockSpec((tm, D), lambda i: (i, 0)),
        out_shape=jax.ShapeDtypeStruct((T, D), F32),
        scratch_shapes=[pltpu.VMEM((2, TOP_K, tm, D // 2), jnp.uint32), pltpu.SemaphoreType.DMA((2,))],
        compiler_params=_cparams(("arbitrary",)),
        name="combine",
    )(d3, d3, gates, x1, norm_f_w.reshape(1, D).astype(F32), y_buf)


def _routing(top_idx, n_experts, bm):
    T = top_idx.shape[0]
    A = T * TOP_K
    flat_e = top_idx.reshape(A)
    onehot = (flat_e[:, None] == jnp.arange(n_experts, dtype=jnp.int32)[None, :]).astype(jnp.int32)
    csum = jnp.cumsum(onehot, axis=0)
    rank = jnp.sum((csum - onehot) * onehot, axis=1)
    counts = csum[-1]
    padded = (counts + bm - 1) // bm * bm
    pend = jnp.cumsum(padded)
    pstart = pend - padded
    dest = jnp.sum(onehot * pstart[None, :], axis=1) + rank
    R = (A + bm - 1) // bm * bm + n_experts * bm
    nb = R // bm
    block_start = jnp.arange(nb, dtype=jnp.int32) * bm
    block_expert = jnp.minimum(jnp.sum((block_start[:, None] >= pend[None, :]).astype(jnp.int32), axis=1),
                               n_experts - 1)
    block_valid = (block_start < pend[-1]).astype(jnp.int32)
    return dest.astype(jnp.int32), R, block_expert, block_valid


def _pick(pref, n):
    t = min(pref, n)
    while n % t:
        t //= 2
    return t


def kernel(x, norm1_w, w_in, b_gate, conv_w, A_log, dt_bias, onorm_w, w_branch_a, w_pool,
           pool_scale, w_branch_p, w_out, norm2_w, router_w, router_b, w1, b1, w2, b2, norm_f_w):
    B, S, D = x.shape
    T = B * S
    depth = norm1_w.shape[0]
    x2 = x.reshape(T, D)
    for l in range(depth):
        conv_ch = conv_w.shape[-1]
        v_width = w_branch_a.shape[1]
        qk_width = (conv_ch - v_width) // 2
        n_v = A_log.shape[-1]
        qkvz = conv_ch + v_width
        ab_lo, ab_hi = qkvz, qkvz + 2 * n_v
        E = router_w.shape[-1]

        wl = w_in[l]
        w_main = jnp.concatenate([wl[:, :qkvz], wl[:, ab_hi:]], axis=1).astype(BF16)
        w_ab = jnp.zeros((D, LANES), BF16).at[:, :2 * n_v].set(wl[:, ab_lo:ab_hi].astype(BF16))
        p_off, ga_off, gp_off = qkvz, qkvz + D, qkvz + 2 * D

        tn_in = _pick(1024, math.gcd(qk_width, v_width, w_main.shape[1]))
        tm_in = _pick(1024, S // 2 if S >= 2 * CONV_HALO else S)
        proj, ab = _in_proj(x2, norm1_w[l], w_main, w_ab, conv_w[l], qk_width, v_width, S, tm_in, tn_in)
        proj3 = proj.reshape(B, S, -1)
        G = 8 if (qk_width // HEAD) % 8 == 0 else 2
        o_a = _delta(proj3, ab.reshape(B, S, LANES), A_log[l], dt_bias[l], onorm_w[l],
                     qk_width, v_width, G, _pick(512, S))
        o_p = _pool(proj3, w_pool[l], pool_scale[l], p_off, D)
        h = _branch(o_a.reshape(T, v_width), o_p.reshape(T, D), proj,
                    w_branch_a[l].astype(BF16), w_branch_p[l].astype(BF16), b_gate[l],
                    ga_off, gp_off, _pick(1024, T), _pick(512, D))
        x1, xp, idx_pad, gate_pad = _post(x2, h, w_out[l].astype(BF16), norm2_w[l],
                                          router_w[l], router_b[l], _pick(512, T))

        bm = 512
        y_tile = _pick(1024, D)
        dest, n_rows, block_expert, block_valid = _routing(idx_pad[:, :TOP_K], E, bm)
        x_buf = _dispatch_rows(xp, dest, n_rows, _pick(512, T))
        act = _gmm1(block_expert, block_valid, x_buf, w1[l].astype(F32),
                    b1[l][:, None, :].astype(F32), bm, _pick(2048, w1.shape[-1]))
        y_buf = _gmm2(block_expert, block_valid, act, w2[l].astype(F32),
                      b2[l][:, None, :].astype(F32), bm, y_tile)
        is_last = l == depth - 1
        assert is_last, "final norm is fused into the last layer's combine"
        x2 = _combine(dest, gate_pad, x1, norm_f_w, y_buf, _pick(128, T), y_tile)
    return x2.reshape(B, S, D)
```

```python
import functools
import math

import jax
import jax.numpy as jnp
from jax import lax
from jax.experimental import pallas as pl
from jax.experimental.pallas import tpu as pltpu

F32 = jnp.float32
BF16 = jnp.bfloat16

HEAD = 128
CHUNK = 64
CONV_WIDTH = 4
POOL_WINDOWS = (2, 4, 8, 16)
TOP_K = 4
SWIGLU_ALPHA = 1.702
SWIGLU_LIMIT = 7.0
RMS_EPS = 1e-6
L2_EPS = 1e-6
LANES = 128
HIGHEST = lax.Precision.HIGHEST
VMEM_LIMIT = 56 * 1024 * 1024


def _cparams(sem):
    return pltpu.CompilerParams(dimension_semantics=sem, vmem_limit_bytes=VMEM_LIMIT)


def _sigmoid(x):
    return 1.0 / (1.0 + jnp.exp(-x))


def _bdot(a, b):
    return jnp.dot(a.astype(BF16), b.astype(BF16), preferred_element_type=F32)


CONV_HALO = 8


def _inproj_kernel(x_ref, nw_ref, w_ref, wab_ref, cw_ref, o_ref, ab_ref, xn_ref, halo_ref, acc_ref, *,
                   rows, n_q, n_k, n_conv, tiles_per_seq):
    i = pl.program_id(0)
    j = pl.program_id(1)
    tm, tn = o_ref.shape

    @pl.when(j == 0)
    def _():
        def body(r, c):
            r0 = pl.multiple_of(r * rows, rows)
            x = x_ref[pl.ds(r0, rows), :]
            inv = lax.rsqrt(jnp.mean(x * x, axis=-1, keepdims=True) + RMS_EPS)
            xn_ref[pl.ds(r0, rows), :] = (x * inv * nw_ref[...]).astype(BF16)
            return c
        lax.fori_loop(0, tm // rows, body, 0)
        ab_ref[...] = jnp.dot(xn_ref[...], wab_ref[...], preferred_element_type=F32)

    @pl.when(j >= n_conv)
    def _():
        o_ref[...] = jnp.dot(xn_ref[...], w_ref[...], preferred_element_type=F32).astype(o_ref.dtype)

    def conv_tile(normalise):
        jc = jnp.minimum(j, n_conv - 1)
        first = lax.rem(i, tiles_per_seq) == 0
        qscale = jnp.where(j < n_q, HEAD ** -0.5, 1.0).astype(F32)
        acc_ref[:CONV_HALO, :] = jnp.where(first, 0.0, halo_ref[jc])
        acc_ref[CONV_HALO:, :] = jnp.dot(xn_ref[...], w_ref[...], preferred_element_type=F32)
        halo_ref[jc] = acc_ref[tm:, :]
        for c in range(tn // LANES):
            sl = slice(c * LANES, (c + 1) * LANES)
            y = acc_ref[CONV_HALO:, sl] * cw_ref[CONV_WIDTH - 1:CONV_WIDTH, sl]
            for t in range(CONV_WIDTH - 1):
                sh = CONV_WIDTH - 1 - t
                y = y + acc_ref[CONV_HALO - sh:CONV_HALO - sh + tm, sl] * cw_ref[t:t + 1, sl]
            y = y * _sigmoid(y)
            if normalise:
                y = y * (lax.rsqrt(jnp.sum(y * y, axis=-1, keepdims=True) + L2_EPS) * qscale)
            o_ref[:, sl] = y.astype(o_ref.dtype)

    @pl.when(j < n_q + n_k)
    def _():
        conv_tile(True)

    @pl.when(jnp.logical_and(j >= n_q + n_k, j < n_conv))
    def _():
        conv_tile(False)


def _in_proj(x2, norm_w, w_main, w_ab, conv_w, qk_width, v_width, seq_len, tm, tn):
    T, D = x2.shape
    N = w_main.shape[1]
    rows = min(128, tm)
    assert qk_width % tn == 0 and v_width % tn == 0 and tn % HEAD == 0 and seq_len % tm == 0
    assert CONV_WIDTH - 1 <= CONV_HALO <= tm
    n_q = n_k = qk_width // tn
    n_conv = n_q + n_k + v_width // tn
    return pl.pallas_call(
        functools.partial(_inproj_kernel, rows=rows, n_q=n_q, n_k=n_k, n_conv=n_conv,
                          tiles_per_seq=seq_len // tm),
        grid=(T // tm, N // tn),
        in_specs=[
            pl.BlockSpec((tm, D), lambda i, j: (i, 0)),
            pl.BlockSpec((1, D), lambda i, j: (0, 0)),
            pl.BlockSpec((D, tn), lambda i, j: (0, j)),
            pl.BlockSpec((D, LANES), lambda i, j: (0, 0)),
            pl.BlockSpec((CONV_WIDTH, tn), lambda i, j: (0, jnp.minimum(j, n_conv - 1))),
        ],
        out_specs=[
            pl.BlockSpec((tm, tn), lambda i, j: (i, j)),
            pl.BlockSpec((tm, LANES), lambda i, j: (i, 0)),
        ],
        out_shape=[jax.ShapeDtypeStruct((T, N), BF16), jax.ShapeDtypeStruct((T, LANES), F32)],
        scratch_shapes=[pltpu.VMEM((tm, D), BF16), pltpu.VMEM((n_conv, CONV_HALO, tn), F32),
                        pltpu.VMEM((CONV_HALO + tm, tn), F32)],
        compiler_params=_cparams(("arbitrary", "arbitrary")),
        name="in_proj",
    )(x2, norm_w.reshape(1, D), w_main, w_ab, conv_w.astype(F32))


def _delta_kernel(q_ref, k_ref, v_ref, z_ref, ab_ref, alog_ref, dtb_ref, onw_ref, o_ref,
                  s_ref, t_ref, a_ref, gc_ref, vl_ref, *, G, H, unroll_a):
    St = q_ref.shape[0]
    C = CHUNK
    WIDE = 4 * C
    NW = G // 2
    t_idx = pl.program_id(2)
    j = pl.program_id(1)

    @pl.when(t_idx == 0)
    def _():
        s_ref[...] = jnp.zeros_like(s_ref)

    row = lax.broadcasted_iota(jnp.int32, (C, WIDE), 0)
    lanew = lax.broadcasted_iota(jnp.int32, (C, WIDE), 1)
    colw = lanew % C
    blk = lanew // C
    incl_w = row >= colw
    strict_w = row > colw
    eye_w = jnp.where(row == colw, 1.0, 0.0).astype(F32)
    first_half = (lax.broadcasted_iota(jnp.int32, (C, 2 * C), 1) < C)
    r2 = lax.broadcasted_iota(jnp.int32, (C, C), 0)
    c2 = lax.broadcasted_iota(jnp.int32, (C, C), 1)
    tril = jnp.where(r2 >= c2, 1.0, 0.0).astype(F32)
    lane = lax.broadcasted_iota(jnp.int32, (C, LANES), 1)
    shift = lax.rem(LANES - 2 * G * j, LANES)

    def pair_cols(x, c0):
        a = jnp.broadcast_to(x[:, c0:c0 + 1], (C, 2 * C))
        b = jnp.broadcast_to(x[:, c0 + 1:c0 + 2], (C, 2 * C))
        return jnp.where(first_half, a, b)

    blk_mask = [jnp.where(blk == e, 1.0, 0.0).astype(BF16) for e in range(4)]

    def block_diag(mw):
        return jnp.concatenate([mw * blk_mask[e] for e in range(4)], axis=0)

    def prep(i, carry):
        chunks = [i * unroll_a + u for u in range(unroll_a)]
        r0s = [pl.multiple_of(c * C, C) for c in chunks]
        gcs, valss = [], []
        for r0 in r0s:
            ab = ab_ref[pl.ds(r0, C), :]
            sp_in = ab + dtb_ref[...]
            softplus = jnp.maximum(sp_in, 0.0) + jnp.log(1.0 + jnp.exp(-jnp.abs(sp_in)))
            gval = -jnp.exp(alog_ref[...]) * softplus
            vals = pltpu.roll(jnp.where(lane < H, gval, _sigmoid(ab)), shift, 1)
            gc = jnp.dot(tril, vals, preferred_element_type=F32, precision=HIGHEST)
            gc_ref[pl.ds(r0, C), :] = gc
            vl_ref[pl.ds(r0, C), :] = vals
            gcs.append(gc)
            valss.append(vals)
        qbs = [q_ref[pl.ds(r0, C), h * HEAD:(h + 1) * HEAD] for r0 in r0s for h in range(G)]
        kbs = [k_ref[pl.ds(r0, C), h * HEAD:(h + 1) * HEAD] for r0 in r0s for h in range(G)]
        qkk = [lax.dot_general(jnp.concatenate([qb, kb], axis=0), jnp.concatenate([kb, kb], axis=0),
                               (((1,), (1,)), ((), ())), preferred_element_type=F32)
               for qb, kb in zip(qbs, kbs)]
        ps, ms = [], []
        for u, r0 in enumerate(r0s):
            for w in range(NW):
                top, bot = qkk[u * G + 2 * w], qkk[u * G + 2 * w + 1]
                qk_w = jnp.concatenate([top[:C], bot[:C]], axis=1)
                kk_w = jnp.concatenate([top[C:], bot[C:]], axis=1)
                gcol = jnp.concatenate([pair_cols(gcs[u], 4 * w), pair_cols(gcs[u], 4 * w + 2)], axis=1)
                bcol = jnp.concatenate([pair_cols(valss[u], H + 4 * w),
                                        pair_cols(valss[u], H + 4 * w + 2)], axis=1)
                grow = jnp.sum(gcol * eye_w, axis=0, keepdims=True)
                dec = jnp.exp(jnp.where(incl_w, gcol - grow, -jnp.inf))
                lmat = jnp.where(strict_w, kk_w * dec * bcol, 0.0)
                a_ref[pl.ds(r0, C), w * WIDE:(w + 1) * WIDE] = (qk_w * dec).astype(BF16)
                ps.append(eye_w - lmat)
                ms.append(lmat)
        ms = [_bdot(m, block_diag(m.astype(BF16))) for m in ms]
        for _ in range(int(math.log2(C)) - 2):
            outs = [jnp.dot(jnp.concatenate([p, m], axis=0).astype(BF16), block_diag(m.astype(BF16)),
                            preferred_element_type=F32) for p, m in zip(ps, ms)]
            ps = [p + o[:C] for p, o in zip(ps, outs)]
            ms = [o[C:] for o in outs]
        outs = [_bdot(p, block_diag(m.astype(BF16))) for p, m in zip(ps, ms)]
        for u, r0 in enumerate(r0s):
            for w in range(NW):
                t_ref[pl.ds(r0, C), w * WIDE:(w + 1) * WIDE] = (ps[u * NW + w] + outs[u * NW + w]).astype(BF16)
        return carry

    lax.fori_loop(0, St // (C * unroll_a), prep, 0)

    def scan_chunk(c, carry):
        r0 = pl.multiple_of(c * C, C)
        gc = gc_ref[pl.ds(r0, C), :]
        vals = vl_ref[pl.ds(r0, C), :]
        kbs = [k_ref[pl.ds(r0, C), h * HEAD:(h + 1) * HEAD] for h in range(G)]
        ks = [jnp.dot(jnp.concatenate([q_ref[pl.ds(r0, C), h * HEAD:(h + 1) * HEAD], kbs[h]], axis=0),
                      jnp.concatenate([s_ref[2 * h], s_ref[2 * h + 1]], axis=1).astype(BF16),
                      preferred_element_type=F32) for h in range(G)]
        egs, gcols, rhss = [], [], []
        for hv in range(2 * G):
            h, e = divmod(hv, 2)
            gcol = jnp.broadcast_to(gc[:, hv:hv + 1], (C, LANES))
            bcol = jnp.broadcast_to(vals[:, H + hv:H + hv + 1], (C, LANES))
            eg = jnp.exp(gcol)
            v = v_ref[pl.ds(r0, C), hv * HEAD:(hv + 1) * HEAD].astype(F32)
            rhss.append((bcol * (v - eg * ks[h][C:, e * HEAD:(e + 1) * HEAD])).astype(BF16))
            egs.append(eg)
            gcols.append(gcol)
        vnews = [jnp.dot(t_ref[pl.ds(r0, C), hv * C:(hv + 1) * C], rhss[hv],
                         preferred_element_type=F32) for hv in range(2 * G)]
        intra = [jnp.dot(a_ref[pl.ds(r0, C), hv * C:(hv + 1) * C], vnews[hv].astype(BF16),
                         preferred_element_type=F32) for hv in range(2 * G)]
        vnd = [(jnp.exp(gcols[hv][C - 1:C, :] - gcols[hv]) * vnews[hv]).astype(BF16)
               for hv in range(2 * G)]
        upd = [lax.dot_general(kbs[h], jnp.concatenate([vnd[2 * h], vnd[2 * h + 1]], axis=1),
                               (((0,), (0,)), ((), ())), preferred_element_type=F32)
               for h in range(G)]
        for hv in range(2 * G):
            h, e = divmod(hv, 2)
            s_ref[hv] = (s_ref[hv] * jnp.exp(gcols[hv][C - 1:C, :])
                         + upd[h][:, e * HEAD:(e + 1) * HEAD])
            o = egs[hv] * ks[h][:C, e * HEAD:(e + 1) * HEAD] + intra[hv]
            zz = z_ref[pl.ds(r0, C), hv * HEAD:(hv + 1) * HEAD].astype(F32)
            on = o * lax.rsqrt(jnp.mean(o * o, axis=-1, keepdims=True) + RMS_EPS)
            on = on * onw_ref[...] * (zz * _sigmoid(zz))
            o_ref[pl.ds(r0, C), hv * HEAD:(hv + 1) * HEAD] = on.astype(o_ref.dtype)
        return carry

    lax.fori_loop(0, St // C, scan_chunk, 0)


def _delta(proj3, ab3, a_log, dt_bias, onorm_w, qk_width, v_width, G, St):
    B, S, _ = proj3.shape
    H = a_log.shape[0]
    n_groups = qk_width // (HEAD * G)
    qw, vw = HEAD * G, 2 * HEAD * G
    k_blk0 = qk_width // qw
    v_blk0 = (2 * qk_width) // vw
    z_blk0 = (2 * qk_width + v_width) // vw
    assert (2 * qk_width) % vw == 0 and (2 * qk_width + v_width) % vw == 0
    assert 2 * H <= LANES and S % CHUNK == 0
    assert G % 2 == 0 and S % St == 0 and St % CHUNK == 0
    pad = lambda p: jnp.zeros((1, LANES), F32).at[0, :H].set(p.astype(F32))
    return pl.pallas_call(
        functools.partial(_delta_kernel, G=G, H=H, unroll_a=_pick(4, St // CHUNK)),
        grid=(B, n_groups, S // St),
        in_specs=[
            pl.BlockSpec((None, St, qw), lambda b, j, t: (b, t, j)),
            pl.BlockSpec((None, St, qw), lambda b, j, t: (b, t, k_blk0 + j)),
            pl.BlockSpec((None, St, vw), lambda b, j, t: (b, t, v_blk0 + j)),
            pl.BlockSpec((None, St, vw), lambda b, j, t: (b, t, z_blk0 + j)),
            pl.BlockSpec((None, St, LANES), lambda b, j, t: (b, t, 0)),
            pl.BlockSpec((1, LANES), lambda b, j, t: (0, 0)),
            pl.BlockSpec((1, LANES), lambda b, j, t: (0, 0)),
            pl.BlockSpec((1, HEAD), lambda b, j, t: (0, 0)),
        ],
        out_specs=pl.BlockSpec((None, St, vw), lambda b, j, t: (b, t, j)),
        out_shape=jax.ShapeDtypeStruct((B, S, v_width), BF16),
        scratch_shapes=[
            pltpu.VMEM((2 * G, HEAD, HEAD), F32),
            pltpu.VMEM((St, qw), BF16),
            pltpu.VMEM((St, qw), BF16),
            pltpu.VMEM((St, LANES), F32),
            pltpu.VMEM((St, LANES), F32),
        ],
        compiler_params=_cparams(("arbitrary", "arbitrary", "arbitrary")),
        name="delta",
    )(proj3, proj3, proj3, proj3, ab3, pad(a_log), pad(dt_bias), onorm_w.reshape(1, HEAD).astype(F32))


POOL_HALO = 16


def _pool_kernel(p_ref, w_ref, sc_ref, o_ref, pad_ref, *, rt):
    S = p_ref.shape[0]
    g = pl.program_id(1)
    pad_ref[0:POOL_HALO, :] = jnp.zeros((POOL_HALO, pad_ref.shape[1]), F32)
    pad_ref[POOL_HALO:, :] = p_ref[...].astype(F32)
    win = jnp.left_shift(2, g).astype(F32)

    def body(r, c):
        r0 = pl.multiple_of(r * rt, rt)
        x = pad_ref[pl.ds(r0, rt + POOL_HALO), :]
        y1 = x + pltpu.roll(x, 1, 0)
        y2 = y1 + pltpu.roll(y1, 2, 0)
        y3 = y2 + pltpu.roll(y2, 4, 0)
        y4 = y3 + pltpu.roll(y3, 8, 0)
        ysel = jnp.where(g == 0, y1, jnp.where(g == 1, y2, jnp.where(g == 2, y3, y4)))
        pos = (r0 + 1 + lax.broadcasted_iota(jnp.int32, (rt, 1), 0)).astype(F32)
        cnt = jnp.minimum(pos, win)
        mixed = ysel[POOL_HALO:] / cnt - x[POOL_HALO:]
        out = _bdot(mixed, w_ref[...]) * sc_ref[...]
        o_ref[pl.ds(r0, rt), :] = out.astype(o_ref.dtype)
        return c

    lax.fori_loop(0, S // rt, body, 0)


def _pool(proj3, w_pool, pool_scale, p_off, d_model):
    B, S, _ = proj3.shape
    ng, cg, _ = w_pool.shape
    assert POOL_WINDOWS == tuple(2 << i for i in range(ng)) and max(POOL_WINDOWS) <= POOL_HALO
    assert p_off % cg == 0
    blk0 = p_off // cg
    rt = min(256, S)
    return pl.pallas_call(
        functools.partial(_pool_kernel, rt=rt),
        grid=(B, ng),
        in_specs=[
            pl.BlockSpec((None, S, cg), lambda b, g: (b, 0, blk0 + g)),
            pl.BlockSpec((None, cg, cg), lambda b, g: (g, 0, 0)),
            pl.BlockSpec((1, cg), lambda b, g: (0, g)),
        ],
        out_specs=pl.BlockSpec((None, S, cg), lambda b, g: (b, 0, g)),
        out_shape=jax.ShapeDtypeStruct((B, S, d_model), BF16),
        scratch_shapes=[pltpu.VMEM((S + POOL_HALO, cg), F32)],
        compiler_params=_cparams(("arbitrary", "arbitrary")),
        name="pool",
    )(proj3, w_pool.astype(BF16), pool_scale.reshape(1, -1).astype(F32))


def _branch_kernel(oa_ref, op_ref, wa_ref, wp_ref, ga_ref, gp_ref, bga_ref, bgp_ref, h_ref):
    a = jnp.dot(oa_ref[...], wa_ref[...], preferred_element_type=F32)
    p = jnp.dot(op_ref[...], wp_ref[...], preferred_element_type=F32)
    ga = _sigmoid(ga_ref[...].astype(F32) + bga_ref[...])
    gp = _sigmoid(gp_ref[...].astype(F32) + bgp_ref[...])
    h_ref[...] = (ga * a + gp * p).astype(h_ref.dtype)


def _branch(o_a, o_p, proj, w_a, w_p, b_gate, ga_off, gp_off, tm, tn):
    T, V = o_a.shape
    D = w_a.shape[1]
    assert ga_off % tn == 0 and gp_off % tn == 0
    ga0, gp0 = ga_off // tn, gp_off // tn
    nd = D // tn
    bg = b_gate.reshape(1, 2 * D).astype(F32)
    return pl.pallas_call(
        _branch_kernel,
        grid=(T // tm, nd),
        in_specs=[
            pl.BlockSpec((tm, V), lambda i, j: (i, 0)),
            pl.BlockSpec((tm, D), lambda i, j: (i, 0)),
            pl.BlockSpec((V, tn), lambda i, j: (0, j)),
            pl.BlockSpec((D, tn), lambda i, j: (0, j)),
            pl.BlockSpec((tm, tn), lambda i, j: (i, ga0 + j)),
            pl.BlockSpec((tm, tn), lambda i, j: (i, gp0 + j)),
            pl.BlockSpec((1, tn), lambda i, j: (0, j)),
            pl.BlockSpec((1, tn), lambda i, j: (0, nd + j)),
        ],
        out_specs=pl.BlockSpec((tm, tn), lambda i, j: (i, j)),
        out_shape=jax.ShapeDtypeStruct((T, D), BF16),
        compiler_params=_cparams(("arbitrary", "arbitrary")),
        name="branch",
    )(o_a, o_p, w_a, w_p, proj, proj, bg, bg)


def _pack_pair(lo, hi):
    lo = lax.bitcast_convert_type(lo.astype(BF16).astype(F32), jnp.uint32)
    hi = lax.bitcast_convert_type(hi.astype(BF16).astype(F32), jnp.uint32)
    return jnp.right_shift(lo, jnp.uint32(16)) | (hi & jnp.uint32(0xFFFF0000))


def _pack_halves(x):
    half = x.shape[1] // 2
    return _pack_pair(x[:, :half], x[:, half:])


def _unpack_halves(u):
    lo = lax.bitcast_convert_type(jnp.left_shift(u, jnp.uint32(16)), F32)
    hi = lax.bitcast_convert_type(u & jnp.uint32(0xFFFF0000), F32)
    return lo, hi


def _split_bf16(x):
    hi = x.astype(BF16)
    return hi, (x - hi.astype(F32)).astype(BF16)


def _post_kernel(x_ref, h_ref, wo_ref, nw_ref, rwh_ref, rwl_ref, rb_ref,
                 x1_ref, xp_ref, idx_ref, gate_ref, *, n_experts):
    x1 = x_ref[...] + jnp.dot(h_ref[...], wo_ref[...], preferred_element_type=F32)
    x1_ref[...] = x1
    xn = x1 * lax.rsqrt(jnp.mean(x1 * x1, axis=-1, keepdims=True) + RMS_EPS) * nw_ref[...]
    xp_ref[...] = _pack_halves(xn)
    xh, xl = _split_bf16(xn)
    logits = (jnp.dot(xh, rwh_ref[...], preferred_element_type=F32)
              + jnp.dot(xh, rwl_ref[...], preferred_element_type=F32)
              + jnp.dot(xl, rwh_ref[...], preferred_element_type=F32) + rb_ref[...])
    lane = lax.broadcasted_iota(jnp.int32, logits.shape, 1)
    work = jnp.where(lane < n_experts, logits, -jnp.inf)
    vals, idxs = [], []
    for _ in range(TOP_K):
        m = jnp.max(work, axis=-1, keepdims=True)
        idx = jnp.min(jnp.where(work == m, lane, LANES), axis=-1, keepdims=True)
        vals.append(m)
        idxs.append(idx)
        work = jnp.where(lane == idx, -jnp.inf, work)
    ex = [jnp.exp(v - vals[0]) for v in vals]
    den = ex[0]
    for e in ex[1:]:
        den = den + e
    idx_out = jnp.zeros(logits.shape, jnp.int32)
    gate_out = jnp.zeros(logits.shape, F32)
    for k in range(TOP_K):
        idx_out = jnp.where(lane == k, idxs[k], idx_out)
        gate_out = jnp.where(lane == k, ex[k] / den, gate_out)
    idx_ref[...] = idx_out
    gate_ref[...] = gate_out


def _post(x2, h, w_out, norm2_w, router_w, router_b, tm):
    T, D = x2.shape
    E = router_w.shape[1]
    rw = jnp.zeros((D, LANES), F32).at[:, :E].set(router_w.astype(F32))
    rwh, rwl = _split_bf16(rw)
    rb = jnp.zeros((1, LANES), F32).at[0, :E].set(router_b.astype(F32))
    return pl.pallas_call(
        functools.partial(_post_kernel, n_experts=E),
        grid=(T // tm,),
        in_specs=[
            pl.BlockSpec((tm, D), lambda i: (i, 0)),
            pl.BlockSpec((tm, D), lambda i: (i, 0)),
            pl.BlockSpec((D, D), lambda i: (0, 0)),
            pl.BlockSpec((1, D), lambda i: (0, 0)),
            pl.BlockSpec((D, LANES), lambda i: (0, 0)),
            pl.BlockSpec((D, LANES), lambda i: (0, 0)),
            pl.BlockSpec((1, LANES), lambda i: (0, 0)),
        ],
        out_specs=[
            pl.BlockSpec((tm, D), lambda i: (i, 0)),
            pl.BlockSpec((tm, D // 2), lambda i: (i, 0)),
            pl.BlockSpec((tm, LANES), lambda i: (i, 0)),
            pl.BlockSpec((tm, LANES), lambda i: (i, 0)),
        ],
        out_shape=[
            jax.ShapeDtypeStruct((T, D), F32),
            jax.ShapeDtypeStruct((T, D // 2), jnp.uint32),
            jax.ShapeDtypeStruct((T, LANES), jnp.int32),
            jax.ShapeDtypeStruct((T, LANES), F32),
        ],
        compiler_params=_cparams(("arbitrary",)),
        name="post",
    )(x2, h, w_out, norm2_w.reshape(1, D).astype(F32), rwh, rwl, rb)


ISSUE_UNROLL = 8


def _dispatch_kernel(zf_ref, dest_ref, xp_ref, o_hbm, zbuf, sem, zsem, *, tm, bm):
    @pl.when(pl.program_id(0) == 0)
    def _():
        zbuf[...] = jnp.zeros_like(zbuf)
        nb = o_hbm.shape[0] // bm

        def zcopy(m):
            return pltpu.make_async_copy(zbuf, o_hbm.at[pl.ds(pl.multiple_of(m * bm, bm), bm)], zsem)

        def zstart(m, c):
            @pl.when(zf_ref[m] == 1)
            def _():
                zcopy(m).start()
            return c

        def zwait(m, c):
            @pl.when(zf_ref[m] == 1)
            def _():
                zcopy(m).wait()
            return c

        lax.fori_loop(0, nb, zstart, 0)
        lax.fori_loop(0, nb, zwait, 0)

    def body(g, c):
        for u in range(ISSUE_UNROLL // TOP_K):
            r = g * (ISSUE_UNROLL // TOP_K) + u
            for k in range(TOP_K):
                pltpu.make_async_copy(xp_ref.at[pl.ds(r, 1)],
                                      o_hbm.at[pl.ds(dest_ref[0, 0, r * TOP_K + k], 1)], sem).start()
        return c
    lax.fori_loop(0, tm // (ISSUE_UNROLL // TOP_K), body, 0)
    for _ in range(TOP_K):
        pltpu.make_async_copy(xp_ref, o_hbm.at[pl.ds(0, tm)], sem).wait()


def _dispatch_rows(xp, dest, zero_fill, n_rows, tm, bm):
    T, W = xp.shape
    nt = T // tm
    grid_spec = pltpu.PrefetchScalarGridSpec(
        num_scalar_prefetch=1,
        grid=(nt,),
        in_specs=[
            pl.BlockSpec((1, 1, tm * TOP_K), lambda i, zf: (i, 0, 0), memory_space=pltpu.SMEM),
            pl.BlockSpec((tm, W), lambda i, zf: (i, 0)),
        ],
        out_specs=pl.BlockSpec(memory_space=pl.ANY),
        scratch_shapes=[pltpu.VMEM((bm, W), xp.dtype), pltpu.SemaphoreType.DMA(()),
                        pltpu.SemaphoreType.DMA(())],
    )
    return pl.pallas_call(
        functools.partial(_dispatch_kernel, tm=tm, bm=bm),
        grid_spec=grid_spec,
        out_shape=jax.ShapeDtypeStruct((n_rows, W), xp.dtype),
        compiler_params=_cparams(("arbitrary",)),
        name="dispatch",
    )(zero_fill, dest.reshape(nt, 1, tm * TOP_K), xp)


def _new_expert(be_ref, m):
    return jnp.logical_or(m == 0, be_ref[m] != be_ref[jnp.maximum(m - 1, 0)])


CAST_ROWS = 256


def _gmm1_kernel(be_ref, bv_ref, x_ref, w_ref, b_ref, o_ref, wb_ref):
    m = pl.program_id(1)
    valid = bv_ref[m] == 1

    @pl.when(jnp.logical_and(valid, _new_expert(be_ref, m)))
    def _():
        def body(r, c):
            r0 = pl.multiple_of(r * CAST_ROWS, CAST_ROWS)
            wb_ref[pl.ds(r0, CAST_ROWS), :] = w_ref[pl.ds(r0, CAST_ROWS), :].astype(BF16)
            return c
        lax.fori_loop(0, w_ref.shape[0] // CAST_ROWS, body, 0)

    @pl.when(valid)
    def _():
        even = lax.broadcasted_iota(jnp.int32, (x_ref.shape[0], LANES), 1) % 2 == 0
        lo, hi = _unpack_halves(x_ref[...])
        x = jnp.concatenate([lo.astype(BF16), hi.astype(BF16)], axis=1)

        def act_block(hblk):
            x_glu = jnp.minimum(hblk, SWIGLU_LIMIT)
            x_lin = jnp.clip(hblk, -SWIGLU_LIMIT, SWIGLU_LIMIT) + 1.0
            return x_glu * _sigmoid(SWIGLU_ALPHA * x_glu) * pltpu.roll(x_lin, LANES - 1, 1)

        for i in range(o_ref.shape[1] // LANES):
            sl = slice(2 * i * LANES, 2 * (i + 1) * LANES)
            hs = jnp.dot(x, wb_ref[:, sl], preferred_element_type=F32) + b_ref[:, sl]
            packed = jnp.where(even, act_block(hs[:, :LANES]), pltpu.roll(act_block(hs[:, LANES:]), 1, 1))
            o_ref[:, i * LANES:(i + 1) * LANES] = packed.astype(o_ref.dtype)

    @pl.when(bv_ref[m] == 0)
    def _():
        o_ref[...] = jnp.zeros_like(o_ref)


def _gmm1(block_expert, block_valid, x_buf, w1, b1, bm, tn):
    R, half = x_buf.shape
    E, D, F2 = w1.shape
    nb = R // bm
    assert D == 2 * half and D % CAST_ROWS == 0 and tn % (2 * LANES) == 0
    grid_spec = pltpu.PrefetchScalarGridSpec(
        num_scalar_prefetch=2,
        grid=(F2 // tn, nb),
        in_specs=[
            pl.BlockSpec((bm, half), lambda n, m, be, bv: (m, 0)),
            pl.BlockSpec((None, D, tn), lambda n, m, be, bv: (be[m], 0, n)),
            pl.BlockSpec((None, 1, tn), lambda n, m, be, bv: (be[m], 0, n)),
        ],
        out_specs=pl.BlockSpec((bm, tn // 2), lambda n, m, be, bv: (m, n)),
        scratch_shapes=[pltpu.VMEM((D, tn), BF16)],
    )
    return pl.pallas_call(
        _gmm1_kernel,
        grid_spec=grid_spec,
        out_shape=jax.ShapeDtypeStruct((R, F2 // 2), BF16),
        compiler_params=_cparams(("arbitrary", "arbitrary")),
        name="gmm1",
    )(block_expert, block_valid, x_buf, w1, b1)


def _gmm2_kernel(be_ref, bv_ref, a_ref, w_ref, b_ref, o_ref, wb_ref):
    m = pl.program_id(1)
    valid = bv_ref[m] == 1

    @pl.when(jnp.logical_and(valid, _new_expert(be_ref, m)))
    def _():
        half = LANES // 2

        def body(i, c):
            r0 = pl.multiple_of(i * LANES, LANES)
            words = _pack_pair(w_ref[pl.ds(r0, half), :], w_ref[pl.ds(r0 + half, half), :])
            wb_ref[pl.ds(r0, LANES), :] = pltpu.bitcast(words, BF16)
            return c
        lax.fori_loop(0, w_ref.shape[0] // LANES, body, 0)

    @pl.when(valid)
    def _():
        y = jnp.dot(a_ref[...], wb_ref[...], preferred_element_type=F32) + b_ref[...]
        o_ref[...] = _pack_halves(y)

    @pl.when(bv_ref[m] == 0)
    def _():
        o_ref[...] = jnp.zeros_like(o_ref)


def _gmm2(block_expert, block_valid, act, w2, b2, bm, tn):
    R, F = act.shape
    E, _, D = w2.shape
    nb = R // bm
    assert tn % (2 * LANES) == 0
    grid_spec = pltpu.PrefetchScalarGridSpec(
        num_scalar_prefetch=2,
        grid=(D // tn, nb),
        in_specs=[
            pl.BlockSpec((bm, F), lambda n, m, be, bv: (m, 0)),
            pl.BlockSpec((None, F, tn), lambda n, m, be, bv: (be[m], 0, n)),
            pl.BlockSpec((None, 1, tn), lambda n, m, be, bv: (be[m], 0, n)),
        ],
        out_specs=pl.BlockSpec((bm, tn // 2), lambda n, m, be, bv: (m, n)),
        scratch_shapes=[pltpu.VMEM((F, tn), BF16)],
    )
    return pl.pallas_call(
        _gmm2_kernel,
        grid_spec=grid_spec,
        out_shape=jax.ShapeDtypeStruct((R, D // 2), jnp.uint32),
        compiler_params=_cparams(("arbitrary", "arbitrary")),
        name="gmm2",
    )(block_expert, block_valid, act, w2, b2)


def _combine_kernel(dcur_ref, dnext_ref, gate_ref, x1_ref, nw_ref, y_hbm, o_ref, ybuf, sem, *, tm, y_tile):
    i = pl.program_id(0)
    n = pl.num_programs(0)
    slot = lax.rem(i, 2)

    def copy(d_ref, s, r, k):
        return pltpu.make_async_copy(y_hbm.at[pl.ds(d_ref[0, 0, r * TOP_K + k], 1)],
                                     ybuf.at[s, k, pl.ds(r, 1)], sem.at[s])

    def issue(d_ref, s):
        def body(g, c):
            for u in range(ISSUE_UNROLL // TOP_K):
                for k in range(TOP_K):
                    copy(d_ref, s, g * (ISSUE_UNROLL // TOP_K) + u, k).start()
            return c
        lax.fori_loop(0, tm // (ISSUE_UNROLL // TOP_K), body, 0)

    @pl.when(i == 0)
    def _():
        issue(dcur_ref, 0)

    @pl.when(i + 1 < n)
    def _():
        issue(dnext_ref, 1 - slot)

    for k in range(TOP_K):
        pltpu.make_async_copy(y_hbm.at[pl.ds(0, tm)], ybuf.at[slot, k], sem.at[slot]).wait()

    lane = lax.broadcasted_iota(jnp.int32, (tm, LANES), 1)
    gates = gate_ref[...]
    acc = x1_ref[...]
    hw = y_tile // 2
    for k in range(TOP_K):
        gk = jnp.sum(jnp.where(lane == k, gates, 0.0), axis=-1, keepdims=True)
        parts = []
        for jt in range(ybuf.shape[-1] // hw):
            lo, hi = _unpack_halves(ybuf[slot, k, :, jt * hw:(jt + 1) * hw])
            parts += [lo, hi]
        acc = acc + gk * jnp.concatenate(parts, axis=1)
    out = acc * lax.rsqrt(jnp.mean(acc * acc, axis=-1, keepdims=True) + RMS_EPS) * nw_ref[...]
    o_ref[...] = out


def _combine(dest, gates, x1, norm_f_w, y_buf, tm, y_tile):
    T, D = x1.shape
    nt = T // tm
    d3 = dest.reshape(nt, 1, tm * TOP_K)
    return pl.pallas_call(
        functools.partial(_combine_kernel, tm=tm, y_tile=y_tile),
        grid=(nt,),
        in_specs=[
            pl.BlockSpec((1, 1, tm * TOP_K), lambda i: (i, 0, 0), memory_space=pltpu.SMEM),
            pl.BlockSpec((1, 1, tm * TOP_K), lambda i: (jnp.minimum(i + 1, nt - 1), 0, 0),
                         memory_space=pltpu.SMEM),
            pl.BlockSpec((tm, LANES), lambda i: (i, 0)),
            pl.BlockSpec((tm, D), lambda i: (i, 0)),
            pl.BlockSpec((1, D), lambda i: (0, 0)),
            pl.BlockSpec(memory_space=pl.ANY),
        ],
        out_specs=pl.BlockSpec((tm, D), lambda i: (i, 0)),
        out_shape=jax.ShapeDtypeStruct((T, D), F32),
        scratch_shapes=[pltpu.VMEM((2, TOP_K, tm, D // 2), jnp.uint32), pltpu.SemaphoreType.DMA((2,))],
        compiler_params=_cparams(("arbitrary",)),
        name="combine",
    )(d3, d3, gates, x1, norm_f_w.reshape(1, D).astype(F32), y_buf)


def _routing(top_idx, n_experts, bm):
    T = top_idx.shape[0]
    A = T * TOP_K
    flat_e = top_idx.reshape(A)
    onehot = (flat_e[:, None] == jnp.arange(n_experts, dtype=jnp.int32)[None, :]).astype(jnp.int32)
    csum = jnp.cumsum(onehot, axis=0)
    rank = jnp.sum((csum - onehot) * onehot, axis=1)
    counts = csum[-1]
    padded = (counts + bm - 1) // bm * bm
    pend = jnp.cumsum(padded)
    pstart = pend - padded
    dest = jnp.sum(onehot * pstart[None, :], axis=1) + rank
    R = (A + bm - 1) // bm * bm + n_experts * bm
    nb = R // bm
    block_start = jnp.arange(nb, dtype=jnp.int32) * bm
    block_expert = jnp.minimum(jnp.sum((block_start[:, None] >= pend[None, :]).astype(jnp.int32), axis=1),
                               n_experts - 1)
    block_valid = (block_start < pend[-1]).astype(jnp.int32)
    is_last = jnp.any((block_start[:, None] + bm == pend[None, :]) & (padded[None, :] > 0), axis=1)
    zero_fill = (is_last | (block_start >= pend[-1])).astype(jnp.int32)
    return dest.astype(jnp.int32), R, block_expert, block_valid, zero_fill


def _pick(pref, n):
    t = min(pref, n)
    while n % t:
        t //= 2
    return t


def kernel(x, norm1_w, w_in, b_gate, conv_w, A_log, dt_bias, onorm_w, w_branch_a, w_pool,
           pool_scale, w_branch_p, w_out, norm2_w, router_w, router_b, w1, b1, w2, b2, norm_f_w):
    B, S, D = x.shape
    T = B * S
    depth = norm1_w.shape[0]
    x2 = x.reshape(T, D)
    for l in range(depth):
        conv_ch = conv_w.shape[-1]
        v_width = w_branch_a.shape[1]
        qk_width = (conv_ch - v_width) // 2
        n_v = A_log.shape[-1]
        qkvz = conv_ch + v_width
        ab_lo, ab_hi = qkvz, qkvz + 2 * n_v
        E = router_w.shape[-1]

        wl = w_in[l]
        w_main = jnp.concatenate([wl[:, :qkvz], wl[:, ab_hi:]], axis=1).astype(BF16)
        w_ab = jnp.zeros((D, LANES), BF16).at[:, :2 * n_v].set(wl[:, ab_lo:ab_hi].astype(BF16))
        p_off, ga_off, gp_off = qkvz, qkvz + D, qkvz + 2 * D

        tn_in = _pick(1024, math.gcd(qk_width, v_width, w_main.shape[1]))
        tm_in = _pick(1024, S // 2 if S >= 2 * CONV_HALO else S)
        proj, ab = _in_proj(x2, norm1_w[l], w_main, w_ab, conv_w[l], qk_width, v_width, S, tm_in, tn_in)
        proj3 = proj.reshape(B, S, -1)
        G = 8 if (qk_width // HEAD) % 8 == 0 else 2
        o_a = _delta(proj3, ab.reshape(B, S, LANES), A_log[l], dt_bias[l], onorm_w[l],
                     qk_width, v_width, G, _pick(512, S))
        o_p = _pool(proj3, w_pool[l], pool_scale[l], p_off, D)
        h = _branch(o_a.reshape(T, v_width), o_p.reshape(T, D), proj,
                    w_branch_a[l].astype(BF16), w_branch_p[l].astype(BF16), b_gate[l],
                    ga_off, gp_off, _pick(1024, T), _pick(512, D))
        x1, xp, idx_pad, gate_pad = _post(x2, h, w_out[l].astype(BF16), norm2_w[l],
                                          router_w[l], router_b[l], _pick(512, T))

        bm = 512
        y_tile = _pick(1024, D)
        dest, n_rows, block_expert, block_valid, zero_fill = _routing(idx_pad[:, :TOP_K], E, bm)
        x_buf = _dispatch_rows(xp, dest, zero_fill, n_rows, _pick(512, T), bm)
        act = _gmm1(block_expert, block_valid, x_buf, w1[l].astype(F32),
                    b1[l][:, None, :].astype(F32), bm, _pick(2048, w1.shape[-1]))
        y_buf = _gmm2(block_expert, block_valid, act, w2[l].astype(F32),
                      b2[l][:, None, :].astype(F32), bm, y_tile)
        is_last = l == depth - 1
        assert is_last, "final norm is fused into the last layer's combine"
        x2 = _combine(dest, gate_pad, x1, norm_f_w, y_buf, _pick(128, T), y_tile)
    return x2.reshape(B, S, D)
```

```python
import functools
import math

import jax
import jax.numpy as jnp
from jax import lax
from jax.experimental import pallas as pl
from jax.experimental.pallas import tpu as pltpu

F32 = jnp.float32
BF16 = jnp.bfloat16

HEAD = 128
CHUNK = 64
CONV_WIDTH = 4
POOL_WINDOWS = (2, 4, 8, 16)
TOP_K = 4
SWIGLU_ALPHA = 1.702
SWIGLU_LIMIT = 7.0
RMS_EPS = 1e-6
L2_EPS = 1e-6
LANES = 128
HIGHEST = lax.Precision.HIGHEST
VMEM_LIMIT = 56 * 1024 * 1024


def _cparams(sem):
    return pltpu.CompilerParams(dimension_semantics=sem, vmem_limit_bytes=VMEM_LIMIT)


def _sigmoid(x):
    return 1.0 / (1.0 + jnp.exp(-x))


def _bdot(a, b):
    return jnp.dot(a.astype(BF16), b.astype(BF16), preferred_element_type=F32)


CONV_HALO = 8


def _inproj_kernel(x_ref, nw_ref, w_ref, w2_ref, wab_ref, cw_ref, o_ref, ab_ref, xn_ref, halo_ref, acc_ref, *,
                   rows, n_q, n_k, n_conv, n_first, tiles_per_seq):
    i = pl.program_id(0)
    j = pl.program_id(1)
    tm, tn = o_ref.shape

    @pl.when(j == 0)
    def _():
        def body(r, c):
            r0 = pl.multiple_of(r * rows, rows)
            x = x_ref[pl.ds(r0, rows), :]
            inv = lax.rsqrt(jnp.mean(x * x, axis=-1, keepdims=True) + RMS_EPS)
            xn_ref[pl.ds(r0, rows), :] = (x * inv * nw_ref[...]).astype(BF16)
            return c
        lax.fori_loop(0, tm // rows, body, 0)
        ab_ref[...] = jnp.dot(xn_ref[...], wab_ref[...], preferred_element_type=F32)

    @pl.when(jnp.logical_and(j >= n_conv, j < n_first))
    def _():
        o_ref[...] = jnp.dot(xn_ref[...], w_ref[...], preferred_element_type=F32).astype(o_ref.dtype)

    @pl.when(j >= n_first)
    def _():
        o_ref[...] = jnp.dot(xn_ref[...], w2_ref[...], preferred_element_type=F32).astype(o_ref.dtype)

    def conv_tile(normalise):
        jc = jnp.minimum(j, n_conv - 1)
        first = lax.rem(i, tiles_per_seq) == 0
        qscale = jnp.where(j < n_q, HEAD ** -0.5, 1.0).astype(F32)
        acc_ref[:CONV_HALO, :] = jnp.where(first, 0.0, halo_ref[jc])
        acc_ref[CONV_HALO:, :] = jnp.dot(xn_ref[...], w_ref[...], preferred_element_type=F32)
        halo_ref[jc] = acc_ref[tm:, :]
        for c in range(tn // LANES):
            sl = slice(c * LANES, (c + 1) * LANES)
            y = acc_ref[CONV_HALO:, sl] * cw_ref[CONV_WIDTH - 1:CONV_WIDTH, sl]
            for t in range(CONV_WIDTH - 1):
                sh = CONV_WIDTH - 1 - t
                y = y + acc_ref[CONV_HALO - sh:CONV_HALO - sh + tm, sl] * cw_ref[t:t + 1, sl]
            y = y * _sigmoid(y)
            if normalise:
                y = y * (lax.rsqrt(jnp.sum(y * y, axis=-1, keepdims=True) + L2_EPS) * qscale)
            o_ref[:, sl] = y.astype(o_ref.dtype)

    @pl.when(j < n_q + n_k)
    def _():
        conv_tile(True)

    @pl.when(jnp.logical_and(j >= n_q + n_k, j < n_conv))
    def _():
        conv_tile(False)


def _in_proj(x2, norm_w, w_first, w_second, w_ab, conv_w, qk_width, v_width, seq_len, tm, tn):
    T, D = x2.shape
    N = w_first.shape[1] + w_second.shape[1]
    rows = min(128, tm)
    assert qk_width % tn == 0 and v_width % tn == 0 and tn % HEAD == 0 and seq_len % tm == 0
    assert CONV_WIDTH - 1 <= CONV_HALO <= tm and w_first.shape[1] % tn == 0
    n_q = n_k = qk_width // tn
    n_conv = n_q + n_k + v_width // tn
    n_first = w_first.shape[1] // tn
    return pl.pallas_call(
        functools.partial(_inproj_kernel, rows=rows, n_q=n_q, n_k=n_k, n_conv=n_conv, n_first=n_first,
                          tiles_per_seq=seq_len // tm),
        grid=(T // tm, N // tn),
        in_specs=[
            pl.BlockSpec((tm, D), lambda i, j: (i, 0)),
            pl.BlockSpec((1, D), lambda i, j: (0, 0)),
            pl.BlockSpec((D, tn), lambda i, j: (0, jnp.minimum(j, n_first - 1))),
            pl.BlockSpec((D, tn), lambda i, j: (0, jnp.maximum(j - n_first, 0))),
            pl.BlockSpec((D, LANES), lambda i, j: (0, 0)),
            pl.BlockSpec((CONV_WIDTH, tn), lambda i, j: (0, jnp.minimum(j, n_conv - 1))),
        ],
        out_specs=[
            pl.BlockSpec((tm, tn), lambda i, j: (i, j)),
            pl.BlockSpec((tm, LANES), lambda i, j: (i, 0)),
        ],
        out_shape=[jax.ShapeDtypeStruct((T, N), BF16), jax.ShapeDtypeStruct((T, LANES), F32)],
        scratch_shapes=[pltpu.VMEM((tm, D), BF16), pltpu.VMEM((n_conv, CONV_HALO, tn), F32),
                        pltpu.VMEM((CONV_HALO + tm, tn), F32)],
        compiler_params=_cparams(("arbitrary", "arbitrary")),
        name="in_proj",
    )(x2, norm_w.reshape(1, D), w_first, w_second, w_ab, conv_w.astype(F32))


def _delta_kernel(q_ref, k_ref, v_ref, z_ref, ab_ref, alog_ref, dtb_ref, onw_ref, o_ref,
                  s_ref, t_ref, a_ref, gc_ref, vl_ref, *, G, H, unroll_a):
    St = q_ref.shape[0]
    C = CHUNK
    WIDE = 4 * C
    NW = G // 2
    t_idx = pl.program_id(2)
    j = pl.program_id(1)

    @pl.when(t_idx == 0)
    def _():
        s_ref[...] = jnp.zeros_like(s_ref)

    row = lax.broadcasted_iota(jnp.int32, (C, WIDE), 0)
    lanew = lax.broadcasted_iota(jnp.int32, (C, WIDE), 1)
    colw = lanew % C
    blk = lanew // C
    incl_w = row >= colw
    strict_w = row > colw
    eye_w = jnp.where(row == colw, 1.0, 0.0).astype(F32)
    first_half = (lax.broadcasted_iota(jnp.int32, (C, 2 * C), 1) < C)
    r2 = lax.broadcasted_iota(jnp.int32, (C, C), 0)
    c2 = lax.broadcasted_iota(jnp.int32, (C, C), 1)
    tril = jnp.where(r2 >= c2, 1.0, 0.0).astype(F32)
    lane = lax.broadcasted_iota(jnp.int32, (C, LANES), 1)
    shift = lax.rem(LANES - 2 * G * j, LANES)

    def pair_cols(x, c0):
        a = jnp.broadcast_to(x[:, c0:c0 + 1], (C, 2 * C))
        b = jnp.broadcast_to(x[:, c0 + 1:c0 + 2], (C, 2 * C))
        return jnp.where(first_half, a, b)

    blk_mask = [jnp.where(blk == e, 1.0, 0.0).astype(BF16) for e in range(4)]

    def block_diag(mw):
        return jnp.concatenate([mw * blk_mask[e] for e in range(4)], axis=0)

    def prep(i, carry):
        chunks = [i * unroll_a + u for u in range(unroll_a)]
        r0s = [pl.multiple_of(c * C, C) for c in chunks]
        gcs, valss = [], []
        for r0 in r0s:
            ab = ab_ref[pl.ds(r0, C), :]
            sp_in = ab + dtb_ref[...]
            softplus = jnp.maximum(sp_in, 0.0) + jnp.log(1.0 + jnp.exp(-jnp.abs(sp_in)))
            gval = -jnp.exp(alog_ref[...]) * softplus
            vals = pltpu.roll(jnp.where(lane < H, gval, _sigmoid(ab)), shift, 1)
            gc = jnp.dot(tril, vals, preferred_element_type=F32, precision=HIGHEST)
            gc_ref[pl.ds(r0, C), :] = gc
            vl_ref[pl.ds(r0, C), :] = vals
            gcs.append(gc)
            valss.append(vals)
        qbs = [q_ref[pl.ds(r0, C), h * HEAD:(h + 1) * HEAD] for r0 in r0s for h in range(G)]
        kbs = [k_ref[pl.ds(r0, C), h * HEAD:(h + 1) * HEAD] for r0 in r0s for h in range(G)]
        qkk = [lax.dot_general(jnp.concatenate([qb, kb], axis=0), jnp.concatenate([kb, kb], axis=0),
                               (((1,), (1,)), ((), ())), preferred_element_type=F32)
               for qb, kb in zip(qbs, kbs)]
        ps, ms = [], []
        for u, r0 in enumerate(r0s):
            for w in range(NW):
                top, bot = qkk[u * G + 2 * w], qkk[u * G + 2 * w + 1]
                qk_w = jnp.concatenate([top[:C], bot[:C]], axis=1)
                kk_w = jnp.concatenate([top[C:], bot[C:]], axis=1)
                gcol = jnp.concatenate([pair_cols(gcs[u], 4 * w), pair_cols(gcs[u], 4 * w + 2)], axis=1)
                bcol = jnp.concatenate([pair_cols(valss[u], H + 4 * w),
                                        pair_cols(valss[u], H + 4 * w + 2)], axis=1)
                grow = jnp.sum(gcol * eye_w, axis=0, keepdims=True)
                dec = jnp.exp(jnp.where(incl_w, gcol - grow, -jnp.inf))
                lmat = jnp.where(strict_w, kk_w * dec * bcol, 0.0)
                a_ref[pl.ds(r0, C), w * WIDE:(w + 1) * WIDE] = (qk_w * dec).astype(BF16)
                ps.append(eye_w - lmat)
                ms.append(lmat)
        ms = [_bdot(m, block_diag(m.astype(BF16))) for m in ms]
        for _ in range(int(math.log2(C)) - 2):
            outs = [jnp.dot(jnp.concatenate([p, m], axis=0).astype(BF16), block_diag(m.astype(BF16)),
                            preferred_element_type=F32) for p, m in zip(ps, ms)]
            ps = [p + o[:C] for p, o in zip(ps, outs)]
            ms = [o[C:] for o in outs]
        outs = [_bdot(p, block_diag(m.astype(BF16))) for p, m in zip(ps, ms)]
        for u, r0 in enumerate(r0s):
            for w in range(NW):
                t_ref[pl.ds(r0, C), w * WIDE:(w + 1) * WIDE] = (ps[u * NW + w] + outs[u * NW + w]).astype(BF16)
        return carry

    lax.fori_loop(0, St // (C * unroll_a), prep, 0)

    def scan_chunk(c, carry):
        r0 = pl.multiple_of(c * C, C)
        gc = gc_ref[pl.ds(r0, C), :]
        vals = vl_ref[pl.ds(r0, C), :]
        kbs = [k_ref[pl.ds(r0, C), h * HEAD:(h + 1) * HEAD] for h in range(G)]
        ks = [jnp.dot(jnp.concatenate([q_ref[pl.ds(r0, C), h * HEAD:(h + 1) * HEAD], kbs[h]], axis=0),
                      jnp.concatenate([s_ref[2 * h], s_ref[2 * h + 1]], axis=1).astype(BF16),
                      preferred_element_type=F32) for h in range(G)]
        egs, gcols, rhss = [], [], []
        for hv in range(2 * G):
            h, e = divmod(hv, 2)
            gcol = jnp.broadcast_to(gc[:, hv:hv + 1], (C, LANES))
            bcol = jnp.broadcast_to(vals[:, H + hv:H + hv + 1], (C, LANES))
            eg = jnp.exp(gcol)
            v = v_ref[pl.ds(r0, C), hv * HEAD:(hv + 1) * HEAD].astype(F32)
            rhss.append((bcol * (v - eg * ks[h][C:, e * HEAD:(e + 1) * HEAD])).astype(BF16))
            egs.append(eg)
            gcols.append(gcol)
        vnews = [jnp.dot(t_ref[pl.ds(r0, C), hv * C:(hv + 1) * C], rhss[hv],
                         preferred_element_type=F32) for hv in range(2 * G)]
        intra = [jnp.dot(a_ref[pl.ds(r0, C), hv * C:(hv + 1) * C], vnews[hv].astype(BF16),
                         preferred_element_type=F32) for hv in range(2 * G)]
        vnd = [(jnp.exp(gcols[hv][C - 1:C, :] - gcols[hv]) * vnews[hv]).astype(BF16)
               for hv in range(2 * G)]
        upd = [lax.dot_general(kbs[h], jnp.concatenate([vnd[2 * h], vnd[2 * h + 1]], axis=1),
                               (((0,), (0,)), ((), ())), preferred_element_type=F32)
               for h in range(G)]
        for hv in range(2 * G):
            h, e = divmod(hv, 2)
            s_ref[hv] = (s_ref[hv] * jnp.exp(gcols[hv][C - 1:C, :])
                         + upd[h][:, e * HEAD:(e + 1) * HEAD])
            o = egs[hv] * ks[h][:C, e * HEAD:(e + 1) * HEAD] + intra[hv]
            zz = z_ref[pl.ds(r0, C), hv * HEAD:(hv + 1) * HEAD].astype(F32)
            on = o * lax.rsqrt(jnp.mean(o * o, axis=-1, keepdims=True) + RMS_EPS)
            on = on * onw_ref[...] * (zz * _sigmoid(zz))
            o_ref[pl.ds(r0, C), hv * HEAD:(hv + 1) * HEAD] = on.astype(o_ref.dtype)
        return carry

    lax.fori_loop(0, St // C, scan_chunk, 0)


def _delta(proj3, ab3, a_log, dt_bias, onorm_w, qk_width, v_width, G, St):
    B, S, _ = proj3.shape
    H = a_log.shape[0]
    n_groups = qk_width // (HEAD * G)
    qw, vw = HEAD * G, 2 * HEAD * G
    k_blk0 = qk_width // qw
    v_blk0 = (2 * qk_width) // vw
    z_blk0 = (2 * qk_width + v_width) // vw
    assert (2 * qk_width) % vw == 0 and (2 * qk_width + v_width) % vw == 0
    assert 2 * H <= LANES and S % CHUNK == 0
    assert G % 2 == 0 and S % St == 0 and St % CHUNK == 0
    pad = lambda p: jnp.zeros((1, LANES), F32).at[0, :H].set(p.astype(F32))
    return pl.pallas_call(
        functools.partial(_delta_kernel, G=G, H=H, unroll_a=_pick(4, St // CHUNK)),
        grid=(B, n_groups, S // St),
        in_specs=[
            pl.BlockSpec((None, St, qw), lambda b, j, t: (b, t, j)),
            pl.BlockSpec((None, St, qw), lambda b, j, t: (b, t, k_blk0 + j)),
            pl.BlockSpec((None, St, vw), lambda b, j, t: (b, t, v_blk0 + j)),
            pl.BlockSpec((None, St, vw), lambda b, j, t: (b, t, z_blk0 + j)),
            pl.BlockSpec((None, St, LANES), lambda b, j, t: (b, t, 0)),
            pl.BlockSpec((1, LANES), lambda b, j, t: (0, 0)),
            pl.BlockSpec((1, LANES), lambda b, j, t: (0, 0)),
            pl.BlockSpec((1, HEAD), lambda b, j, t: (0, 0)),
        ],
        out_specs=pl.BlockSpec((None, St, vw), lambda b, j, t: (b, t, j)),
        out_shape=jax.ShapeDtypeStruct((B, S, v_width), BF16),
        scratch_shapes=[
            pltpu.VMEM((2 * G, HEAD, HEAD), F32),
            pltpu.VMEM((St, qw), BF16),
            pltpu.VMEM((St, qw), BF16),
            pltpu.VMEM((St, LANES), F32),
            pltpu.VMEM((St, LANES), F32),
        ],
        compiler_params=_cparams(("arbitrary", "arbitrary", "arbitrary")),
        name="delta",
    )(proj3, proj3, proj3, proj3, ab3, pad(a_log), pad(dt_bias), onorm_w.reshape(1, HEAD).astype(F32))


POOL_HALO = 16


def _pool_kernel(p_ref, w_ref, sc_ref, o_ref, pad_ref, *, rt):
    S = p_ref.shape[0]
    g = pl.program_id(1)
    pad_ref[0:POOL_HALO, :] = jnp.zeros((POOL_HALO, pad_ref.shape[1]), F32)
    pad_ref[POOL_HALO:, :] = p_ref[...].astype(F32)
    win = jnp.left_shift(2, g).astype(F32)

    def body(r, c):
        r0 = pl.multiple_of(r * rt, rt)
        x = pad_ref[pl.ds(r0, rt + POOL_HALO), :]
        y1 = x + pltpu.roll(x, 1, 0)
        y2 = y1 + pltpu.roll(y1, 2, 0)
        y3 = y2 + pltpu.roll(y2, 4, 0)
        y4 = y3 + pltpu.roll(y3, 8, 0)
        ysel = jnp.where(g == 0, y1, jnp.where(g == 1, y2, jnp.where(g == 2, y3, y4)))
        pos = (r0 + 1 + lax.broadcasted_iota(jnp.int32, (rt, 1), 0)).astype(F32)
        cnt = jnp.minimum(pos, win)
        mixed = ysel[POOL_HALO:] / cnt - x[POOL_HALO:]
        out = _bdot(mixed, w_ref[...]) * sc_ref[...]
        o_ref[pl.ds(r0, rt), :] = out.astype(o_ref.dtype)
        return c

    lax.fori_loop(0, S // rt, body, 0)


def _pool(proj3, w_pool, pool_scale, p_off, d_model):
    B, S, _ = proj3.shape
    ng, cg, _ = w_pool.shape
    assert POOL_WINDOWS == tuple(2 << i for i in range(ng)) and max(POOL_WINDOWS) <= POOL_HALO
    assert p_off % cg == 0
    blk0 = p_off // cg
    rt = min(256, S)
    return pl.pallas_call(
        functools.partial(_pool_kernel, rt=rt),
        grid=(B, ng),
        in_specs=[
            pl.BlockSpec((None, S, cg), lambda b, g: (b, 0, blk0 + g)),
            pl.BlockSpec((None, cg, cg), lambda b, g: (g, 0, 0)),
            pl.BlockSpec((1, cg), lambda b, g: (0, g)),
        ],
        out_specs=pl.BlockSpec((None, S, cg), lambda b, g: (b, 0, g)),
        out_shape=jax.ShapeDtypeStruct((B, S, d_model), BF16),
        scratch_shapes=[pltpu.VMEM((S + POOL_HALO, cg), F32)],
        compiler_params=_cparams(("arbitrary", "arbitrary")),
        name="pool",
    )(proj3, w_pool.astype(BF16), pool_scale.reshape(1, -1).astype(F32))


def _branch_kernel(oa_ref, op_ref, wa_ref, wp_ref, ga_ref, gp_ref, bga_ref, bgp_ref, h_ref):
    a = jnp.dot(oa_ref[...], wa_ref[...], preferred_element_type=F32)
    p = jnp.dot(op_ref[...], wp_ref[...], preferred_element_type=F32)
    ga = _sigmoid(ga_ref[...].astype(F32) + bga_ref[...])
    gp = _sigmoid(gp_ref[...].astype(F32) + bgp_ref[...])
    h_ref[...] = (ga * a + gp * p).astype(h_ref.dtype)


def _branch(o_a, o_p, proj, w_a, w_p, b_gate, ga_off, gp_off, tm, tn):
    T, V = o_a.shape
    D = w_a.shape[1]
    assert ga_off % tn == 0 and gp_off % tn == 0
    ga0, gp0 = ga_off // tn, gp_off // tn
    nd = D // tn
    bg = b_gate.reshape(1, 2 * D).astype(F32)
    return pl.pallas_call(
        _branch_kernel,
        grid=(T // tm, nd),
        in_specs=[
            pl.BlockSpec((tm, V), lambda i, j: (i, 0)),
            pl.BlockSpec((tm, D), lambda i, j: (i, 0)),
            pl.BlockSpec((V, tn), lambda i, j: (0, j)),
            pl.BlockSpec((D, tn), lambda i, j: (0, j)),
            pl.BlockSpec((tm, tn), lambda i, j: (i, ga0 + j)),
            pl.BlockSpec((tm, tn), lambda i, j: (i, gp0 + j)),
            pl.BlockSpec((1, tn), lambda i, j: (0, j)),
            pl.BlockSpec((1, tn), lambda i, j: (0, nd + j)),
        ],
        out_specs=pl.BlockSpec((tm, tn), lambda i, j: (i, j)),
        out_shape=jax.ShapeDtypeStruct((T, D), BF16),
        compiler_params=_cparams(("arbitrary", "arbitrary")),
        name="branch",
    )(o_a, o_p, w_a, w_p, proj, proj, bg, bg)


def _pack_pair(lo, hi):
    lo = lax.bitcast_convert_type(lo.astype(BF16).astype(F32), jnp.uint32)
    hi = lax.bitcast_convert_type(hi.astype(BF16).astype(F32), jnp.uint32)
    return jnp.right_shift(lo, jnp.uint32(16)) | (hi & jnp.uint32(0xFFFF0000))


def _pack_halves(x):
    half = x.shape[1] // 2
    return _pack_pair(x[:, :half], x[:, half:])


def _unpack_halves(u):
    lo = lax.bitcast_convert_type(jnp.left_shift(u, jnp.uint32(16)), F32)
    hi = lax.bitcast_convert_type(u & jnp.uint32(0xFFFF0000), F32)
    return lo, hi


def _split_bf16(x):
    hi = x.astype(BF16)
    return hi, (x - hi.astype(F32)).astype(BF16)


def _post_kernel(x_ref, h_ref, wo_ref, nw_ref, rwh_ref, rwl_ref, rb_ref,
                 x1_ref, xp_ref, idx_ref, gate_ref, *, n_experts):
    x1 = x_ref[...] + jnp.dot(h_ref[...], wo_ref[...], preferred_element_type=F32)
    x1_ref[...] = x1
    xn = x1 * lax.rsqrt(jnp.mean(x1 * x1, axis=-1, keepdims=True) + RMS_EPS) * nw_ref[...]
    xp_ref[...] = _pack_halves(xn)
    xh, xl = _split_bf16(xn)
    logits = (jnp.dot(xh, rwh_ref[...], preferred_element_type=F32)
              + jnp.dot(xh, rwl_ref[...], preferred_element_type=F32)
              + jnp.dot(xl, rwh_ref[...], preferred_element_type=F32) + rb_ref[...])
    lane = lax.broadcasted_iota(jnp.int32, logits.shape, 1)
    work = jnp.where(lane < n_experts, logits, -jnp.inf)
    vals, idxs = [], []
    for _ in range(TOP_K):
        m = jnp.max(work, axis=-1, keepdims=True)
        idx = jnp.min(jnp.where(work == m, lane, LANES), axis=-1, keepdims=True)
        vals.append(m)
        idxs.append(idx)
        work = jnp.where(lane == idx, -jnp.inf, work)
    ex = [jnp.exp(v - vals[0]) for v in vals]
    den = ex[0]
    for e in ex[1:]:
        den = den + e
    idx_out = jnp.zeros(logits.shape, jnp.int32)
    gate_out = jnp.zeros(logits.shape, F32)
    for k in range(TOP_K):
        idx_out = jnp.where(lane == k, idxs[k], idx_out)
        gate_out = jnp.where(lane == k, ex[k] / den, gate_out)
    idx_ref[...] = idx_out
    gate_ref[...] = gate_out


def _post(x2, h, w_out, norm2_w, router_w, router_b, tm):
    T, D = x2.shape
    E = router_w.shape[1]
    rw = jnp.zeros((D, LANES), F32).at[:, :E].set(router_w.astype(F32))
    rwh, rwl = _split_bf16(rw)
    rb = jnp.zeros((1, LANES), F32).at[0, :E].set(router_b.astype(F32))
    return pl.pallas_call(
        functools.partial(_post_kernel, n_experts=E),
        grid=(T // tm,),
        in_specs=[
            pl.BlockSpec((tm, D), lambda i: (i, 0)),
            pl.BlockSpec((tm, D), lambda i: (i, 0)),
            pl.BlockSpec((D, D), lambda i: (0, 0)),
            pl.BlockSpec((1, D), lambda i: (0, 0)),
            pl.BlockSpec((D, LANES), lambda i: (0, 0)),
            pl.BlockSpec((D, LANES), lambda i: (0, 0)),
            pl.BlockSpec((1, LANES), lambda i: (0, 0)),
        ],
        out_specs=[
            pl.BlockSpec((tm, D), lambda i: (i, 0)),
            pl.BlockSpec((tm, D // 2), lambda i: (i, 0)),
            pl.BlockSpec((tm, LANES), lambda i: (i, 0)),
            pl.BlockSpec((tm, LANES), lambda i: (i, 0)),
        ],
        out_shape=[
            jax.ShapeDtypeStruct((T, D), F32),
            jax.ShapeDtypeStruct((T, D // 2), jnp.uint32),
            jax.ShapeDtypeStruct((T, LANES), jnp.int32),
            jax.ShapeDtypeStruct((T, LANES), F32),
        ],
        compiler_params=_cparams(("arbitrary",)),
        name="post",
    )(x2, h, w_out, norm2_w.reshape(1, D).astype(F32), rwh, rwl, rb)


ISSUE_UNROLL = 8


def _dispatch_kernel(zf_ref, dest_ref, xp_ref, o_hbm, zbuf, sem, zsem, *, tm, bm):
    @pl.when(pl.program_id(0) == 0)
    def _():
        zbuf[...] = jnp.zeros_like(zbuf)
        nb = o_hbm.shape[0] // bm

        def zcopy(m):
            return pltpu.make_async_copy(zbuf, o_hbm.at[pl.ds(pl.multiple_of(m * bm, bm), bm)], zsem)

        def zstart(m, c):
            @pl.when(zf_ref[m] == 1)
            def _():
                zcopy(m).start()
            return c

        def zwait(m, c):
            @pl.when(zf_ref[m] == 1)
            def _():
                zcopy(m).wait()
            return c

        lax.fori_loop(0, nb, zstart, 0)
        lax.fori_loop(0, nb, zwait, 0)

    def body(g, c):
        for u in range(ISSUE_UNROLL // TOP_K):
            r = g * (ISSUE_UNROLL // TOP_K) + u
            for k in range(TOP_K):
                pltpu.make_async_copy(xp_ref.at[pl.ds(r, 1)],
                                      o_hbm.at[pl.ds(dest_ref[0, 0, r * TOP_K + k], 1)], sem).start()
        return c
    lax.fori_loop(0, tm // (ISSUE_UNROLL // TOP_K), body, 0)
    for _ in range(TOP_K):
        pltpu.make_async_copy(xp_ref, o_hbm.at[pl.ds(0, tm)], sem).wait()


def _dispatch_rows(xp, dest, zero_fill, n_rows, tm, bm):
    T, W = xp.shape
    nt = T // tm
    grid_spec = pltpu.PrefetchScalarGridSpec(
        num_scalar_prefetch=1,
        grid=(nt,),
        in_specs=[
            pl.BlockSpec((1, 1, tm * TOP_K), lambda i, zf: (i, 0, 0), memory_space=pltpu.SMEM),
            pl.BlockSpec((tm, W), lambda i, zf: (i, 0)),
        ],
        out_specs=pl.BlockSpec(memory_space=pl.ANY),
        scratch_shapes=[pltpu.VMEM((bm, W), xp.dtype), pltpu.SemaphoreType.DMA(()),
                        pltpu.SemaphoreType.DMA(())],
    )
    return pl.pallas_call(
        functools.partial(_dispatch_kernel, tm=tm, bm=bm),
        grid_spec=grid_spec,
        out_shape=jax.ShapeDtypeStruct((n_rows, W), xp.dtype),
        compiler_params=_cparams(("arbitrary",)),
        name="dispatch",
    )(zero_fill, dest.reshape(nt, 1, tm * TOP_K), xp)


def _new_expert(be_ref, m):
    return jnp.logical_or(m == 0, be_ref[m] != be_ref[jnp.maximum(m - 1, 0)])


CAST_ROWS = 256


def _gmm1_kernel(be_ref, bv_ref, x_ref, w_ref, b_ref, o_ref, wb_ref):
    m = pl.program_id(1)
    valid = bv_ref[m] == 1

    @pl.when(jnp.logical_and(valid, _new_expert(be_ref, m)))
    def _():
        def body(r, c):
            r0 = pl.multiple_of(r * CAST_ROWS, CAST_ROWS)
            wb_ref[pl.ds(r0, CAST_ROWS), :] = w_ref[pl.ds(r0, CAST_ROWS), :].astype(BF16)
            return c
        lax.fori_loop(0, w_ref.shape[0] // CAST_ROWS, body, 0)

    @pl.when(valid)
    def _():
        even = lax.broadcasted_iota(jnp.int32, (x_ref.shape[0], LANES), 1) % 2 == 0
        lo, hi = _unpack_halves(x_ref[...])
        x = jnp.concatenate([lo.astype(BF16), hi.astype(BF16)], axis=1)

        def act_block(hblk):
            x_glu = jnp.minimum(hblk, SWIGLU_LIMIT)
            x_lin = jnp.clip(hblk, -SWIGLU_LIMIT, SWIGLU_LIMIT) + 1.0
            return x_glu * _sigmoid(SWIGLU_ALPHA * x_glu) * pltpu.roll(x_lin, LANES - 1, 1)

        for i in range(o_ref.shape[1] // LANES):
            sl = slice(2 * i * LANES, 2 * (i + 1) * LANES)
            hs = jnp.dot(x, wb_ref[:, sl], preferred_element_type=F32) + b_ref[:, sl]
            packed = jnp.where(even, act_block(hs[:, :LANES]), pltpu.roll(act_block(hs[:, LANES:]), 1, 1))
            o_ref[:, i * LANES:(i + 1) * LANES] = packed.astype(o_ref.dtype)

    @pl.when(bv_ref[m] == 0)
    def _():
        o_ref[...] = jnp.zeros_like(o_ref)


def _gmm1(block_expert, block_valid, x_buf, w1, b1, bm, tn):
    R, half = x_buf.shape
    E, D, F2 = w1.shape
    nb = R // bm
    assert D == 2 * half and D % CAST_ROWS == 0 and tn % (2 * LANES) == 0
    grid_spec = pltpu.PrefetchScalarGridSpec(
        num_scalar_prefetch=2,
        grid=(F2 // tn, nb),
        in_specs=[
            pl.BlockSpec((bm, half), lambda n, m, be, bv: (m, 0)),
            pl.BlockSpec((None, D, tn), lambda n, m, be, bv: (be[m], 0, n)),
            pl.BlockSpec((None, 1, tn), lambda n, m, be, bv: (be[m], 0, n)),
        ],
        out_specs=pl.BlockSpec((bm, tn // 2), lambda n, m, be, bv: (m, n)),
        scratch_shapes=[pltpu.VMEM((D, tn), BF16)],
    )
    return pl.pallas_call(
        _gmm1_kernel,
        grid_spec=grid_spec,
        out_shape=jax.ShapeDtypeStruct((R, F2 // 2), BF16),
        compiler_params=_cparams(("arbitrary", "arbitrary")),
        name="gmm1",
    )(block_expert, block_valid, x_buf, w1, b1)


def _gmm2_kernel(be_ref, bv_ref, a_ref, w_ref, b_ref, o_ref, wb_ref):
    m = pl.program_id(1)
    valid = bv_ref[m] == 1

    @pl.when(jnp.logical_and(valid, _new_expert(be_ref, m)))
    def _():
        half = LANES // 2

        def body(i, c):
            r0 = pl.multiple_of(i * LANES, LANES)
            words = _pack_pair(w_ref[pl.ds(r0, half), :], w_ref[pl.ds(r0 + half, half), :])
            wb_ref[pl.ds(r0, LANES), :] = pltpu.bitcast(words, BF16)
            return c
        lax.fori_loop(0, w_ref.shape[0] // LANES, body, 0)

    @pl.when(valid)
    def _():
        y = jnp.dot(a_ref[...], wb_ref[...], preferred_element_type=F32) + b_ref[...]
        o_ref[...] = _pack_halves(y)

    @pl.when(bv_ref[m] == 0)
    def _():
        o_ref[...] = jnp.zeros_like(o_ref)


def _gmm2(block_expert, block_valid, act, w2, b2, bm, tn):
    R, F = act.shape
    E, _, D = w2.shape
    nb = R // bm
    assert tn % (2 * LANES) == 0
    grid_spec = pltpu.PrefetchScalarGridSpec(
        num_scalar_prefetch=2,
        grid=(D // tn, nb),
        in_specs=[
            pl.BlockSpec((bm, F), lambda n, m, be, bv: (m, 0)),
            pl.BlockSpec((None, F, tn), lambda n, m, be, bv: (be[m], 0, n)),
            pl.BlockSpec((None, 1, tn), lambda n, m, be, bv: (be[m], 0, n)),
        ],
        out_specs=pl.BlockSpec((bm, tn // 2), lambda n, m, be, bv: (m, n)),
        scratch_shapes=[pltpu.VMEM((F, tn), BF16)],
    )
    return pl.pallas_call(
        _gmm2_kernel,
        grid_spec=grid_spec,
        out_shape=jax.ShapeDtypeStruct((R, D // 2), jnp.uint32),
        compiler_params=_cparams(("arbitrary", "arbitrary")),
        name="gmm2",
    )(block_expert, block_valid, act, w2, b2)


def _combine_kernel(dcur_ref, dnext_ref, gate_ref, x1_ref, nw_ref, y_hbm, o_ref, ybuf, sem, *, tm, y_tile):
    i = pl.program_id(0)
    n = pl.num_programs(0)
    slot = lax.rem(i, 2)

    def copy(d_ref, s, r, k):
        return pltpu.make_async_copy(y_hbm.at[pl.ds(d_ref[0, 0, r * TOP_K + k], 1)],
                                     ybuf.at[s, k, pl.ds(r, 1)], sem.at[s])

    def issue(d_ref, s):
        def body(g, c):
            for u in range(ISSUE_UNROLL // TOP_K):
                for k in range(TOP_K):
                    copy(d_ref, s, g * (ISSUE_UNROLL // TOP_K) + u, k).start()
            return c
        lax.fori_loop(0, tm // (ISSUE_UNROLL // TOP_K), body, 0)

    @pl.when(i == 0)
    def _():
        issue(dcur_ref, 0)

    @pl.when(i + 1 < n)
    def _():
        issue(dnext_ref, 1 - slot)

    for k in range(TOP_K):
        pltpu.make_async_copy(y_hbm.at[pl.ds(0, tm)], ybuf.at[slot, k], sem.at[slot]).wait()

    lane = lax.broadcasted_iota(jnp.int32, (tm, LANES), 1)
    gates = gate_ref[...]
    acc = x1_ref[...]
    hw = y_tile // 2
    for k in range(TOP_K):
        gk = jnp.sum(jnp.where(lane == k, gates, 0.0), axis=-1, keepdims=True)
        parts = []
        for jt in range(ybuf.shape[-1] // hw):
            lo, hi = _unpack_halves(ybuf[slot, k, :, jt * hw:(jt + 1) * hw])
            parts += [lo, hi]
        acc = acc + gk * jnp.concatenate(parts, axis=1)
    out = acc * lax.rsqrt(jnp.mean(acc * acc, axis=-1, keepdims=True) + RMS_EPS) * nw_ref[...]
    o_ref[...] = out


def _combine(dest, gates, x1, norm_f_w, y_buf, tm, y_tile):
    T, D = x1.shape
    nt = T // tm
    d3 = dest.reshape(nt, 1, tm * TOP_K)
    return pl.pallas_call(
        functools.partial(_combine_kernel, tm=tm, y_tile=y_tile),
        grid=(nt,),
        in_specs=[
            pl.BlockSpec((1, 1, tm * TOP_K), lambda i: (i, 0, 0), memory_space=pltpu.SMEM),
            pl.BlockSpec((1, 1, tm * TOP_K), lambda i: (jnp.minimum(i + 1, nt - 1), 0, 0),
                         memory_space=pltpu.SMEM),
            pl.BlockSpec((tm, LANES), lambda i: (i, 0)),
            pl.BlockSpec((tm, D), lambda i: (i, 0)),
            pl.BlockSpec((1, D), lambda i: (0, 0)),
            pl.BlockSpec(memory_space=pl.ANY),
        ],
        out_specs=pl.BlockSpec((tm, D), lambda i: (i, 0)),
        out_shape=jax.ShapeDtypeStruct((T, D), F32),
        scratch_shapes=[pltpu.VMEM((2, TOP_K, tm, D // 2), jnp.uint32), pltpu.SemaphoreType.DMA((2,))],
        compiler_params=_cparams(("arbitrary",)),
        name="combine",
    )(d3, d3, gates, x1, norm_f_w.reshape(1, D).astype(F32), y_buf)


def _routing(top_idx, n_experts, bm):
    T = top_idx.shape[0]
    A = T * TOP_K
    flat_e = top_idx.reshape(A)
    onehot = (flat_e[:, None] == jnp.arange(n_experts, dtype=jnp.int32)[None, :]).astype(jnp.int32)
    csum = jnp.cumsum(onehot, axis=0)
    rank = jnp.sum((csum - onehot) * onehot, axis=1)
    counts = csum[-1]
    padded = (counts + bm - 1) // bm * bm
    pend = jnp.cumsum(padded)
    pstart = pend - padded
    dest = jnp.sum(onehot * pstart[None, :], axis=1) + rank
    R = (A + bm - 1) // bm * bm + n_experts * bm
    nb = R // bm
    block_start = jnp.arange(nb, dtype=jnp.int32) * bm
    block_expert = jnp.minimum(jnp.sum((block_start[:, None] >= pend[None, :]).astype(jnp.int32), axis=1),
                               n_experts - 1)
    block_valid = (block_start < pend[-1]).astype(jnp.int32)
    is_last = jnp.any((block_start[:, None] + bm == pend[None, :]) & (padded[None, :] > 0), axis=1)
    zero_fill = (is_last | (block_start >= pend[-1])).astype(jnp.int32)
    return dest.astype(jnp.int32), R, block_expert, block_valid, zero_fill


def _pick(pref, n):
    t = min(pref, n)
    while n % t:
        t //= 2
    return t


def kernel(x, norm1_w, w_in, b_gate, conv_w, A_log, dt_bias, onorm_w, w_branch_a, w_pool,
           pool_scale, w_branch_p, w_out, norm2_w, router_w, router_b, w1, b1, w2, b2, norm_f_w):
    B, S, D = x.shape
    T = B * S
    depth = norm1_w.shape[0]
    x2 = x.reshape(T, D)
    for l in range(depth):
        conv_ch = conv_w.shape[-1]
        v_width = w_branch_a.shape[1]
        qk_width = (conv_ch - v_width) // 2
        n_v = A_log.shape[-1]
        qkvz = conv_ch + v_width
        ab_lo, ab_hi = qkvz, qkvz + 2 * n_v
        E = router_w.shape[-1]

        wl = w_in[l]
        w_first = wl[:, :qkvz].astype(BF16)
        w_second = wl[:, ab_hi:].astype(BF16)
        w_ab = jnp.zeros((D, LANES), BF16).at[:, :2 * n_v].set(wl[:, ab_lo:ab_hi].astype(BF16))
        p_off, ga_off, gp_off = qkvz, qkvz + D, qkvz + 2 * D

        tn_in = _pick(1024, math.gcd(qk_width, v_width, w_second.shape[1]))
        tm_in = _pick(1024, S // 2 if S >= 2 * CONV_HALO else S)
        proj, ab = _in_proj(x2, norm1_w[l], w_first, w_second, w_ab, conv_w[l], qk_width, v_width, S,
                            tm_in, tn_in)
        proj3 = proj.reshape(B, S, -1)
        G = 8 if (qk_width // HEAD) % 8 == 0 else 2
        o_a = _delta(proj3, ab.reshape(B, S, LANES), A_log[l], dt_bias[l], onorm_w[l],
                     qk_width, v_width, G, _pick(512, S))
        o_p = _pool(proj3, w_pool[l], pool_scale[l], p_off, D)
        h = _branch(o_a.reshape(T, v_width), o_p.reshape(T, D), proj,
                    w_branch_a[l].astype(BF16), w_branch_p[l].astype(BF16), b_gate[l],
                    ga_off, gp_off, _pick(1024, T), _pick(512, D))
        x1, xp, idx_pad, gate_pad = _post(x2, h, w_out[l].astype(BF16), norm2_w[l],
                                          router_w[l], router_b[l], _pick(512, T))

        bm = 512
        y_tile = _pick(1024, D)
        dest, n_rows, block_expert, block_valid, zero_fill = _routing(idx_pad[:, :TOP_K], E, bm)
        x_buf = _dispatch_rows(xp, dest, zero_fill, n_rows, _pick(512, T), bm)
        act = _gmm1(block_expert, block_valid, x_buf, w1[l].astype(F32),
                    b1[l][:, None, :].astype(F32), bm, _pick(2048, w1.shape[-1]))
        y_buf = _gmm2(block_expert, block_valid, act, w2[l].astype(F32),
                      b2[l][:, None, :].astype(F32), bm, y_tile)
        is_last = l == depth - 1
        assert is_last, "final norm is fused into the last layer's combine"
        x2 = _combine(dest, gate_pad, x1, norm_f_w, y_buf, _pick(128, T), y_tile)
    return x2.reshape(B, S, D)
```

```python
import functools
import math

import jax
import jax.numpy as jnp
from jax import lax
from jax.experimental import pallas as pl
from jax.experimental.pallas import tpu as pltpu

F32 = jnp.float32
BF16 = jnp.bfloat16

HEAD = 128
CHUNK = 64
CONV_WIDTH = 4
POOL_WINDOWS = (2, 4, 8, 16)
TOP_K = 4
SWIGLU_ALPHA = 1.702
SWIGLU_LIMIT = 7.0
RMS_EPS = 1e-6
L2_EPS = 1e-6
LANES = 128
HIGHEST = lax.Precision.HIGHEST
VMEM_LIMIT = 56 * 1024 * 1024


def _cparams(sem):
    return pltpu.CompilerParams(dimension_semantics=sem, vmem_limit_bytes=VMEM_LIMIT)


def _sigmoid(x):
    return 1.0 / (1.0 + jnp.exp(-x))


def _bdot(a, b):
    return jnp.dot(a.astype(BF16), b.astype(BF16), preferred_element_type=F32)


CONV_HALO = 8


def _inproj_kernel(x_ref, nw_ref, w_ref, w2_ref, wab_ref, cw_ref, o_ref, ab_ref, xn_ref, halo_ref, acc_ref, *,
                   rows, n_q, n_k, n_conv, n_first, tiles_per_seq):
    i = pl.program_id(0)
    j = pl.program_id(1)
    tm, tn = o_ref.shape

    @pl.when(j == 0)
    def _():
        def body(r, c):
            r0 = pl.multiple_of(r * rows, rows)
            x = x_ref[pl.ds(r0, rows), :]
            inv = lax.rsqrt(jnp.mean(x * x, axis=-1, keepdims=True) + RMS_EPS)
            xn_ref[pl.ds(r0, rows), :] = (x * inv * nw_ref[...]).astype(BF16)
            return c
        lax.fori_loop(0, tm // rows, body, 0)
        ab_ref[...] = jnp.dot(xn_ref[...], wab_ref[...], preferred_element_type=F32)

    @pl.when(jnp.logical_and(j >= n_conv, j < n_first))
    def _():
        o_ref[...] = jnp.dot(xn_ref[...], w_ref[...], preferred_element_type=F32).astype(o_ref.dtype)

    @pl.when(j >= n_first)
    def _():
        o_ref[...] = jnp.dot(xn_ref[...], w2_ref[...], preferred_element_type=F32).astype(o_ref.dtype)

    def conv_tile(normalise):
        jc = jnp.minimum(j, n_conv - 1)
        first = lax.rem(i, tiles_per_seq) == 0
        qscale = jnp.where(j < n_q, HEAD ** -0.5, 1.0).astype(F32)
        acc_ref[:CONV_HALO, :] = jnp.where(first, 0.0, halo_ref[jc])
        acc_ref[CONV_HALO:, :] = jnp.dot(xn_ref[...], w_ref[...], preferred_element_type=F32)
        halo_ref[jc] = acc_ref[tm:, :]
        for c in range(tn // LANES):
            sl = slice(c * LANES, (c + 1) * LANES)
            y = acc_ref[CONV_HALO:, sl] * cw_ref[CONV_WIDTH - 1:CONV_WIDTH, sl]
            for t in range(CONV_WIDTH - 1):
                sh = CONV_WIDTH - 1 - t
                y = y + acc_ref[CONV_HALO - sh:CONV_HALO - sh + tm, sl] * cw_ref[t:t + 1, sl]
            y = y * _sigmoid(y)
            if normalise:
                y = y * (lax.rsqrt(jnp.sum(y * y, axis=-1, keepdims=True) + L2_EPS) * qscale)
            o_ref[:, sl] = y.astype(o_ref.dtype)

    @pl.when(j < n_q + n_k)
    def _():
        conv_tile(True)

    @pl.when(jnp.logical_and(j >= n_q + n_k, j < n_conv))
    def _():
        conv_tile(False)


def _in_proj(x2, norm_w, w_first, w_second, w_ab, conv_w, qk_width, v_width, seq_len, tm, tn):
    T, D = x2.shape
    N = w_first.shape[1] + w_second.shape[1]
    rows = min(128, tm)
    assert qk_width % tn == 0 and v_width % tn == 0 and tn % HEAD == 0 and seq_len % tm == 0
    assert CONV_WIDTH - 1 <= CONV_HALO <= tm and w_first.shape[1] % tn == 0
    n_q = n_k = qk_width // tn
    n_conv = n_q + n_k + v_width // tn
    n_first = w_first.shape[1] // tn
    return pl.pallas_call(
        functools.partial(_inproj_kernel, rows=rows, n_q=n_q, n_k=n_k, n_conv=n_conv, n_first=n_first,
                          tiles_per_seq=seq_len // tm),
        grid=(T // tm, N // tn),
        in_specs=[
            pl.BlockSpec((tm, D), lambda i, j: (i, 0)),
            pl.BlockSpec((1, D), lambda i, j: (0, 0)),
            pl.BlockSpec((D, tn), lambda i, j: (0, jnp.minimum(j, n_first - 1))),
            pl.BlockSpec((D, tn), lambda i, j: (0, jnp.maximum(j - n_first, 0))),
            pl.BlockSpec((D, LANES), lambda i, j: (0, 0)),
            pl.BlockSpec((CONV_WIDTH, tn), lambda i, j: (0, jnp.minimum(j, n_conv - 1))),
        ],
        out_specs=[
            pl.BlockSpec((tm, tn), lambda i, j: (i, j)),
            pl.BlockSpec((tm, LANES), lambda i, j: (i, 0)),
        ],
        out_shape=[jax.ShapeDtypeStruct((T, N), BF16), jax.ShapeDtypeStruct((T, LANES), F32)],
        scratch_shapes=[pltpu.VMEM((tm, D), BF16), pltpu.VMEM((n_conv, CONV_HALO, tn), F32),
                        pltpu.VMEM((CONV_HALO + tm, tn), F32)],
        compiler_params=_cparams(("arbitrary", "arbitrary")),
        name="in_proj",
    )(x2, norm_w.reshape(1, D), w_first, w_second, w_ab, conv_w.astype(F32))


def _delta_kernel(q_ref, k_ref, v_ref, z_ref, ab_ref, alog_ref, dtb_ref, onw_ref, o_ref,
                  s_ref, t_ref, a_ref, gc_ref, vl_ref, *, G, H, unroll_a):
    St = q_ref.shape[0]
    C = CHUNK
    WIDE = 4 * C
    NW = G // 2
    t_idx = pl.program_id(2)
    j = pl.program_id(1)

    @pl.when(t_idx == 0)
    def _():
        s_ref[...] = jnp.zeros_like(s_ref)

    row = lax.broadcasted_iota(jnp.int32, (C, WIDE), 0)
    lanew = lax.broadcasted_iota(jnp.int32, (C, WIDE), 1)
    colw = lanew % C
    blk = lanew // C
    incl_w = row >= colw
    strict_w = row > colw
    eye_w = jnp.where(row == colw, 1.0, 0.0).astype(F32)
    first_half = (lax.broadcasted_iota(jnp.int32, (C, 2 * C), 1) < C)
    r2 = lax.broadcasted_iota(jnp.int32, (C, C), 0)
    c2 = lax.broadcasted_iota(jnp.int32, (C, C), 1)
    tril = jnp.where(r2 >= c2, 1.0, 0.0).astype(F32)
    lane = lax.broadcasted_iota(jnp.int32, (C, LANES), 1)
    shift = lax.rem(LANES - 2 * G * j, LANES)

    def pair_cols(x, c0):
        a = jnp.broadcast_to(x[:, c0:c0 + 1], (C, 2 * C))
        b = jnp.broadcast_to(x[:, c0 + 1:c0 + 2], (C, 2 * C))
        return jnp.where(first_half, a, b)

    blk_mask = [jnp.where(blk == e, 1.0, 0.0).astype(BF16) for e in range(4)]

    def block_diag(mw):
        return jnp.concatenate([mw * blk_mask[e] for e in range(4)], axis=0)

    def prep(i, carry):
        chunks = [i * unroll_a + u for u in range(unroll_a)]
        r0s = [pl.multiple_of(c * C, C) for c in chunks]
        gcs, valss = [], []
        for r0 in r0s:
            ab = ab_ref[pl.ds(r0, C), :]
            sp_in = ab + dtb_ref[...]
            softplus = jnp.maximum(sp_in, 0.0) + jnp.log(1.0 + jnp.exp(-jnp.abs(sp_in)))
            gval = -jnp.exp(alog_ref[...]) * softplus
            vals = pltpu.roll(jnp.where(lane < H, gval, _sigmoid(ab)), shift, 1)
            gc = jnp.dot(tril, vals, preferred_element_type=F32, precision=HIGHEST)
            gc_ref[pl.ds(r0, C), :] = gc
            vl_ref[pl.ds(r0, C), :] = vals
            gcs.append(gc)
            valss.append(vals)
        qbs = [q_ref[pl.ds(r0, C), h * HEAD:(h + 1) * HEAD] for r0 in r0s for h in range(G)]
        kbs = [k_ref[pl.ds(r0, C), h * HEAD:(h + 1) * HEAD] for r0 in r0s for h in range(G)]
        qkk = [lax.dot_general(jnp.concatenate([qb, kb], axis=0), jnp.concatenate([kb, kb], axis=0),
                               (((1,), (1,)), ((), ())), preferred_element_type=F32)
               for qb, kb in zip(qbs, kbs)]
        ps, ms = [], []
        for u, r0 in enumerate(r0s):
            for w in range(NW):
                top, bot = qkk[u * G + 2 * w], qkk[u * G + 2 * w + 1]
                qk_w = jnp.concatenate([top[:C], bot[:C]], axis=1)
                kk_w = jnp.concatenate([top[C:], bot[C:]], axis=1)
                gcol = jnp.concatenate([pair_cols(gcs[u], 4 * w), pair_cols(gcs[u], 4 * w + 2)], axis=1)
                bcol = jnp.concatenate([pair_cols(valss[u], H + 4 * w),
                                        pair_cols(valss[u], H + 4 * w + 2)], axis=1)
                grow = jnp.sum(gcol * eye_w, axis=0, keepdims=True)
                dec = jnp.exp(jnp.where(incl_w, gcol - grow, -jnp.inf))
                lmat = jnp.where(strict_w, kk_w * dec * bcol, 0.0)
                a_ref[pl.ds(r0, C), w * WIDE:(w + 1) * WIDE] = (qk_w * dec).astype(BF16)
                ps.append(eye_w - lmat)
                ms.append(lmat)
        ms = [_bdot(m, block_diag(m.astype(BF16))) for m in ms]
        for _ in range(int(math.log2(C)) - 2):
            outs = [jnp.dot(jnp.concatenate([p, m], axis=0).astype(BF16), block_diag(m.astype(BF16)),
                            preferred_element_type=F32) for p, m in zip(ps, ms)]
            ps = [p + o[:C] for p, o in zip(ps, outs)]
            ms = [o[C:] for o in outs]
        outs = [_bdot(p, block_diag(m.astype(BF16))) for p, m in zip(ps, ms)]
        for u, r0 in enumerate(r0s):
            for w in range(NW):
                t_ref[pl.ds(r0, C), w * WIDE:(w + 1) * WIDE] = (ps[u * NW + w] + outs[u * NW + w]).astype(BF16)
        return carry

    lax.fori_loop(0, St // (C * unroll_a), prep, 0)

    def scan_chunk(c, carry):
        r0 = pl.multiple_of(c * C, C)
        gc = gc_ref[pl.ds(r0, C), :]
        vals = vl_ref[pl.ds(r0, C), :]
        kbs = [k_ref[pl.ds(r0, C), h * HEAD:(h + 1) * HEAD] for h in range(G)]
        ks = [jnp.dot(jnp.concatenate([q_ref[pl.ds(r0, C), h * HEAD:(h + 1) * HEAD], kbs[h]], axis=0),
                      jnp.concatenate([s_ref[2 * h], s_ref[2 * h + 1]], axis=1).astype(BF16),
                      preferred_element_type=F32) for h in range(G)]
        egs, gcols, rhss = [], [], []
        for hv in range(2 * G):
            h, e = divmod(hv, 2)
            gcol = jnp.broadcast_to(gc[:, hv:hv + 1], (C, LANES))
            bcol = jnp.broadcast_to(vals[:, H + hv:H + hv + 1], (C, LANES))
            eg = jnp.exp(gcol)
            v = v_ref[pl.ds(r0, C), hv * HEAD:(hv + 1) * HEAD].astype(F32)
            rhss.append((bcol * (v - eg * ks[h][C:, e * HEAD:(e + 1) * HEAD])).astype(BF16))
            egs.append(eg)
            gcols.append(gcol)
        vnews = [jnp.dot(t_ref[pl.ds(r0, C), hv * C:(hv + 1) * C], rhss[hv],
                         preferred_element_type=F32) for hv in range(2 * G)]
        intra = [jnp.dot(a_ref[pl.ds(r0, C), hv * C:(hv + 1) * C], vnews[hv].astype(BF16),
                         preferred_element_type=F32) for hv in range(2 * G)]
        vnd = [(jnp.exp(gcols[hv][C - 1:C, :] - gcols[hv]) * vnews[hv]).astype(BF16)
               for hv in range(2 * G)]
        upd = [lax.dot_general(kbs[h], jnp.concatenate([vnd[2 * h], vnd[2 * h + 1]], axis=1),
                               (((0,), (0,)), ((), ())), preferred_element_type=F32)
               for h in range(G)]
        for hv in range(2 * G):
            h, e = divmod(hv, 2)
            s_ref[hv] = (s_ref[hv] * jnp.exp(gcols[hv][C - 1:C, :])
                         + upd[h][:, e * HEAD:(e + 1) * HEAD])
            o = egs[hv] * ks[h][:C, e * HEAD:(e + 1) * HEAD] + intra[hv]
            zz = z_ref[pl.ds(r0, C), hv * HEAD:(hv + 1) * HEAD].astype(F32)
            on = o * lax.rsqrt(jnp.mean(o * o, axis=-1, keepdims=True) + RMS_EPS)
            on = on * onw_ref[...] * (zz * _sigmoid(zz))
            o_ref[pl.ds(r0, C), hv * HEAD:(hv + 1) * HEAD] = on.astype(o_ref.dtype)
        return carry

    lax.fori_loop(0, St // C, scan_chunk, 0)


def _delta(proj3, ab3, a_log, dt_bias, onorm_w, qk_width, v_width, G, St):
    B, S, _ = proj3.shape
    H = a_log.shape[0]
    n_groups = qk_width // (HEAD * G)
    qw, vw = HEAD * G, 2 * HEAD * G
    k_blk0 = qk_width // qw
    v_blk0 = (2 * qk_width) // vw
    z_blk0 = (2 * qk_width + v_width) // vw
    assert (2 * qk_width) % vw == 0 and (2 * qk_width + v_width) % vw == 0
    assert 2 * H <= LANES and S % CHUNK == 0
    assert G % 2 == 0 and S % St == 0 and St % CHUNK == 0
    pad = lambda p: jnp.zeros((1, LANES), F32).at[0, :H].set(p.astype(F32))
    return pl.pallas_call(
        functools.partial(_delta_kernel, G=G, H=H, unroll_a=_pick(4, St // CHUNK)),
        grid=(B, n_groups, S // St),
        in_specs=[
            pl.BlockSpec((None, St, qw), lambda b, j, t: (b, t, j)),
            pl.BlockSpec((None, St, qw), lambda b, j, t: (b, t, k_blk0 + j)),
            pl.BlockSpec((None, St, vw), lambda b, j, t: (b, t, v_blk0 + j)),
            pl.BlockSpec((None, St, vw), lambda b, j, t: (b, t, z_blk0 + j)),
            pl.BlockSpec((None, St, LANES), lambda b, j, t: (b, t, 0)),
            pl.BlockSpec((1, LANES), lambda b, j, t: (0, 0)),
            pl.BlockSpec((1, LANES), lambda b, j, t: (0, 0)),
            pl.BlockSpec((1, HEAD), lambda b, j, t: (0, 0)),
        ],
        out_specs=pl.BlockSpec((None, St, vw), lambda b, j, t: (b, t, j)),
        out_shape=jax.ShapeDtypeStruct((B, S, v_width), BF16),
        scratch_shapes=[
            pltpu.VMEM((2 * G, HEAD, HEAD), F32),
            pltpu.VMEM((St, qw), BF16),
            pltpu.VMEM((St, qw), BF16),
            pltpu.VMEM((St, LANES), F32),
            pltpu.VMEM((St, LANES), F32),
        ],
        compiler_params=_cparams(("arbitrary", "arbitrary", "arbitrary")),
        name="delta",
    )(proj3, proj3, proj3, proj3, ab3, pad(a_log), pad(dt_bias), onorm_w.reshape(1, HEAD).astype(F32))


POOL_HALO = 16


def _pool_kernel(p_ref, w_ref, sc_ref, o_ref, pad_ref, *, rt):
    S = p_ref.shape[0]
    g = pl.program_id(1)
    pad_ref[0:POOL_HALO, :] = jnp.zeros((POOL_HALO, pad_ref.shape[1]), F32)
    pad_ref[POOL_HALO:, :] = p_ref[...].astype(F32)
    win = jnp.left_shift(2, g).astype(F32)

    def body(r, c):
        r0 = pl.multiple_of(r * rt, rt)
        x = pad_ref[pl.ds(r0, rt + POOL_HALO), :]
        y1 = x + pltpu.roll(x, 1, 0)
        y2 = y1 + pltpu.roll(y1, 2, 0)
        y3 = y2 + pltpu.roll(y2, 4, 0)
        y4 = y3 + pltpu.roll(y3, 8, 0)
        ysel = jnp.where(g == 0, y1, jnp.where(g == 1, y2, jnp.where(g == 2, y3, y4)))
        pos = (r0 + 1 + lax.broadcasted_iota(jnp.int32, (rt, 1), 0)).astype(F32)
        cnt = jnp.minimum(pos, win)
        mixed = ysel[POOL_HALO:] / cnt - x[POOL_HALO:]
        out = _bdot(mixed, w_ref[...]) * sc_ref[...]
        o_ref[pl.ds(r0, rt), :] = out.astype(o_ref.dtype)
        return c

    lax.fori_loop(0, S // rt, body, 0)


def _pool(proj3, w_pool, pool_scale, p_off, d_model):
    B, S, _ = proj3.shape
    ng, cg, _ = w_pool.shape
    assert POOL_WINDOWS == tuple(2 << i for i in range(ng)) and max(POOL_WINDOWS) <= POOL_HALO
    assert p_off % cg == 0
    blk0 = p_off // cg
    rt = min(256, S)
    return pl.pallas_call(
        functools.partial(_pool_kernel, rt=rt),
        grid=(B, ng),
        in_specs=[
            pl.BlockSpec((None, S, cg), lambda b, g: (b, 0, blk0 + g)),
            pl.BlockSpec((None, cg, cg), lambda b, g: (g, 0, 0)),
            pl.BlockSpec((1, cg), lambda b, g: (0, g)),
        ],
        out_specs=pl.BlockSpec((None, S, cg), lambda b, g: (b, 0, g)),
        out_shape=jax.ShapeDtypeStruct((B, S, d_model), BF16),
        scratch_shapes=[pltpu.VMEM((S + POOL_HALO, cg), F32)],
        compiler_params=_cparams(("arbitrary", "arbitrary")),
        name="pool",
    )(proj3, w_pool.astype(BF16), pool_scale.reshape(1, -1).astype(F32))


def _branch_kernel(oa_ref, op_ref, wa_ref, wp_ref, ga_ref, gp_ref, bga_ref, bgp_ref, h_ref):
    a = jnp.dot(oa_ref[...], wa_ref[...], preferred_element_type=F32)
    p = jnp.dot(op_ref[...], wp_ref[...], preferred_element_type=F32)
    ga = _sigmoid(ga_ref[...].astype(F32) + bga_ref[...])
    gp = _sigmoid(gp_ref[...].astype(F32) + bgp_ref[...])
    h_ref[...] = (ga * a + gp * p).astype(h_ref.dtype)


def _branch(o_a, o_p, proj, w_a, w_p, b_gate, ga_off, gp_off, tm, tn):
    T, V = o_a.shape
    D = w_a.shape[1]
    assert ga_off % tn == 0 and gp_off % tn == 0
    ga0, gp0 = ga_off // tn, gp_off // tn
    nd = D // tn
    bg = b_gate.reshape(1, 2 * D).astype(F32)
    return pl.pallas_call(
        _branch_kernel,
        grid=(T // tm, nd),
        in_specs=[
            pl.BlockSpec((tm, V), lambda i, j: (i, 0)),
            pl.BlockSpec((tm, D), lambda i, j: (i, 0)),
            pl.BlockSpec((V, tn), lambda i, j: (0, j)),
            pl.BlockSpec((D, tn), lambda i, j: (0, j)),
            pl.BlockSpec((tm, tn), lambda i, j: (i, ga0 + j)),
            pl.BlockSpec((tm, tn), lambda i, j: (i, gp0 + j)),
            pl.BlockSpec((1, tn), lambda i, j: (0, j)),
            pl.BlockSpec((1, tn), lambda i, j: (0, nd + j)),
        ],
        out_specs=pl.BlockSpec((tm, tn), lambda i, j: (i, j)),
        out_shape=jax.ShapeDtypeStruct((T, D), BF16),
        compiler_params=_cparams(("arbitrary", "arbitrary")),
        name="branch",
    )(o_a, o_p, w_a, w_p, proj, proj, bg, bg)


def _pack_pair(lo, hi):
    lo = lax.bitcast_convert_type(lo.astype(BF16).astype(F32), jnp.uint32)
    hi = lax.bitcast_convert_type(hi.astype(BF16).astype(F32), jnp.uint32)
    return jnp.right_shift(lo, jnp.uint32(16)) | (hi & jnp.uint32(0xFFFF0000))


def _pack_halves(x):
    half = x.shape[1] // 2
    return _pack_pair(x[:, :half], x[:, half:])


def _unpack_halves(u):
    lo = lax.bitcast_convert_type(jnp.left_shift(u, jnp.uint32(16)), F32)
    hi = lax.bitcast_convert_type(u & jnp.uint32(0xFFFF0000), F32)
    return lo, hi


def _split_bf16(x):
    hi = x.astype(BF16)
    return hi, (x - hi.astype(F32)).astype(BF16)


def _post_kernel(x_ref, h_ref, wo_ref, nw_ref, rwh_ref, rwl_ref, rb_ref,
                 x1_ref, xp_ref, idx_ref, gate_ref, *, n_experts):
    x1 = x_ref[...] + jnp.dot(h_ref[...], wo_ref[...], preferred_element_type=F32)
    x1_ref[...] = x1
    xn = x1 * lax.rsqrt(jnp.mean(x1 * x1, axis=-1, keepdims=True) + RMS_EPS) * nw_ref[...]
    xp_ref[...] = _pack_halves(xn)
    xh, xl = _split_bf16(xn)
    logits = (jnp.dot(xh, rwh_ref[...], preferred_element_type=F32)
              + jnp.dot(xh, rwl_ref[...], preferred_element_type=F32)
              + jnp.dot(xl, rwh_ref[...], preferred_element_type=F32) + rb_ref[...])
    lane = lax.broadcasted_iota(jnp.int32, logits.shape, 1)
    work = jnp.where(lane < n_experts, logits, -jnp.inf)
    vals, idxs = [], []
    for _ in range(TOP_K):
        m = jnp.max(work, axis=-1, keepdims=True)
        idx = jnp.min(jnp.where(work == m, lane, LANES), axis=-1, keepdims=True)
        vals.append(m)
        idxs.append(idx)
        work = jnp.where(lane == idx, -jnp.inf, work)
    ex = [jnp.exp(v - vals[0]) for v in vals]
    den = ex[0]
    for e in ex[1:]:
        den = den + e
    idx_out = jnp.zeros(logits.shape, jnp.int32)
    gate_out = jnp.zeros(logits.shape, F32)
    for k in range(TOP_K):
        idx_out = jnp.where(lane == k, idxs[k], idx_out)
        gate_out = jnp.where(lane == k, ex[k] / den, gate_out)
    idx_ref[...] = idx_out
    gate_ref[...] = gate_out


def _post(x2, h, w_out, norm2_w, router_w, router_b, tm):
    T, D = x2.shape
    E = router_w.shape[1]
    rw = jnp.zeros((D, LANES), F32).at[:, :E].set(router_w.astype(F32))
    rwh, rwl = _split_bf16(rw)
    rb = jnp.zeros((1, LANES), F32).at[0, :E].set(router_b.astype(F32))
    return pl.pallas_call(
        functools.partial(_post_kernel, n_experts=E),
        grid=(T // tm,),
        in_specs=[
            pl.BlockSpec((tm, D), lambda i: (i, 0)),
            pl.BlockSpec((tm, D), lambda i: (i, 0)),
            pl.BlockSpec((D, D), lambda i: (0, 0)),
            pl.BlockSpec((1, D), lambda i: (0, 0)),
            pl.BlockSpec((D, LANES), lambda i: (0, 0)),
            pl.BlockSpec((D, LANES), lambda i: (0, 0)),
            pl.BlockSpec((1, LANES), lambda i: (0, 0)),
        ],
        out_specs=[
            pl.BlockSpec((tm, D), lambda i: (i, 0)),
            pl.BlockSpec((tm, D // 2), lambda i: (i, 0)),
            pl.BlockSpec((tm, LANES), lambda i: (i, 0)),
            pl.BlockSpec((tm, LANES), lambda i: (i, 0)),
        ],
        out_shape=[
            jax.ShapeDtypeStruct((T, D), F32),
            jax.ShapeDtypeStruct((T, D // 2), jnp.uint32),
            jax.ShapeDtypeStruct((T, LANES), jnp.int32),
            jax.ShapeDtypeStruct((T, LANES), F32),
        ],
        compiler_params=_cparams(("arbitrary",)),
        name="post",
    )(x2, h, w_out, norm2_w.reshape(1, D).astype(F32), rwh, rwl, rb)


ISSUE_UNROLL = 8


def _dispatch_kernel(zf_ref, dest_ref, xp_ref, o_hbm, zbuf, sem, zsem, *, tm, bm):
    @pl.when(pl.program_id(0) == 0)
    def _():
        zbuf[...] = jnp.zeros_like(zbuf)
        nb = o_hbm.shape[0] // bm

        def zcopy(m):
            return pltpu.make_async_copy(zbuf, o_hbm.at[pl.ds(pl.multiple_of(m * bm, bm), bm)], zsem)

        def zstart(m, c):
            @pl.when(zf_ref[m] == 1)
            def _():
                zcopy(m).start()
            return c

        def zwait(m, c):
            @pl.when(zf_ref[m] == 1)
            def _():
                zcopy(m).wait()
            return c

        lax.fori_loop(0, nb, zstart, 0)
        lax.fori_loop(0, nb, zwait, 0)

    def body(g, c):
        for u in range(ISSUE_UNROLL // TOP_K):
            r = g * (ISSUE_UNROLL // TOP_K) + u
            for k in range(TOP_K):
                pltpu.make_async_copy(xp_ref.at[pl.ds(r, 1)],
                                      o_hbm.at[pl.ds(dest_ref[0, 0, r * TOP_K + k], 1)], sem).start()
        return c
    lax.fori_loop(0, tm // (ISSUE_UNROLL // TOP_K), body, 0)
    for _ in range(TOP_K):
        pltpu.make_async_copy(xp_ref, o_hbm.at[pl.ds(0, tm)], sem).wait()


def _dispatch_rows(xp, dest, zero_fill, n_rows, tm, bm):
    T, W = xp.shape
    nt = T // tm
    grid_spec = pltpu.PrefetchScalarGridSpec(
        num_scalar_prefetch=1,
        grid=(nt,),
        in_specs=[
            pl.BlockSpec((1, 1, tm * TOP_K), lambda i, zf: (i, 0, 0), memory_space=pltpu.SMEM),
            pl.BlockSpec((tm, W), lambda i, zf: (i, 0)),
        ],
        out_specs=pl.BlockSpec(memory_space=pl.ANY),
        scratch_shapes=[pltpu.VMEM((bm, W), xp.dtype), pltpu.SemaphoreType.DMA(()),
                        pltpu.SemaphoreType.DMA(())],
    )
    return pl.pallas_call(
        functools.partial(_dispatch_kernel, tm=tm, bm=bm),
        grid_spec=grid_spec,
        out_shape=jax.ShapeDtypeStruct((n_rows, W), xp.dtype),
        compiler_params=_cparams(("arbitrary",)),
        name="dispatch",
    )(zero_fill, dest.reshape(nt, 1, tm * TOP_K), xp)


def _new_expert(be_ref, m):
    return jnp.logical_or(m == 0, be_ref[m] != be_ref[jnp.maximum(m - 1, 0)])


CAST_ROWS = 256


def _gmm1_kernel(be_ref, bv_ref, x_ref, w_ref, b_ref, o_ref, wb_ref):
    m = pl.program_id(1)
    valid = bv_ref[m] == 1

    @pl.when(jnp.logical_and(valid, _new_expert(be_ref, m)))
    def _():
        def body(r, c):
            r0 = pl.multiple_of(r * CAST_ROWS, CAST_ROWS)
            wb_ref[pl.ds(r0, CAST_ROWS), :] = w_ref[pl.ds(r0, CAST_ROWS), :].astype(BF16)
            return c
        lax.fori_loop(0, w_ref.shape[0] // CAST_ROWS, body, 0)

    @pl.when(valid)
    def _():
        even = lax.broadcasted_iota(jnp.int32, (x_ref.shape[0], LANES), 1) % 2 == 0
        lo, hi = _unpack_halves(x_ref[...])
        x = jnp.concatenate([lo.astype(BF16), hi.astype(BF16)], axis=1)

        def act_block(hblk):
            x_glu = jnp.minimum(hblk, SWIGLU_LIMIT)
            x_lin = jnp.clip(hblk, -SWIGLU_LIMIT, SWIGLU_LIMIT) + 1.0
            return x_glu * _sigmoid(SWIGLU_ALPHA * x_glu) * pltpu.roll(x_lin, LANES - 1, 1)

        for i in range(o_ref.shape[1] // LANES):
            sl = slice(2 * i * LANES, 2 * (i + 1) * LANES)
            hs = jnp.dot(x, wb_ref[:, sl], preferred_element_type=F32) + b_ref[:, sl]
            packed = jnp.where(even, act_block(hs[:, :LANES]), pltpu.roll(act_block(hs[:, LANES:]), 1, 1))
            o_ref[:, i * LANES:(i + 1) * LANES] = packed.astype(o_ref.dtype)

    @pl.when(bv_ref[m] == 0)
    def _():
        o_ref[...] = jnp.zeros_like(o_ref)


def _gmm1(block_expert, block_valid, x_buf, w1, b1, bm, tn):
    R, half = x_buf.shape
    E, D, F2 = w1.shape
    nb = R // bm
    assert D == 2 * half and D % CAST_ROWS == 0 and tn % (2 * LANES) == 0
    grid_spec = pltpu.PrefetchScalarGridSpec(
        num_scalar_prefetch=2,
        grid=(F2 // tn, nb),
        in_specs=[
            pl.BlockSpec((bm, half), lambda n, m, be, bv: (m, 0)),
            pl.BlockSpec((None, D, tn), lambda n, m, be, bv: (be[m], 0, n)),
            pl.BlockSpec((None, 1, tn), lambda n, m, be, bv: (be[m], 0, n)),
        ],
        out_specs=pl.BlockSpec((bm, tn // 2), lambda n, m, be, bv: (m, n)),
        scratch_shapes=[pltpu.VMEM((D, tn), BF16)],
    )
    return pl.pallas_call(
        _gmm1_kernel,
        grid_spec=grid_spec,
        out_shape=jax.ShapeDtypeStruct((R, F2 // 2), BF16),
        compiler_params=_cparams(("arbitrary", "arbitrary")),
        name="gmm1",
    )(block_expert, block_valid, x_buf, w1, b1)


def _gmm2_kernel(be_ref, bv_ref, a_ref, w_ref, b_ref, o_ref, wb_ref):
    m = pl.program_id(1)
    valid = bv_ref[m] == 1

    @pl.when(jnp.logical_and(valid, _new_expert(be_ref, m)))
    def _():
        half = LANES // 2

        def body(i, c):
            r0 = pl.multiple_of(i * LANES, LANES)
            words = _pack_pair(w_ref[pl.ds(r0, half), :], w_ref[pl.ds(r0 + half, half), :])
            wb_ref[pl.ds(r0, LANES), :] = pltpu.bitcast(words, BF16)
            return c
        lax.fori_loop(0, w_ref.shape[0] // LANES, body, 0)

    @pl.when(valid)
    def _():
        y = jnp.dot(a_ref[...], wb_ref[...], preferred_element_type=F32) + b_ref[...]
        o_ref[...] = _pack_halves(y)

    @pl.when(bv_ref[m] == 0)
    def _():
        o_ref[...] = jnp.zeros_like(o_ref)


def _gmm2(block_expert, block_valid, act, w2, b2, bm, tn):
    R, F = act.shape
    E, _, D = w2.shape
    nb = R // bm
    assert tn % (2 * LANES) == 0
    grid_spec = pltpu.PrefetchScalarGridSpec(
        num_scalar_prefetch=2,
        grid=(D // tn, nb),
        in_specs=[
            pl.BlockSpec((bm, F), lambda n, m, be, bv: (m, 0)),
            pl.BlockSpec((None, F, tn), lambda n, m, be, bv: (be[m], 0, n)),
            pl.BlockSpec((None, 1, tn), lambda n, m, be, bv: (be[m], 0, n)),
        ],
        out_specs=pl.BlockSpec((bm, tn // 2), lambda n, m, be, bv: (m, n)),
        scratch_shapes=[pltpu.VMEM((F, tn), BF16)],
    )
    return pl.pallas_call(
        _gmm2_kernel,
        grid_spec=grid_spec,
        out_shape=jax.ShapeDtypeStruct((R, D // 2), jnp.uint32),
        compiler_params=_cparams(("arbitrary", "arbitrary")),
        name="gmm2",
    )(block_expert, block_valid, act, w2, b2)


def _combine_kernel(dcur_ref, dnext_ref, gate_ref, x1_ref, nw_ref, y_hbm, o_ref, ybuf, sem, *, tm, y_tile):
    i = pl.program_id(0)
    n = pl.num_programs(0)
    slot = lax.rem(i, 2)

    def copy(d_ref, s, r, k):
        return pltpu.make_async_copy(y_hbm.at[pl.ds(d_ref[0, 0, r * TOP_K + k], 1)],
                                     ybuf.at[s, k, pl.ds(r, 1)], sem.at[s])

    def issue(d_ref, s):
        def body(g, c):
            for u in range(ISSUE_UNROLL // TOP_K):
                for k in range(TOP_K):
                    copy(d_ref, s, g * (ISSUE_UNROLL // TOP_K) + u, k).start()
            return c
        lax.fori_loop(0, tm // (ISSUE_UNROLL // TOP_K), body, 0)

    @pl.when(i == 0)
    def _():
        issue(dcur_ref, 0)

    @pl.when(i + 1 < n)
    def _():
        issue(dnext_ref, 1 - slot)

    for k in range(TOP_K):
        pltpu.make_async_copy(y_hbm.at[pl.ds(0, tm)], ybuf.at[slot, k], sem.at[slot]).wait()

    lane = lax.broadcasted_iota(jnp.int32, (tm, LANES), 1)
    gates = gate_ref[...]
    acc = x1_ref[...]
    hw = y_tile // 2
    for k in range(TOP_K):
        gk = jnp.sum(jnp.where(lane == k, gates, 0.0), axis=-1, keepdims=True)
        parts = []
        for jt in range(ybuf.shape[-1] // hw):
            lo, hi = _unpack_halves(ybuf[slot, k, :, jt * hw:(jt + 1) * hw])
            parts += [lo, hi]
        acc = acc + gk * jnp.concatenate(parts, axis=1)
    out = acc * lax.rsqrt(jnp.mean(acc * acc, axis=-1, keepdims=True) + RMS_EPS) * nw_ref[...]
    o_ref[...] = out


def _combine(dest, gates, x1, norm_f_w, y_buf, tm, y_tile):
    T, D = x1.shape
    nt = T // tm
    d3 = dest.reshape(nt, 1, tm * TOP_K)
    return pl.pallas_call(
        functools.partial(_combine_kernel, tm=tm, y_tile=y_tile),
        grid=(nt,),
        in_specs=[
            pl.BlockSpec((1, 1, tm * TOP_K), lambda i: (i, 0, 0), memory_space=pltpu.SMEM),
            pl.BlockSpec((1, 1, tm * TOP_K), lambda i: (jnp.minimum(i + 1, nt - 1), 0, 0),
                         memory_space=pltpu.SMEM),
            pl.BlockSpec((tm, LANES), lambda i: (i, 0)),
            pl.BlockSpec((tm, D), lambda i: (i, 0)),
            pl.BlockSpec((1, D), lambda i: (0, 0)),
            pl.BlockSpec(memory_space=pl.ANY),
        ],
        out_specs=pl.BlockSpec((tm, D), lambda i: (i, 0)),
        out_shape=jax.ShapeDtypeStruct((T, D), F32),
        scratch_shapes=[pltpu.VMEM((2, TOP_K, tm, D // 2), jnp.uint32), pltpu.SemaphoreType.DMA((2,))],
        compiler_params=_cparams(("arbitrary",)),
        name="combine",
    )(d3, d3, gates, x1, norm_f_w.reshape(1, D).astype(F32), y_buf)


def _routing(top_idx, n_experts, bm):
    T = top_idx.shape[0]
    A = T * TOP_K
    flat_e = top_idx.reshape(A)
    onehot = (flat_e[:, None] == jnp.arange(n_experts, dtype=jnp.int32)[None, :]).astype(jnp.int32)
    csum = jnp.cumsum(onehot, axis=0)
    rank = jnp.sum((csum - onehot) * onehot, axis=1)
    counts = csum[-1]
    padded = (counts + bm - 1) // bm * bm
    pend = jnp.cumsum(padded)
    pstart = pend - padded
    dest = jnp.sum(onehot * pstart[None, :], axis=1) + rank
    R = (A + bm - 1) // bm * bm + n_experts * bm
    nb = R // bm
    block_start = jnp.arange(nb, dtype=jnp.int32) * bm
    block_expert = jnp.minimum(jnp.sum((block_start[:, None] >= pend[None, :]).astype(jnp.int32), axis=1),
                               n_experts - 1)
    block_valid = (block_start < pend[-1]).astype(jnp.int32)
    is_last = jnp.any((block_start[:, None] + bm == pend[None, :]) & (padded[None, :] > 0), axis=1)
    zero_fill = (is_last | (block_start >= pend[-1])).astype(jnp.int32)
    return dest.astype(jnp.int32), R, block_expert, block_valid, zero_fill


def _pick(pref, n):
    t = min(pref, n)
    while n % t:
        t //= 2
    return t


def kernel(x, norm1_w, w_in, b_gate, conv_w, A_log, dt_bias, onorm_w, w_branch_a, w_pool,
           pool_scale, w_branch_p, w_out, norm2_w, router_w, router_b, w1, b1, w2, b2, norm_f_w):
    B, S, D = x.shape
    T = B * S
    depth = norm1_w.shape[0]
    x2 = x.reshape(T, D)
    for l in range(depth):
        conv_ch = conv_w.shape[-1]
        v_width = w_branch_a.shape[1]
        qk_width = (conv_ch - v_width) // 2
        n_v = A_log.shape[-1]
        qkvz = conv_ch + v_width
        ab_lo, ab_hi = qkvz, qkvz + 2 * n_v
        E = router_w.shape[-1]

        wl = w_in[l]
        w_first = wl[:, :qkvz].astype(BF16)
        w_second = wl[:, ab_hi:].astype(BF16)
        w_ab = jnp.zeros((D, LANES), BF16).at[:, :2 * n_v].set(wl[:, ab_lo:ab_hi].astype(BF16))
        p_off, ga_off, gp_off = qkvz, qkvz + D, qkvz + 2 * D

        tn_in = _pick(1024, math.gcd(qk_width, v_width, w_second.shape[1]))
        tm_in = _pick(1024, S // 2 if S >= 2 * CONV_HALO else S)
        proj, ab = _in_proj(x2, norm1_w[l], w_first, w_second, w_ab, conv_w[l], qk_width, v_width, S,
                            tm_in, tn_in)
        proj3 = proj.reshape(B, S, -1)
        G = 8 if (qk_width // HEAD) % 8 == 0 else 2
        o_a = _delta(proj3, ab.reshape(B, S, LANES), A_log[l], dt_bias[l], onorm_w[l],
                     qk_width, v_width, G, _pick(512, S))
        o_p = _pool(proj3, w_pool[l], pool_scale[l], p_off, D)
        h = _branch(o_a.reshape(T, v_width), o_p.reshape(T, D), proj,
                    w_branch_a[l].astype(BF16), w_branch_p[l].astype(BF16), b_gate[l],
                    ga_off, gp_off, _pick(1024, T), _pick(512, D))
        x1, xp, idx_pad, gate_pad = _post(x2, h, w_out[l].astype(BF16), norm2_w[l],
                                          router_w[l], router_b[l], _pick(512, T))

        bm = 512
        y_tile = _pick(2048, D)
        dest, n_rows, block_expert, block_valid, zero_fill = _routing(idx_pad[:, :TOP_K], E, bm)
        x_buf = _dispatch_rows(xp, dest, zero_fill, n_rows, _pick(512, T), bm)
        act = _gmm1(block_expert, block_valid, x_buf, w1[l].astype(F32),
                    b1[l][:, None, :].astype(F32), bm, _pick(2048, w1.shape[-1]))
        y_buf = _gmm2(block_expert, block_valid, act, w2[l].astype(F32),
                      b2[l][:, None, :].astype(F32), bm, y_tile)
        is_last = l == depth - 1
        assert is_last, "final norm is fused into the last layer's combine"
        x2 = _combine(dest, gate_pad, x1, norm_f_w, y_buf, _pick(128, T), y_tile)
    return x2.reshape(B, S, D)
```
